```python
import math
import jax, jax.numpy as jnp
from jax import lax
import numpy as np

D_MODEL = 1024
BATCH = 2
SEQ = 16384
DEPTH = 4

CHUNK = 64
Q_BLOCK = 128
LRU_WIDTH = 256
LRU_HEADS = 4
LRU_HEAD_DIM = LRU_WIDTH // LRU_HEADS
CONV_WIDTH = 4
LRU_C = 8.0
SG_WIDTH = 256
SG_GROUPS = 4
SG_GROUP_DIM = SG_WIDTH // SG_GROUPS
SG_LEN = 128
ATTN_HEADS = 4
ATTN_QK_DIM = 64
ATTN_V_DIM = 2 * ATTN_QK_DIM
ATTN_QK_TOTAL = ATTN_HEADS * 2 * ATTN_QK_DIM
ATTN_WIDTH = ATTN_HEADS * ATTN_V_DIM
N_BUCKETS = 32
MAX_DISTANCE = 2048
N_BRANCH = 3
FFN_HIDDEN = -(-8 * D_MODEL // (3 * 256)) * 256
EPS = 1e-6

_IN_SIZES = (LRU_WIDTH, LRU_WIDTH, SG_WIDTH, SG_WIDTH, ATTN_QK_TOTAL, ATTN_QK_TOTAL, ATTN_WIDTH, N_BRANCH * D_MODEL)
IN_COLS = sum(_IN_SIZES)
IN_SPLITS = tuple(int(s) for s in np.cumsum(_IN_SIZES)[:-1])

kernel_name = "hybrid_rglru_sgu_diffattn_trunk"


def rms_norm(x, g):
    x32 = x.astype(jnp.float32)
    y = x32 * lax.rsqrt(jnp.mean(x32 * x32, axis=-1, keepdims=True) + EPS)
    return (y * g.astype(jnp.float32)).astype(x.dtype)


def layer_norm(x, g, b):
    x32 = x.astype(jnp.float32)
    mu = jnp.mean(x32, axis=-1, keepdims=True)
    var = jnp.mean(jnp.square(x32 - mu), axis=-1, keepdims=True)
    y = (x32 - mu) * lax.rsqrt(var + EPS)
    return (y * g.astype(jnp.float32) + b.astype(jnp.float32)).astype(x.dtype)


def rglru_branch(xa, ga, conv_w, conv_b, wa, ba, wi, bi, lam):
    B, S, _ = xa.shape
    xp = jnp.pad(xa, ((0, 0), (CONV_WIDTH - 1, 0), (0, 0)))
    xc = conv_b
    for tap in range(CONV_WIDTH):
        xc = xc + xp[:, tap:tap + S] * conv_w[tap]
    xh = xc.reshape(B, S, LRU_HEADS, LRU_HEAD_DIM)
    r = jax.nn.sigmoid(jnp.einsum('bshi,hij->bshj', xh, wa).reshape(B, S, LRU_WIDTH) + ba)
    i = jax.nn.sigmoid(jnp.einsum('bshi,hij->bshj', xh, wi).reshape(B, S, LRU_WIDTH) + bi)
    log_a = (-LRU_C * jax.nn.softplus(-lam.astype(jnp.float32))) * r.astype(jnp.float32)
    a = jnp.exp(log_a)
    mult = jnp.sqrt(-jnp.expm1(2.0 * log_a))
    bterm = mult * (i * xc).astype(jnp.float32)

    def combine(left, right):
        a1, b1 = left
        a2, b2 = right
        return a1 * a2, a2 * b1 + b2

    _, h = lax.associative_scan(combine, (a, bterm), axis=1)
    return h.astype(xa.dtype) * jax.nn.gelu(ga)


def spatial_gating_branch(u, v, ln_g, ln_b, w_s, b_s):
    B, S, _ = v.shape
    vn = layer_norm(v, ln_g, ln_b)
    vb = vn.reshape(B, S // SG_LEN, SG_LEN, SG_GROUPS, SG_GROUP_DIM)
    causal = jnp.tril(jnp.ones((SG_LEN, SG_LEN), dtype=bool))
    ws = jnp.where(causal, w_s, jnp.zeros_like(w_s))
    mixed = jnp.einsum('gts,bnsgc->bntgc', ws, vb) + b_s.T[None, None, :, :, None]
    return u * mixed.reshape(B, S, SG_WIDTH)


def t5_bucket(rel):
    half = N_BUCKETS // 2
    max_exact = half // 2
    ret = jnp.where(rel > 0, half, 0)
    n = jnp.abs(rel)
    nf = jnp.maximum(n, 1).astype(jnp.float32)
    large = max_exact + (jnp.log(nf / max_exact) / math.log(MAX_DISTANCE / max_exact)
                         * (half - max_exact)).astype(jnp.int32)
    large = jnp.minimum(large, half - 1)
    return ret + jnp.where(n < max_exact, n, large)


def diff_attention_branch(q, k, v, q_g, k_g, lq1, lk1, lq2, lk2, sub_g, rel_bias, lam_init):
    B, S, _ = q.shape
    H, dk = ATTN_HEADS, ATTN_QK_DIM
    q = rms_norm(q.reshape(B, S, H, 2, dk), q_g) * (dk ** -0.5)
    k = rms_norm(k.reshape(B, S, H, 2, dk), k_g)
    v = v.reshape(B, S, H, ATTN_V_DIM)
    f32 = jnp.float32
    lam = (jnp.exp(jnp.sum(lq1.astype(f32) * lk1.astype(f32)))
           - jnp.exp(jnp.sum(lq2.astype(f32) * lk2.astype(f32))) + lam_init)
    nb = S // Q_BLOCK
    qb = q.reshape(B, nb, Q_BLOCK, H, 2, dk).transpose(1, 0, 2, 3, 4, 5)
    k_pos = jnp.arange(S, dtype=jnp.int32)
    k_chunk = k_pos // CHUNK

    def block(args):
        qi, blk = args
        q_pos = blk * Q_BLOCK + jnp.arange(Q_BLOCK, dtype=jnp.int32)
        bias = rel_bias[t5_bucket(k_pos[None, :] - q_pos[:, None])]
        allowed = k_chunk[None, :] <= (q_pos // CHUNK)[:, None]
        s = (jnp.einsum('bqhcd,bkhcd->bhcqk', qi, k).astype(f32)
             + bias.transpose(2, 0, 1)[None, :, None].astype(f32))
        s = jnp.where(allowed, s, -jnp.inf)
        p = jax.nn.softmax(s, axis=-1)
        attn = p[:, :, 0] - lam * p[:, :, 1]
        return jnp.einsum('bhqk,bkhe->bqhe', attn.astype(v.dtype), v)

    o = lax.map(block, (qb, jnp.arange(nb, dtype=jnp.int32)))
    o = o.transpose(1, 0, 2, 3, 4).reshape(B, S, H, ATTN_V_DIM)
    o = rms_norm(o, sub_g) * (1.0 - lam_init)
    return o.reshape(B, S, ATTN_WIDTH)


def setup_inputs(seed: int = 0) -> dict:
    key = jax.random.key(seed)
    ks = iter(jax.random.split(key, 40))
    f32 = jnp.float32

    def nrm(shape, scale):
        return jax.random.normal(next(ks), shape, f32) * scale

    def gain(shape):
        return 1.0 + nrm(shape, 0.02)

    L, D = DEPTH, D_MODEL
    u = jax.random.uniform(next(ks), (L, LRU_WIDTH), f32, 0.9, 0.999)
    s = u ** (1.0 / LRU_C)
    lru_lambda = jnp.log(s) - jnp.log1p(-s)
    return {
        "x": nrm((BATCH, SEQ, D), 1.0),
        "ln1_g": gain((L, D)),
        "w_in": nrm((L, D, IN_COLS), D ** -0.5),
        "b_gate": nrm((L, N_BRANCH * D), 0.01),
        "conv_w": nrm((L, CONV_WIDTH, LRU_WIDTH), CONV_WIDTH ** -0.5),
        "conv_b": nrm((L, LRU_WIDTH), 0.01),
        "lru_wa": nrm((L, LRU_HEADS, LRU_HEAD_DIM, LRU_HEAD_DIM), LRU_HEAD_DIM ** -0.5),
        "lru_ba": nrm((L, LRU_WIDTH), 0.01),
        "lru_wi": nrm((L, LRU_HEADS, LRU_HEAD_DIM, LRU_HEAD_DIM), LRU_HEAD_DIM ** -0.5),
        "lru_bi": nrm((L, LRU_WIDTH), 0.01),
        "lru_lambda": lru_lambda,
        "sg_ln_g": gain((L, SG_WIDTH)),
        "sg_ln_b": nrm((L, SG_WIDTH), 0.01),
        "sg_w": nrm((L, SG_GROUPS, SG_LEN, SG_LEN), SG_LEN ** -0.5),
        "sg_b": 1.0 + nrm((L, SG_GROUPS, SG_LEN), 0.01),
        "q_norm_g": gain((L, ATTN_QK_DIM)),
        "k_norm_g": gain((L, ATTN_QK_DIM)),
        "lambda_q1": nrm((L, ATTN_QK_DIM), 0.1),
        "lambda_k1": nrm((L, ATTN_QK_DIM), 0.1),
        "lambda_q2": nrm((L, ATTN_QK_DIM), 0.1),
        "lambda_k2": nrm((L, ATTN_QK_DIM), 0.1),
        "subln_g": gain((L, ATTN_V_DIM)),
        "rel_bias": nrm((N_BUCKETS, ATTN_HEADS), 0.5),
        "w_pa": nrm((L, LRU_WIDTH, D), LRU_WIDTH ** -0.5),
        "w_pb": nrm((L, SG_WIDTH, D), SG_WIDTH ** -0.5),
        "w_pc": nrm((L, ATTN_WIDTH, D), ATTN_WIDTH ** -0.5),
        "w_o": nrm((L, D, D), D ** -0.5),
        "ln2_g": gain((L, D)),
        "w_ff_gate": nrm((L, D, FFN_HIDDEN), D ** -0.5),
        "w_ff_up": nrm((L, D, FFN_HIDDEN), D ** -0.5),
        "w_ff_down": nrm((L, FFN_HIDDEN, D), FFN_HIDDEN ** -0.5),
    }


def reference(x, ln1_g, w_in, b_gate, conv_w, conv_b, lru_wa, lru_ba, lru_wi, lru_bi, lru_lambda,
              sg_ln_g, sg_ln_b, sg_w, sg_b, q_norm_g, k_norm_g, lambda_q1, lambda_k1, lambda_q2,
              lambda_k2, subln_g, rel_bias, w_pa, w_pb, w_pc, w_o, ln2_g, w_ff_gate, w_ff_up,
              w_ff_down):
    B, S, D = x.shape
    for l in range(DEPTH):
        lam_init = 0.8 - 0.6 * math.exp(-0.3 * l)
        h = rms_norm(x, ln1_g[l])
        proj = h @ w_in[l]
        xa, ga, u, v, q, k, vv, gates = jnp.split(proj, IN_SPLITS, axis=-1)
        ya = rglru_branch(xa, ga, conv_w[l], conv_b[l], lru_wa[l], lru_ba[l], lru_wi[l], lru_bi[l],
                          lru_lambda[l])
        yb = spatial_gating_branch(u, v, sg_ln_g[l], sg_ln_b[l], sg_w[l], sg_b[l])
        yc = diff_attention_branch(q, k, vv, q_norm_g[l], k_norm_g[l], lambda_q1[l], lambda_k1[l],
                                   lambda_q2[l], lambda_k2[l], subln_g[l], rel_bias, lam_init)
        g = jax.nn.sigmoid(gates + b_gate[l]).reshape(B, S, N_BRANCH, D)
        merged = (g[:, :, 0] * (ya @ w_pa[l]) + g[:, :, 1] * (yb @ w_pb[l])
                  + g[:, :, 2] * (yc @ w_pc[l]))
        x = x + merged @ w_o[l]
        h2 = rms_norm(x, ln2_g[l])
        x = x + (jax.nn.silu(h2 @ w_ff_gate[l]) * (h2 @ w_ff_up[l])) @ w_ff_down[l]
    return x
```

```python
import functools
import math

import jax
import jax.numpy as jnp
from jax import lax
from jax.experimental import pallas as pl
from jax.experimental.pallas import tpu as pltpu

F32 = jnp.float32
BF16 = jnp.bfloat16

EPS = 1e-6
CHUNK = 64
LRU_C = 8.0
SG_LEN = 128
QK_DIM = 64
V_DIM = 128
MAX_DISTANCE = 2048
NEG_BIG = -1e30

ATT_T = 512
VMEM_LIMIT = 56 * 1024 * 1024


def _cparams(*sem):
    return pltpu.CompilerParams(dimension_semantics=sem, vmem_limit_bytes=VMEM_LIMIT)


def _inproj_body(x_ref, g_ref, w_ref, o_ref, h_ref):
    @pl.when(pl.program_id(1) == 0)
    def _():
        x = x_ref[...]
        ms = jnp.mean(x * x, axis=-1, keepdims=True)
        h_ref[...] = (x * lax.rsqrt(ms + EPS) * g_ref[...]).astype(BF16)

    o_ref[...] = jnp.dot(h_ref[...], w_ref[...], preferred_element_type=F32).astype(o_ref.dtype)


def _inproj(x2, g, w, l, tm=1024, tn=512):
    n, d = x2.shape
    c = w.shape[-1]
    return pl.pallas_call(
        _inproj_body,
        grid=(n // tm, c // tn),
        in_specs=[
            pl.BlockSpec((tm, d), lambda i, j: (i, 0)),
            pl.BlockSpec((1, d), lambda i, j: (0, 0)),
            pl.BlockSpec((None, d, tn), lambda i, j: (l, 0, j)),
        ],
        out_specs=pl.BlockSpec((tm, tn), lambda i, j: (i, j)),
        out_shape=jax.ShapeDtypeStruct((n, c), BF16),
        scratch_shapes=[pltpu.VMEM((tm, d), BF16)],
        compiler_params=_cparams("parallel", "arbitrary"),
        name="inproj",
    )(x2, g, w)


def _lru_body(xa_ref, ga_ref, cw_ref, cb_ref, wa_ref, ba_ref, wi_ref, bi_ref, c_ref, o_ref,
              ext_ref, h_ref, *, ts):
    w = xa_ref.shape[-1]

    @pl.when(pl.program_id(1) == 0)
    def _():
        ext_ref[0:8, :] = jnp.zeros((8, w), F32)
        h_ref[...] = jnp.zeros_like(h_ref)

    xa = xa_ref[...].astype(F32)
    ext_ref[8:8 + ts, :] = xa
    xc = (cb_ref[...] + ext_ref[5:5 + ts, :] * cw_ref[0:1, :] + ext_ref[6:6 + ts, :] * cw_ref[1:2, :]
          + ext_ref[7:7 + ts, :] * cw_ref[2:3, :] + xa * cw_ref[3:4, :])
    ext_ref[0:8, :] = ext_ref[ts:ts + 8, :]

    xcb = xc.astype(BF16)
    r = jax.nn.sigmoid(jnp.dot(xcb, wa_ref[...], preferred_element_type=F32) + ba_ref[...])
    i = jax.nn.sigmoid(jnp.dot(xcb, wi_ref[...], preferred_element_type=F32) + bi_ref[...])
    log_a = c_ref[...] * r
    a = jnp.exp(log_a)
    mult = jnp.sqrt(-jnp.tanh(log_a) * (a * a + 1.0))
    b = mult * (i * xc)

    row = lax.broadcasted_iota(jnp.int32, (ts, w), 0)
    d = 1
    while d < ts:
        keep = row >= d
        a_sh = jnp.where(keep, pltpu.roll(a, d, 0), 1.0)
        b_sh = jnp.where(keep, pltpu.roll(b, d, 0), 0.0)
        b = a * b_sh + b
        a = a * a_sh
        d *= 2
    h = a * h_ref[...] + b
    h_ref[...] = h[ts - 1:ts, :]
    o_ref[...] = (h * jax.nn.gelu(ga_ref[...].astype(F32))).astype(o_ref.dtype)


def _lru(proj3, xa_blk, cw, cb, wa, ba, wi, bi, cvec, ts=512):
    bsz, s, _ = proj3.shape
    w = cw.shape[-1]
    full = lambda shp: pl.BlockSpec(shp, lambda b, t: (0,) * len(shp))
    return pl.pallas_call(
        functools.partial(_lru_body, ts=ts),
        grid=(bsz, s // ts),
        in_specs=[
            pl.BlockSpec((None, ts, w), lambda b, t: (b, t, xa_blk)),
            pl.BlockSpec((None, ts, w), lambda b, t: (b, t, xa_blk + 1)),
            full((4, w)), full((1, w)), full((w, w)), full((1, w)), full((w, w)), full((1, w)),
            full((1, w)),
        ],
        out_specs=pl.BlockSpec((None, ts, w), lambda b, t: (b, t, 0)),
        out_shape=jax.ShapeDtypeStruct((bsz, s, w), BF16),
        scratch_shapes=[pltpu.VMEM((ts + 8, w), F32), pltpu.VMEM((1, w), F32)],
        compiler_params=_cparams("parallel", "arbitrary"),
        name="lru",
    )(proj3, proj3, cw, cb, wa, ba, wi, bi, cvec)


def _sgu_body(u_ref, v_ref, g_ref, b_ref, ws_ref, bs_ref, o_ref, *, tr, groups):
    w = v_ref.shape[-1]
    gd = w // groups
    v = v_ref[...].astype(F32)
    mu = jnp.mean(v, axis=-1, keepdims=True)
    vc = v - mu
    var = jnp.mean(vc * vc, axis=-1, keepdims=True)
    vn = (vc * lax.rsqrt(var + EPS) * g_ref[...] + b_ref[...]).astype(BF16)
    lane_grp = lax.broadcasted_iota(jnp.int32, (SG_LEN, w), 1) // gd
    ws = ws_ref[...]
    zero = jnp.zeros((SG_LEN, w), BF16)
    for blk in range(tr // SG_LEN):
        rows = slice(blk * SG_LEN, (blk + 1) * SG_LEN)
        vb = vn[rows]
        stacked = jnp.concatenate([jnp.where(lane_grp == g, vb, zero) for g in range(groups)], axis=0)
        mixed = jnp.dot(ws, stacked, preferred_element_type=F32) + bs_ref[...]
        o_ref[rows, :] = (u_ref[rows, :].astype(F32) * mixed).astype(o_ref.dtype)


def _sgu(proj, u_blk, ln_g, ln_b, ws_cat, bs_full, groups, tr=512):
    n, _ = proj.shape
    w = ln_g.shape[-1]
    full = lambda shp: pl.BlockSpec(shp, lambda i: (0,) * len(shp))
    return pl.pallas_call(
        functools.partial(_sgu_body, tr=tr, groups=groups),
        grid=(n // tr,),
        in_specs=[
            pl.BlockSpec((tr, w), lambda i: (i, u_blk)),
            pl.BlockSpec((tr, w), lambda i: (i, u_blk + 1)),
            full((1, w)), full((1, w)), full(ws_cat.shape), full(bs_full.shape),
        ],
        out_specs=pl.BlockSpec((tr, w), lambda i: (i, 0)),
        out_shape=jax.ShapeDtypeStruct((n, w), BF16),
        compiler_params=_cparams("parallel"),
        name="sgu",
    )(proj, proj, ln_g, ln_b, ws_cat, bs_full)


def _prep_body(q_ref, k_ref, v_ref, gq_ref, gk_ref, bd_ref, qt_ref, kn_ref, vt_ref, *, heads):
    def group_rms(y_ref, g_ref):
        y = y_ref[...].astype(F32)
        y2 = y * y
        hi = y2.astype(BF16)
        lo = (y2 - hi.astype(F32)).astype(BF16)
        ss = (jnp.dot(hi, bd_ref[...], preferred_element_type=F32)
              + jnp.dot(lo, bd_ref[...], preferred_element_type=F32))
        return y * lax.rsqrt(ss * (1.0 / QK_DIM) + EPS) * g_ref[...]

    qn = group_rms(q_ref, gq_ref)
    kn_ref[...] = group_rms(k_ref, gk_ref).astype(BF16)
    v = v_ref[...].astype(F32)
    for h in range(heads):
        cols = slice(h * V_DIM, (h + 1) * V_DIM)
        qt_ref[h] = qn[:, cols].T.astype(BF16)
        vt_ref[h] = v[:, cols].T.astype(BF16)


def _prep(proj3, q_blk, gq, gk, bd, heads, tm=512):
    bsz, s, _ = proj3.shape
    wq = heads * 2 * QK_DIM
    full = lambda shp: pl.BlockSpec(shp, lambda b, t: (0,) * len(shp))
    tspec = pl.BlockSpec((None, heads, V_DIM, tm), lambda b, t: (b, 0, 0, t))
    return pl.pallas_call(
        functools.partial(_prep_body, heads=heads),
        grid=(bsz, s // tm),
        in_specs=[
            pl.BlockSpec((None, tm, wq), lambda b, t: (b, t, q_blk)),
            pl.BlockSpec((None, tm, wq), lambda b, t: (b, t, q_blk + 1)),
            pl.BlockSpec((None, tm, wq), lambda b, t: (b, t, q_blk + 2)),
            full((1, wq)), full((1, wq)), full((wq, wq)),
        ],
        out_specs=[tspec, pl.BlockSpec((None, tm, wq), lambda b, t: (b, t, 0)), tspec],
        out_shape=[
            jax.ShapeDtypeStruct((bsz, heads, V_DIM, s), BF16),
            jax.ShapeDtypeStruct((bsz, s, wq), BF16),
            jax.ShapeDtypeStruct((bsz, heads, V_DIM, s), BF16),
        ],
        compiler_params=_cparams("parallel", "parallel"),
        name="qkv_prep",
    )(proj3, proj3, proj3, gq, gk, bd)


def _attn_body(qt_ref, k_ref, vt_ref, bias_ref, lam_ref, subg_ref, o_ref, acc_ref, *, t, nd, lam_init):
    qi = pl.program_id(2)
    qt = qt_ref[...]
    row = lax.broadcasted_iota(jnp.int32, qt.shape, 0)
    zero = jnp.zeros_like(qt)
    q1 = jnp.where(row < QK_DIM, qt, zero)
    q2 = jnp.where(row >= QK_DIM, qt, zero)
    acc_ref[...] = jnp.zeros_like(acc_ref)

    def update(idx, s, vt, m, l):
        mn = jnp.maximum(m, jnp.max(s, axis=0, keepdims=True))
        alpha = jnp.exp(m - mn)
        p = jnp.exp(s - mn)
        l = alpha * l + jnp.sum(p, axis=0, keepdims=True)
        acc_ref[idx] = acc_ref[idx] * alpha + jnp.dot(vt, p.astype(BF16), preferred_element_type=F32)
        return mn, l

    def step(kj, carry, near):
        m1, l1, m2, l2 = carry
        off = pl.multiple_of(kj * t, t)
        k = k_ref[pl.ds(off, t), :]
        vt = vt_ref[:, pl.ds(off, t)]
        s1 = jnp.dot(k, q1, preferred_element_type=F32)
        s2 = jnp.dot(k, q2, preferred_element_type=F32)
        if near:
            b = bias_ref[qi - kj]
            s1 = s1 + b
            s2 = s2 + b
        m1, l1 = update(0, s1, vt, m1, l1)
        m2, l2 = update(1, s2, vt, m2, l2)
        return m1, l1, m2, l2

    n_far = jnp.maximum(qi - (nd - 1), 0)
    neg = jnp.full((1, t), NEG_BIG, F32)
    zer = jnp.zeros((1, t), F32)
    carry = lax.fori_loop(0, n_far, functools.partial(step, near=False), (neg, zer, neg, zer))
    _, l1, _, l2 = lax.fori_loop(n_far, qi + 1, functools.partial(step, near=True), carry)

    lam4 = lam_ref[...]
    lam = (jnp.exp(jnp.sum(lam4[0:1] * lam4[1:2], axis=-1, keepdims=True))
           - jnp.exp(jnp.sum(lam4[2:3] * lam4[3:4], axis=-1, keepdims=True)) + lam_init)
    o = acc_ref[0] / l1 - lam * (acc_ref[1] / l2)
    ms = jnp.mean(o * o, axis=0, keepdims=True)
    o = o * lax.rsqrt(ms + EPS) * subg_ref[...] * (1.0 - lam_init)
    o_ref[...] = o.T.astype(o_ref.dtype)


def _attn(qt, kn, vt, bias, lam4, subg, lam_init, t=ATT_T):
    bsz, heads, _, s = qt.shape
    nd = bias.shape[1]
    return pl.pallas_call(
        functools.partial(_attn_body, t=t, nd=nd, lam_init=lam_init),
        grid=(bsz, heads, s // t),
        in_specs=[
            pl.BlockSpec((None, None, V_DIM, t), lambda b, h, i: (b, h, 0, i)),
            pl.BlockSpec((None, s, V_DIM), lambda b, h, i: (b, 0, h)),
            pl.BlockSpec((None, None, V_DIM, s), lambda b, h, i: (b, h, 0, 0)),
            pl.BlockSpec((None, nd, t, t), lambda b, h, i: (h, 0, 0, 0)),
            pl.BlockSpec(lam4.shape, lambda b, h, i: (0, 0)),
            pl.BlockSpec(subg.shape, lambda b, h, i: (0, 0)),
        ],
        out_specs=pl.BlockSpec((None, t, V_DIM), lambda b, h, i: (b, i, h)),
        out_shape=jax.ShapeDtypeStruct((bsz, s, heads * V_DIM), BF16),
        scratch_shapes=[pltpu.VMEM((2, V_DIM, t), F32)],
        compiler_params=_cparams("parallel", "parallel", "arbitrary"),
        name="diff_attn",
    )(qt, kn, vt, bias, lam4, subg)


def _t5_bucket(rel, n_buckets):
    half = n_buckets // 2
    max_exact = half // 2
    ret = jnp.where(rel > 0, half, 0)
    n = jnp.abs(rel)
    nf = jnp.maximum(n, 1).astype(F32)
    large = max_exact + (jnp.log(nf / max_exact) / math.log(MAX_DISTANCE / max_exact)
                         * (half - max_exact)).astype(jnp.int32)
    large = jnp.minimum(large, half - 1)
    return ret + jnp.where(n < max_exact, n, large)


def _bias_tiles(rel_bias, t, nd):
    n_buckets = rel_bias.shape[0]
    j = jnp.arange(t, dtype=jnp.int32)[:, None]
    i = jnp.arange(t, dtype=jnp.int32)[None, :]
    delta = jnp.arange(nd, dtype=jnp.int32)[:, None, None]
    rel = (j - i)[None] - delta * t
    b = rel_bias[_t5_bucket(rel, n_buckets)] - rel_bias[n_buckets // 2 - 1]
    b = jnp.transpose(b, (3, 0, 1, 2))
    allowed = (j // CHUNK) <= (i // CHUNK)
    return b.at[:, 0].set(jnp.where(allowed[None], b[:, 0], NEG_BIG))


def _merge_body(x_ref, ya_ref, yb_ref, yc_ref, g0_ref, g1_ref, g2_ref, bg_ref, wpa_ref, wpb_ref,
                wpc_ref, wo_ref, o_ref):
    def branch(idx, y_ref, w_ref, g_ref):
        p = jnp.dot(y_ref[...], w_ref[...], preferred_element_type=F32)
        return jax.nn.sigmoid(g_ref[...].astype(F32) + bg_ref[idx]) * p

    merged = (branch(0, ya_ref, wpa_ref, g0_ref) + branch(1, yb_ref, wpb_ref, g1_ref)
              + branch(2, yc_ref, wpc_ref, g2_ref))
    o_ref[...] = x_ref[...] + jnp.dot(merged.astype(BF16), wo_ref[...], preferred_element_type=F32)


def _merge(x2, ya, yb, yc, proj, bg, wpa, wpb, wpc, wo, l, tm=512):
    n, d = x2.shape
    rows = lambda a: pl.BlockSpec((tm, a.shape[-1]), lambda i: (i, 0))
    wspec = lambda a: pl.BlockSpec((None,) + a.shape[1:], lambda i: (l, 0, 0))
    gate = lambda g: pl.BlockSpec((tm, d), lambda i: (i, g))
    return pl.pallas_call(
        _merge_body,
        grid=(n // tm,),
        in_specs=[rows(x2), rows(ya), rows(yb), rows(yc), gate(0), gate(1), gate(2),
                  pl.BlockSpec(bg.shape, lambda i: (0, 0, 0)),
                  wspec(wpa), wspec(wpb), wspec(wpc), wspec(wo)],
        out_specs=pl.BlockSpec((tm, d), lambda i: (i, 0)),
        out_shape=jax.ShapeDtypeStruct((n, d), F32),
        compiler_params=_cparams("parallel"),
        name="merge",
    )(x2, ya, yb, yc, proj, proj, proj, bg, wpa, wpb, wpc, wo)


def _ffn_body(x_ref, g_ref, wg_ref, wu_ref, wd_ref, o_ref, h_ref):
    @pl.when(pl.program_id(1) == 0)
    def _():
        x = x_ref[...]
        ms = jnp.mean(x * x, axis=-1, keepdims=True)
        h_ref[...] = (x * lax.rsqrt(ms + EPS) * g_ref[...]).astype(BF16)
        o_ref[...] = x

    h = h_ref[...]
    gate = jnp.dot(h, wg_ref[...], preferred_element_type=F32)
    up = jnp.dot(h, wu_ref[...], preferred_element_type=F32)
    act = (jax.nn.silu(gate) * up).astype(BF16)
    o_ref[...] += jnp.dot(act, wd_ref[...], preferred_element_type=F32)


def _ffn(x2, g, wg, wu, wd, l, tm=512):
    n, d = x2.shape
    hid = wg.shape[-1]
    th = hid // 2 if (hid // 2) % 128 == 0 else hid
    return pl.pallas_call(
        _ffn_body,
        grid=(n // tm, hid // th),
        in_specs=[
            pl.BlockSpec((tm, d), lambda i, j: (i, 0)),
            pl.BlockSpec((1, d), lambda i, j: (0, 0)),
            pl.BlockSpec((None, d, th), lambda i, j: (l, 0, j)),
            pl.BlockSpec((None, d, th), lambda i, j: (l, 0, j)),
            pl.BlockSpec((None, th, d), lambda i, j: (l, j, 0)),
        ],
        out_specs=pl.BlockSpec((tm, d), lambda i, j: (i, 0)),
        out_shape=jax.ShapeDtypeStruct((n, d), F32),
        scratch_shapes=[pltpu.VMEM((tm, d), BF16)],
        compiler_params=_cparams("parallel", "arbitrary"),
        name="ffn",
    )(x2, g, wg, wu, wd)


def _block_diag(w):
    heads, a, b = w.shape
    eye = jnp.eye(heads, dtype=w.dtype)
    return (eye[:, None, :, None] * w[:, :, None, :]).reshape(heads * a, heads * b)


def kernel(x, ln1_g, w_in, b_gate, conv_w, conv_b, lru_wa, lru_ba, lru_wi, lru_bi, lru_lambda, sg_ln_g, sg_ln_b, sg_w, sg_b, q_norm_g, k_norm_g, lambda_q1, lambda_k1, lambda_q2, lambda_k2, subln_g, rel_bias, w_pa, w_pb, w_pc, w_o, ln2_g, w_ff_gate, w_ff_up, w_ff_down):
    bsz, s, d = x.shape
    depth = w_in.shape[0]
    lru_w = conv_w.shape[-1]
    sg_wd = sg_ln_g.shape[-1]
    groups = sg_w.shape[1]
    heads = rel_bias.shape[1]
    wq = heads * 2 * QK_DIM
    n_gate = 3 * d
    branch_cols = w_in.shape[-1] - n_gate
    assert branch_cols == 2 * lru_w + 2 * sg_wd + 3 * wq
    assert lru_w == sg_wd and n_gate % lru_w == 0 and (n_gate + 4 * lru_w) % wq == 0
    assert s % ATT_T == 0 and ATT_T % CHUNK == 0

    w_in_p = jnp.concatenate([w_in[..., branch_cols:], w_in[..., :branch_cols]], axis=-1).astype(BF16)
    xa_blk = n_gate // lru_w
    u_blk = xa_blk + 2
    q_blk = (n_gate + 4 * lru_w) // wq

    nd = -(-(MAX_DISTANCE // 2 + ATT_T) // ATT_T)
    bias = _bias_tiles(rel_bias, ATT_T, nd)
    bd = _block_diag(jnp.ones((wq // QK_DIM, QK_DIM, QK_DIM), BF16))
    w_pa_b, w_pb_b, w_pc_b, w_o_b = (w.astype(BF16) for w in (w_pa, w_pb, w_pc, w_o))
    wg_b, wu_b, wd_b = (w.astype(BF16) for w in (w_ff_gate, w_ff_up, w_ff_down))
    causal = jnp.tril(jnp.ones((SG_LEN, SG_LEN), bool))

    x2 = x.reshape(bsz * s, d)
    for l in range(depth):
        lam_init = 0.8 - 0.6 * math.exp(-0.3 * l)
        proj = _inproj(x2, ln1_g[l][None], w_in_p, l)
        proj3 = proj.reshape(bsz, s, -1)

        cvec = (-LRU_C * jax.nn.softplus(-lru_lambda[l]))[None]
        ya = _lru(proj3, xa_blk, conv_w[l], conv_b[l][None], _block_diag(lru_wa[l]).astype(BF16),
                  lru_ba[l][None], _block_diag(lru_wi[l]).astype(BF16), lru_bi[l][None], cvec)

        ws = jnp.where(causal, sg_w[l], 0.0).astype(BF16)
        ws_cat = jnp.transpose(ws, (1, 0, 2)).reshape(SG_LEN, groups * SG_LEN)
        bs_full = jnp.repeat(sg_b[l].T, sg_wd // groups, axis=1)
        yb = _sgu(proj, u_blk, sg_ln_g[l][None], sg_ln_b[l][None], ws_cat, bs_full, groups)

        gq = jnp.tile(q_norm_g[l] * (QK_DIM ** -0.5), wq // QK_DIM)[None]
        gk = jnp.tile(k_norm_g[l], wq // QK_DIM)[None]
        qt, kn, vt = _prep(proj3, q_blk, gq, gk, bd, heads)
        lam4 = jnp.stack([lambda_q1[l], lambda_k1[l], lambda_q2[l], lambda_k2[l]])
        subg = jnp.broadcast_to(subln_g[l][:, None], (V_DIM, ATT_T))
        yc = _attn(qt, kn, vt, bias, lam4, subg, lam_init)

        x2 = _merge(x2, ya.reshape(bsz * s, -1), yb, yc.reshape(bsz * s, -1), proj,
                    b_gate[l].reshape(3, 1, d), w_pa_b, w_pb_b, w_pc_b, w_o_b, l)
        x2 = _ffn(x2, ln2_g[l][None], wg_b, wu_b, wd_b, l)
    return x2.reshape(bsz, s, d)
```

```python
import functools
import math

import jax
import jax.numpy as jnp
from jax import lax
from jax.experimental import pallas as pl
from jax.experimental.pallas import tpu as pltpu

F32 = jnp.float32
BF16 = jnp.bfloat16

EPS = 1e-6
CHUNK = 64
LRU_C = 8.0
SG_LEN = 128
QK_DIM = 64
V_DIM = 128
MAX_DISTANCE = 2048
NEG_BIG = -1e30
LOG2E = math.log2(math.e)

ATT_T = 512
VMEM_LIMIT = 56 * 1024 * 1024


def _cparams(*sem):
    return pltpu.CompilerParams(dimension_semantics=sem, vmem_limit_bytes=VMEM_LIMIT)


def _inproj_body(x_ref, g_ref, w_ref, o_ref, h_ref):
    @pl.when(pl.program_id(1) == 0)
    def _():
        x = x_ref[...]
        ms = jnp.mean(x * x, axis=-1, keepdims=True)
        h_ref[...] = (x * lax.rsqrt(ms + EPS) * g_ref[...]).astype(BF16)

    o_ref[...] = jnp.dot(h_ref[...], w_ref[...], preferred_element_type=F32).astype(o_ref.dtype)


def _inproj(x2, g, w, l, tm=1024, tn=512):
    n, d = x2.shape
    c = w.shape[-1]
    return pl.pallas_call(
        _inproj_body,
        grid=(n // tm, c // tn),
        in_specs=[
            pl.BlockSpec((tm, d), lambda i, j: (i, 0)),
            pl.BlockSpec((1, d), lambda i, j: (0, 0)),
            pl.BlockSpec((None, d, tn), lambda i, j: (l, 0, j)),
        ],
        out_specs=pl.BlockSpec((tm, tn), lambda i, j: (i, j)),
        out_shape=jax.ShapeDtypeStruct((n, c), BF16),
        scratch_shapes=[pltpu.VMEM((tm, d), BF16)],
        compiler_params=_cparams("parallel", "arbitrary"),
        name="inproj",
    )(x2, g, w)


def _lru_body(xa_ref, ga_ref, cw_ref, cb_ref, wa_ref, ba_ref, wi_ref, bi_ref, c_ref, o_ref,
              ext_ref, h_ref, *, ts):
    w = xa_ref.shape[-1]

    @pl.when(pl.program_id(1) == 0)
    def _():
        ext_ref[0:8, :] = jnp.zeros((8, w), F32)
        h_ref[...] = jnp.zeros_like(h_ref)

    xa = xa_ref[...].astype(F32)
    ext_ref[8:8 + ts, :] = xa
    xc = (cb_ref[...] + ext_ref[5:5 + ts, :] * cw_ref[0:1, :] + ext_ref[6:6 + ts, :] * cw_ref[1:2, :]
          + ext_ref[7:7 + ts, :] * cw_ref[2:3, :] + xa * cw_ref[3:4, :])
    ext_ref[0:8, :] = ext_ref[ts:ts + 8, :]

    xcb = xc.astype(BF16)
    r = jax.nn.sigmoid(jnp.dot(xcb, wa_ref[...], preferred_element_type=F32) + ba_ref[...])
    i = jax.nn.sigmoid(jnp.dot(xcb, wi_ref[...], preferred_element_type=F32) + bi_ref[...])
    log_a = c_ref[...] * r
    a = jnp.exp(log_a)
    mult = jnp.sqrt(-jnp.tanh(log_a) * (a * a + 1.0))
    b = mult * (i * xc)

    row = lax.broadcasted_iota(jnp.int32, (ts, w), 0)
    d = 1
    while d < ts:
        keep = row >= d
        a_sh = jnp.where(keep, pltpu.roll(a, d, 0), 1.0)
        b_sh = jnp.where(keep, pltpu.roll(b, d, 0), 0.0)
        b = a * b_sh + b
        a = a * a_sh
        d *= 2
    h = a * h_ref[...] + b
    h_ref[...] = h[ts - 1:ts, :]
    o_ref[...] = (h * jax.nn.gelu(ga_ref[...].astype(F32))).astype(o_ref.dtype)


def _lru(proj3, xa_blk, cw, cb, wa, ba, wi, bi, cvec, ts=512):
    bsz, s, _ = proj3.shape
    w = cw.shape[-1]
    full = lambda shp: pl.BlockSpec(shp, lambda b, t: (0,) * len(shp))
    return pl.pallas_call(
        functools.partial(_lru_body, ts=ts),
        grid=(bsz, s // ts),
        in_specs=[
            pl.BlockSpec((None, ts, w), lambda b, t: (b, t, xa_blk)),
            pl.BlockSpec((None, ts, w), lambda b, t: (b, t, xa_blk + 1)),
            full((4, w)), full((1, w)), full((w, w)), full((1, w)), full((w, w)), full((1, w)),
            full((1, w)),
        ],
        out_specs=pl.BlockSpec((None, ts, w), lambda b, t: (b, t, 0)),
        out_shape=jax.ShapeDtypeStruct((bsz, s, w), BF16),
        scratch_shapes=[pltpu.VMEM((ts + 8, w), F32), pltpu.VMEM((1, w), F32)],
        compiler_params=_cparams("parallel", "arbitrary"),
        name="lru",
    )(proj3, proj3, cw, cb, wa, ba, wi, bi, cvec)


def _sgu_body(u_ref, v_ref, g_ref, b_ref, ws_ref, bs_ref, o_ref, *, tr, groups):
    w = v_ref.shape[-1]
    gd = w // groups
    v = v_ref[...].astype(F32)
    mu = jnp.mean(v, axis=-1, keepdims=True)
    vc = v - mu
    var = jnp.mean(vc * vc, axis=-1, keepdims=True)
    vn = (vc * lax.rsqrt(var + EPS) * g_ref[...] + b_ref[...]).astype(BF16)
    lane_grp = lax.broadcasted_iota(jnp.int32, (SG_LEN, w), 1) // gd
    ws = ws_ref[...]
    zero = jnp.zeros((SG_LEN, w), BF16)
    for blk in range(tr // SG_LEN):
        rows = slice(blk * SG_LEN, (blk + 1) * SG_LEN)
        vb = vn[rows]
        stacked = jnp.concatenate([jnp.where(lane_grp == g, vb, zero) for g in range(groups)], axis=0)
        mixed = jnp.dot(ws, stacked, preferred_element_type=F32) + bs_ref[...]
        o_ref[rows, :] = (u_ref[rows, :].astype(F32) * mixed).astype(o_ref.dtype)


def _sgu(proj, u_blk, ln_g, ln_b, ws_cat, bs_full, groups, tr=512):
    n, _ = proj.shape
    w = ln_g.shape[-1]
    full = lambda shp: pl.BlockSpec(shp, lambda i: (0,) * len(shp))
    return pl.pallas_call(
        functools.partial(_sgu_body, tr=tr, groups=groups),
        grid=(n // tr,),
        in_specs=[
            pl.BlockSpec((tr, w), lambda i: (i, u_blk)),
            pl.BlockSpec((tr, w), lambda i: (i, u_blk + 1)),
            full((1, w)), full((1, w)), full(ws_cat.shape), full(bs_full.shape),
        ],
        out_specs=pl.BlockSpec((tr, w), lambda i: (i, 0)),
        out_shape=jax.ShapeDtypeStruct((n, w), BF16),
        compiler_params=_cparams("parallel"),
        name="sgu",
    )(proj, proj, ln_g, ln_b, ws_cat, bs_full)


def _prep_body(q_ref, k_ref, v_ref, gq_ref, gk_ref, bd_ref, qt_ref, kn_ref, vt_ref, *, heads):
    def group_rms(y_ref, g_ref):
        y = y_ref[...].astype(F32)
        y2 = y * y
        hi = y2.astype(BF16)
        lo = (y2 - hi.astype(F32)).astype(BF16)
        ss = (jnp.dot(hi, bd_ref[...], preferred_element_type=F32)
              + jnp.dot(lo, bd_ref[...], preferred_element_type=F32))
        return y * lax.rsqrt(ss * (1.0 / QK_DIM) + EPS) * g_ref[...]

    qn = group_rms(q_ref, gq_ref)
    kn_ref[...] = group_rms(k_ref, gk_ref).astype(BF16)
    v = v_ref[...].astype(F32)
    for h in range(heads):
        cols = slice(h * V_DIM, (h + 1) * V_DIM)
        qt_ref[h] = qn[:, cols].T.astype(BF16)
        vt_ref[h] = v[:, cols].T.astype(BF16)


def _prep(proj3, q_blk, gq, gk, bd, heads, tm=512):
    bsz, s, _ = proj3.shape
    wq = heads * 2 * QK_DIM
    full = lambda shp: pl.BlockSpec(shp, lambda b, t: (0,) * len(shp))
    tspec = pl.BlockSpec((None, heads, V_DIM, tm), lambda b, t: (b, 0, 0, t))
    return pl.pallas_call(
        functools.partial(_prep_body, heads=heads),
        grid=(bsz, s // tm),
        in_specs=[
            pl.BlockSpec((None, tm, wq), lambda b, t: (b, t, q_blk)),
            pl.BlockSpec((None, tm, wq), lambda b, t: (b, t, q_blk + 1)),
            pl.BlockSpec((None, tm, wq), lambda b, t: (b, t, q_blk + 2)),
            full((1, wq)), full((1, wq)), full((wq, wq)),
        ],
        out_specs=[tspec, pl.BlockSpec((None, tm, wq), lambda b, t: (b, t, 0)), tspec],
        out_shape=[
            jax.ShapeDtypeStruct((bsz, heads, V_DIM, s), BF16),
            jax.ShapeDtypeStruct((bsz, s, wq), BF16),
            jax.ShapeDtypeStruct((bsz, heads, V_DIM, s), BF16),
        ],
        compiler_params=_cparams("parallel", "parallel"),
        name="qkv_prep",
    )(proj3, proj3, proj3, gq, gk, bd)


def _attn_body(qt_ref, k_ref, vt_ref, bias_ref, lam_ref, subg_ref, o_ref, acc_ref, *, t, nd, lam_init):
    qi = pl.program_id(2)
    qt = qt_ref[...]
    row = lax.broadcasted_iota(jnp.int32, qt.shape, 0)
    zero = jnp.zeros_like(qt)
    q1 = jnp.where(row < QK_DIM, qt, zero)
    q2 = jnp.where(row >= QK_DIM, qt, zero)
    acc_ref[...] = jnp.zeros_like(acc_ref)

    def update(idx, s, vt, m, l):
        mn = jnp.maximum(m, jnp.max(s, axis=0, keepdims=True))
        alpha = jnp.exp2(m - mn)
        p = jnp.exp2(s - mn)
        l = alpha * l + jnp.sum(p, axis=0, keepdims=True)
        acc_ref[idx] = acc_ref[idx] * alpha + jnp.dot(vt, p.astype(BF16), preferred_element_type=F32)
        return mn, l

    def step(kj, carry, near):
        m1, l1, m2, l2 = carry
        off = pl.multiple_of(kj * t, t)
        k = k_ref[pl.ds(off, t), :]
        vt = vt_ref[:, pl.ds(off, t)]
        s1 = jnp.dot(k, q1, preferred_element_type=F32)
        s2 = jnp.dot(k, q2, preferred_element_type=F32)
        if near:
            b = bias_ref[qi - kj]
            s1 = s1 + b
            s2 = s2 + b
        m1, l1 = update(0, s1, vt, m1, l1)
        m2, l2 = update(1, s2, vt, m2, l2)
        return m1, l1, m2, l2

    n_far = jnp.maximum(qi - (nd - 1), 0)
    neg = jnp.full((1, t), NEG_BIG, F32)
    zer = jnp.zeros((1, t), F32)
    carry = lax.fori_loop(0, n_far, functools.partial(step, near=False), (neg, zer, neg, zer))
    _, l1, _, l2 = lax.fori_loop(n_far, qi + 1, functools.partial(step, near=True), carry)

    lam4 = lam_ref[...]
    lam = (jnp.exp(jnp.sum(lam4[0:1] * lam4[1:2], axis=-1, keepdims=True))
           - jnp.exp(jnp.sum(lam4[2:3] * lam4[3:4], axis=-1, keepdims=True)) + lam_init)
    o = acc_ref[0] / l1 - lam * (acc_ref[1] / l2)
    ms = jnp.mean(o * o, axis=0, keepdims=True)
    o = o * lax.rsqrt(ms + EPS) * subg_ref[...] * (1.0 - lam_init)
    o_ref[...] = o.T.astype(o_ref.dtype)


def _attn(qt, kn, vt, bias, lam4, subg, lam_init, t=ATT_T):
    bsz, heads, _, s = qt.shape
    nd = bias.shape[1]
    return pl.pallas_call(
        functools.partial(_attn_body, t=t, nd=nd, lam_init=lam_init),
        grid=(bsz, heads, s // t),
        in_specs=[
            pl.BlockSpec((None, None, V_DIM, t), lambda b, h, i: (b, h, 0, i)),
            pl.BlockSpec((None, s, V_DIM), lambda b, h, i: (b, 0, h)),
            pl.BlockSpec((None, None, V_DIM, s), lambda b, h, i: (b, h, 0, 0)),
            pl.BlockSpec((None, nd, t, t), lambda b, h, i: (h, 0, 0, 0)),
            pl.BlockSpec(lam4.shape, lambda b, h, i: (0, 0)),
            pl.BlockSpec(subg.shape, lambda b, h, i: (0, 0)),
        ],
        out_specs=pl.BlockSpec((None, t, V_DIM), lambda b, h, i: (b, i, h)),
        out_shape=jax.ShapeDtypeStruct((bsz, s, heads * V_DIM), BF16),
        scratch_shapes=[pltpu.VMEM((2, V_DIM, t), F32)],
        compiler_params=_cparams("parallel", "parallel", "arbitrary"),
        name="diff_attn",
    )(qt, kn, vt, bias, lam4, subg)


def _t5_bucket(rel, n_buckets):
    half = n_buckets // 2
    max_exact = half // 2
    ret = jnp.where(rel > 0, half, 0)
    n = jnp.abs(rel)
    nf = jnp.maximum(n, 1).astype(F32)
    large = max_exact + (jnp.log(nf / max_exact) / math.log(MAX_DISTANCE / max_exact)
                         * (half - max_exact)).astype(jnp.int32)
    large = jnp.minimum(large, half - 1)
    return ret + jnp.where(n < max_exact, n, large)


def _bias_body(f_ref, o_ref, *, t):
    x = jnp.broadcast_to(f_ref[...], (t, 2 * t))
    y = pltpu.roll(x, t + 1, 1, stride=1, stride_axis=0)[:, :t]
    j = lax.broadcasted_iota(jnp.int32, (t, t), 0)
    i = lax.broadcasted_iota(jnp.int32, (t, t), 1)
    allowed = ((j // CHUNK) <= (i // CHUNK)) | (pl.program_id(1) > 0)
    o_ref[...] = jnp.where(allowed, y, NEG_BIG)


def _bias_tiles(rel_bias, t, nd):
    n_buckets, heads = rel_bias.shape
    c = jnp.arange(2 * t, dtype=jnp.int32)[None, :]
    delta = jnp.arange(nd, dtype=jnp.int32)[:, None]
    rel = (t - 1 - c) - delta * t
    f = (rel_bias[_t5_bucket(rel, n_buckets)] - rel_bias[n_buckets // 2 - 1]) * LOG2E
    f = jnp.transpose(f, (2, 0, 1))[:, :, None, :]
    return pl.pallas_call(
        functools.partial(_bias_body, t=t),
        grid=(heads, nd),
        in_specs=[pl.BlockSpec((None, None, 1, 2 * t), lambda h, d: (h, d, 0, 0))],
        out_specs=pl.BlockSpec((None, None, t, t), lambda h, d: (h, d, 0, 0)),
        out_shape=jax.ShapeDtypeStruct((heads, nd, t, t), F32),
        compiler_params=_cparams("parallel", "parallel"),
        name="bias_tiles",
    )(f)


def _merge_body(x_ref, ya_ref, yb_ref, yc_ref, g0_ref, g1_ref, g2_ref, bg_ref, wpa_ref, wpb_ref,
                wpc_ref, wo_ref, o_ref):
    def branch(idx, y_ref, w_ref, g_ref):
        p = jnp.dot(y_ref[...], w_ref[...], preferred_element_type=F32)
        return jax.nn.sigmoid(g_ref[...].astype(F32) + bg_ref[idx]) * p

    merged = (branch(0, ya_ref, wpa_ref, g0_ref) + branch(1, yb_ref, wpb_ref, g1_ref)
              + branch(2, yc_ref, wpc_ref, g2_ref))
    o_ref[...] = x_ref[...] + jnp.dot(merged.astype(BF16), wo_ref[...], preferred_element_type=F32)


def _merge(x2, ya, yb, yc, proj, bg, wpa, wpb, wpc, wo, l, tm=512):
    n, d = x2.shape
    rows = lambda a: pl.BlockSpec((tm, a.shape[-1]), lambda i: (i, 0))
    wspec = lambda a: pl.BlockSpec((None,) + a.shape[1:], lambda i: (l, 0, 0))
    gate = lambda g: pl.BlockSpec((tm, d), lambda i: (i, g))
    return pl.pallas_call(
        _merge_body,
        grid=(n // tm,),
        in_specs=[rows(x2), rows(ya), rows(yb), rows(yc), gate(0), gate(1), gate(2),
                  pl.BlockSpec(bg.shape, lambda i: (0, 0, 0)),
                  wspec(wpa), wspec(wpb), wspec(wpc), wspec(wo)],
        out_specs=pl.BlockSpec((tm, d), lambda i: (i, 0)),
        out_shape=jax.ShapeDtypeStruct((n, d), F32),
        compiler_params=_cparams("parallel"),
        name="merge",
    )(x2, ya, yb, yc, proj, proj, proj, bg, wpa, wpb, wpc, wo)


def _ffn_body(x_ref, g_ref, wg_ref, wu_ref, wd_ref, o_ref, h_ref):
    @pl.when(pl.program_id(1) == 0)
    def _():
        x = x_ref[...]
        ms = jnp.mean(x * x, axis=-1, keepdims=True)
        h_ref[...] = (x * lax.rsqrt(ms + EPS) * g_ref[...]).astype(BF16)
        o_ref[...] = x

    h = h_ref[...]
    gate = jnp.dot(h, wg_ref[...], preferred_element_type=F32)
    up = jnp.dot(h, wu_ref[...], preferred_element_type=F32)
    act = (jax.nn.silu(gate) * up).astype(BF16)
    o_ref[...] += jnp.dot(act, wd_ref[...], preferred_element_type=F32)


def _ffn(x2, g, wg, wu, wd, l, tm=512):
    n, d = x2.shape
    hid = wg.shape[-1]
    th = hid // 2 if (hid // 2) % 128 == 0 else hid
    return pl.pallas_call(
        _ffn_body,
        grid=(n // tm, hid // th),
        in_specs=[
            pl.BlockSpec((tm, d), lambda i, j: (i, 0)),
            pl.BlockSpec((1, d), lambda i, j: (0, 0)),
            pl.BlockSpec((None, d, th), lambda i, j: (l, 0, j)),
            pl.BlockSpec((None, d, th), lambda i, j: (l, 0, j)),
            pl.BlockSpec((None, th, d), lambda i, j: (l, j, 0)),
        ],
        out_specs=pl.BlockSpec((tm, d), lambda i, j: (i, 0)),
        out_shape=jax.ShapeDtypeStruct((n, d), F32),
        scratch_shapes=[pltpu.VMEM((tm, d), BF16)],
        compiler_params=_cparams("parallel", "arbitrary"),
        name="ffn",
    )(x2, g, wg, wu, wd)


def _block_diag(w):
    heads, a, b = w.shape
    eye = jnp.eye(heads, dtype=w.dtype)
    return (eye[:, None, :, None] * w[:, :, None, :]).reshape(heads * a, heads * b)


def kernel(x, ln1_g, w_in, b_gate, conv_w, conv_b, lru_wa, lru_ba, lru_wi, lru_bi, lru_lambda, sg_ln_g, sg_ln_b, sg_w, sg_b, q_norm_g, k_norm_g, lambda_q1, lambda_k1, lambda_q2, lambda_k2, subln_g, rel_bias, w_pa, w_pb, w_pc, w_o, ln2_g, w_ff_gate, w_ff_up, w_ff_down):
    bsz, s, d = x.shape
    depth = w_in.shape[0]
    lru_w = conv_w.shape[-1]
    sg_wd = sg_ln_g.shape[-1]
    groups = sg_w.shape[1]
    heads = rel_bias.shape[1]
    wq = heads * 2 * QK_DIM
    n_gate = 3 * d
    branch_cols = w_in.shape[-1] - n_gate
    assert branch_cols == 2 * lru_w + 2 * sg_wd + 3 * wq
    assert lru_w == sg_wd and n_gate % lru_w == 0 and (n_gate + 4 * lru_w) % wq == 0
    assert s % ATT_T == 0 and ATT_T % CHUNK == 0

    w_in_p = jnp.concatenate([w_in[..., branch_cols:], w_in[..., :branch_cols]], axis=-1).astype(BF16)
    xa_blk = n_gate // lru_w
    u_blk = xa_blk + 2
    q_blk = (n_gate + 4 * lru_w) // wq

    nd = -(-(MAX_DISTANCE // 2 + ATT_T) // ATT_T)
    bias = _bias_tiles(rel_bias, ATT_T, nd)
    bd = _block_diag(jnp.ones((wq // QK_DIM, QK_DIM, QK_DIM), BF16))
    w_pa_b, w_pb_b, w_pc_b, w_o_b = (w.astype(BF16) for w in (w_pa, w_pb, w_pc, w_o))
    wg_b, wu_b, wd_b = (w.astype(BF16) for w in (w_ff_gate, w_ff_up, w_ff_down))
    causal = jnp.tril(jnp.ones((SG_LEN, SG_LEN), bool))

    x2 = x.reshape(bsz * s, d)
    for l in range(depth):
        lam_init = 0.8 - 0.6 * math.exp(-0.3 * l)
        proj = _inproj(x2, ln1_g[l][None], w_in_p, l)
        proj3 = proj.reshape(bsz, s, -1)

        cvec = (-LRU_C * jax.nn.softplus(-lru_lambda[l]))[None]
        ya = _lru(proj3, xa_blk, conv_w[l], conv_b[l][None], _block_diag(lru_wa[l]).astype(BF16),
                  lru_ba[l][None], _block_diag(lru_wi[l]).astype(BF16), lru_bi[l][None], cvec)

        ws = jnp.where(causal, sg_w[l], 0.0).astype(BF16)
        ws_cat = jnp.transpose(ws, (1, 0, 2)).reshape(SG_LEN, groups * SG_LEN)
        bs_full = jnp.repeat(sg_b[l].T, sg_wd // groups, axis=1)
        yb = _sgu(proj, u_blk, sg_ln_g[l][None], sg_ln_b[l][None], ws_cat, bs_full, groups)

        gq = jnp.tile(q_norm_g[l] * (QK_DIM ** -0.5 * LOG2E), wq // QK_DIM)[None]
        gk = jnp.tile(k_norm_g[l], wq // QK_DIM)[None]
        qt, kn, vt = _prep(proj3, q_blk, gq, gk, bd, heads)
        lam4 = jnp.stack([lambda_q1[l], lambda_k1[l], lambda_q2[l], lambda_k2[l]])
        subg = jnp.broadcast_to(subln_g[l][:, None], (V_DIM, ATT_T))
        yc = _attn(qt, kn, vt, bias, lam4, subg, lam_init)

        x2 = _merge(x2, ya.reshape(bsz * s, -1), yb, yc.reshape(bsz * s, -1), proj,
                    b_gate[l].reshape(3, 1, d), w_pa_b, w_pb_b, w_pc_b, w_o_b, l)
        x2 = _ffn(x2, ln2_g[l][None], wg_b, wu_b, wd_b, l)
    return x2.reshape(bsz, s, d)
```

```python
import functools
import math

import jax
import jax.numpy as jnp
from jax import lax
from jax.experimental import pallas as pl
from jax.experimental.pallas import tpu as pltpu

F32 = jnp.float32
BF16 = jnp.bfloat16

EPS = 1e-6
CHUNK = 64
LRU_C = 8.0
SG_LEN = 128
QK_DIM = 64
V_DIM = 128
MAX_DISTANCE = 2048
NEG_BIG = -1e30
LOG2E = math.log2(math.e)

ATT_T = 512
VMEM_LIMIT = 56 * 1024 * 1024


def _cparams(*sem):
    return pltpu.CompilerParams(dimension_semantics=sem, vmem_limit_bytes=VMEM_LIMIT)


def _inproj_body(x_ref, g_ref, w_ref, o_ref, h_ref):
    @pl.when(pl.program_id(1) == 0)
    def _():
        x = x_ref[...]
        ms = jnp.mean(x * x, axis=-1, keepdims=True)
        h_ref[...] = (x * lax.rsqrt(ms + EPS) * g_ref[...]).astype(BF16)

    o_ref[...] = jnp.dot(h_ref[...], w_ref[...], preferred_element_type=F32).astype(o_ref.dtype)


def _inproj(x2, g, w, l, tm=1024, tn=512):
    n, d = x2.shape
    c = w.shape[-1]
    return pl.pallas_call(
        _inproj_body,
        grid=(n // tm, c // tn),
        in_specs=[
            pl.BlockSpec((tm, d), lambda i, j: (i, 0)),
            pl.BlockSpec((1, d), lambda i, j: (0, 0)),
            pl.BlockSpec((None, d, tn), lambda i, j: (l, 0, j)),
        ],
        out_specs=pl.BlockSpec((tm, tn), lambda i, j: (i, j)),
        out_shape=jax.ShapeDtypeStruct((n, c), BF16),
        scratch_shapes=[pltpu.VMEM((tm, d), BF16)],
        compiler_params=_cparams("parallel", "arbitrary"),
        name="inproj",
    )(x2, g, w)


def _lru_body(xa_ref, ga_ref, cw_ref, cb_ref, wa_ref, ba_ref, wi_ref, bi_ref, c_ref, o_ref,
              ext_ref, h_ref, *, ts):
    w = xa_ref.shape[-1]

    @pl.when(pl.program_id(1) == 0)
    def _():
        ext_ref[0:8, :] = jnp.zeros((8, w), F32)
        h_ref[...] = jnp.zeros_like(h_ref)

    xa = xa_ref[...].astype(F32)
    ext_ref[8:8 + ts, :] = xa
    xc = (cb_ref[...] + ext_ref[5:5 + ts, :] * cw_ref[0:1, :] + ext_ref[6:6 + ts, :] * cw_ref[1:2, :]
          + ext_ref[7:7 + ts, :] * cw_ref[2:3, :] + xa * cw_ref[3:4, :])
    ext_ref[0:8, :] = ext_ref[ts:ts + 8, :]

    xcb = xc.astype(BF16)
    r = jax.nn.sigmoid(jnp.dot(xcb, wa_ref[...], preferred_element_type=F32) + ba_ref[...])
    i = jax.nn.sigmoid(jnp.dot(xcb, wi_ref[...], preferred_element_type=F32) + bi_ref[...])
    log_a = c_ref[...] * r
    a = jnp.exp(log_a)
    mult = jnp.sqrt(-jnp.tanh(log_a) * (a * a + 1.0))
    b = mult * (i * xc)

    row = lax.broadcasted_iota(jnp.int32, (ts, w), 0)
    d = 1
    while d < ts:
        keep = row >= d
        a_sh = jnp.where(keep, pltpu.roll(a, d, 0), 1.0)
        b_sh = jnp.where(keep, pltpu.roll(b, d, 0), 0.0)
        b = a * b_sh + b
        a = a * a_sh
        d *= 2
    h = a * h_ref[...] + b
    h_ref[...] = h[ts - 1:ts, :]
    o_ref[...] = (h * jax.nn.gelu(ga_ref[...].astype(F32))).astype(o_ref.dtype)


def _lru(proj3, xa_blk, cw, cb, wa, ba, wi, bi, cvec, ts=512):
    bsz, s, _ = proj3.shape
    w = cw.shape[-1]
    full = lambda shp: pl.BlockSpec(shp, lambda b, t: (0,) * len(shp))
    return pl.pallas_call(
        functools.partial(_lru_body, ts=ts),
        grid=(bsz, s // ts),
        in_specs=[
            pl.BlockSpec((None, ts, w), lambda b, t: (b, t, xa_blk)),
            pl.BlockSpec((None, ts, w), lambda b, t: (b, t, xa_blk + 1)),
            full((4, w)), full((1, w)), full((w, w)), full((1, w)), full((w, w)), full((1, w)),
            full((1, w)),
        ],
        out_specs=pl.BlockSpec((None, ts, w), lambda b, t: (b, t, 0)),
        out_shape=jax.ShapeDtypeStruct((bsz, s, w), BF16),
        scratch_shapes=[pltpu.VMEM((ts + 8, w), F32), pltpu.VMEM((1, w), F32)],
        compiler_params=_cparams("parallel", "arbitrary"),
        name="lru",
    )(proj3, proj3, cw, cb, wa, ba, wi, bi, cvec)


def _sgu_body(u_ref, v_ref, g_ref, b_ref, ws_ref, bs_ref, o_ref, *, tr, groups):
    w = v_ref.shape[-1]
    gd = w // groups
    v = v_ref[...].astype(F32)
    mu = jnp.mean(v, axis=-1, keepdims=True)
    vc = v - mu
    var = jnp.mean(vc * vc, axis=-1, keepdims=True)
    vn = (vc * lax.rsqrt(var + EPS) * g_ref[...] + b_ref[...]).astype(BF16)
    lane_grp = lax.broadcasted_iota(jnp.int32, (SG_LEN, w), 1) // gd
    ws = ws_ref[...]
    zero = jnp.zeros((SG_LEN, w), BF16)
    for blk in range(tr // SG_LEN):
        rows = slice(blk * SG_LEN, (blk + 1) * SG_LEN)
        vb = vn[rows]
        stacked = jnp.concatenate([jnp.where(lane_grp == g, vb, zero) for g in range(groups)], axis=0)
        mixed = jnp.dot(ws, stacked, preferred_element_type=F32) + bs_ref[...]
        o_ref[rows, :] = (u_ref[rows, :].astype(F32) * mixed).astype(o_ref.dtype)


def _sgu(proj, u_blk, ln_g, ln_b, ws_cat, bs_full, groups, tr=512):
    n, _ = proj.shape
    w = ln_g.shape[-1]
    full = lambda shp: pl.BlockSpec(shp, lambda i: (0,) * len(shp))
    return pl.pallas_call(
        functools.partial(_sgu_body, tr=tr, groups=groups),
        grid=(n // tr,),
        in_specs=[
            pl.BlockSpec((tr, w), lambda i: (i, u_blk)),
            pl.BlockSpec((tr, w), lambda i: (i, u_blk + 1)),
            full((1, w)), full((1, w)), full(ws_cat.shape), full(bs_full.shape),
        ],
        out_specs=pl.BlockSpec((tr, w), lambda i: (i, 0)),
        out_shape=jax.ShapeDtypeStruct((n, w), BF16),
        compiler_params=_cparams("parallel"),
        name="sgu",
    )(proj, proj, ln_g, ln_b, ws_cat, bs_full)


def _prep_body(q_ref, k_ref, v_ref, gq_ref, gk_ref, bd_ref, qt_ref, kn_ref, vt_ref, *, heads):
    def group_rms(y_ref, g_ref):
        y = y_ref[...].astype(F32)
        y2 = y * y
        hi = y2.astype(BF16)
        lo = (y2 - hi.astype(F32)).astype(BF16)
        ss = (jnp.dot(hi, bd_ref[...], preferred_element_type=F32)
              + jnp.dot(lo, bd_ref[...], preferred_element_type=F32))
        return y * lax.rsqrt(ss * (1.0 / QK_DIM) + EPS) * g_ref[...]

    qn = group_rms(q_ref, gq_ref)
    kn_ref[...] = group_rms(k_ref, gk_ref).astype(BF16)
    v = v_ref[...].astype(F32)
    for h in range(heads):
        cols = slice(h * V_DIM, (h + 1) * V_DIM)
        qt_ref[h] = qn[:, cols].T.astype(BF16)
        vt_ref[h] = v[:, cols].T.astype(BF16)


def _prep(proj3, q_blk, gq, gk, bd, heads, tm=512):
    bsz, s, _ = proj3.shape
    wq = heads * 2 * QK_DIM
    full = lambda shp: pl.BlockSpec(shp, lambda b, t: (0,) * len(shp))
    tspec = pl.BlockSpec((None, heads, V_DIM, tm), lambda b, t: (b, 0, 0, t))
    return pl.pallas_call(
        functools.partial(_prep_body, heads=heads),
        grid=(bsz, s // tm),
        in_specs=[
            pl.BlockSpec((None, tm, wq), lambda b, t: (b, t, q_blk)),
            pl.BlockSpec((None, tm, wq), lambda b, t: (b, t, q_blk + 1)),
            pl.BlockSpec((None, tm, wq), lambda b, t: (b, t, q_blk + 2)),
            full((1, wq)), full((1, wq)), full((wq, wq)),
        ],
        out_specs=[tspec, pl.BlockSpec((None, tm, wq), lambda b, t: (b, t, 0)), tspec],
        out_shape=[
            jax.ShapeDtypeStruct((bsz, heads, V_DIM, s), BF16),
            jax.ShapeDtypeStruct((bsz, s, wq), BF16),
            jax.ShapeDtypeStruct((bsz, heads, V_DIM, s), BF16),
        ],
        compiler_params=_cparams("parallel", "parallel"),
        name="qkv_prep",
    )(proj3, proj3, proj3, gq, gk, bd)


def _attn_body(qt_ref, k_ref, vt_ref, bias_ref, lam_ref, subg_ref, o_ref, acc_ref, s_ref, *, t, nd,
               lam_init):
    qi = pl.program_id(2)
    qt = qt_ref[...]
    row = lax.broadcasted_iota(jnp.int32, qt.shape, 0)
    zero = jnp.zeros_like(qt)
    q1 = jnp.where(row < QK_DIM, qt, zero)
    q2 = jnp.where(row >= QK_DIM, qt, zero)
    acc_ref[...] = jnp.zeros_like(acc_ref)

    def scores(kj):
        k = k_ref[pl.ds(pl.multiple_of(kj * t, t), t), :]
        return (jnp.dot(k, q1, preferred_element_type=F32),
                jnp.dot(k, q2, preferred_element_type=F32))

    def colmax(s):
        return jnp.max(s, axis=0, keepdims=True)

    def consume(idx, kj, s, mc, m, l):
        vt = vt_ref[:, pl.ds(pl.multiple_of(kj * t, t), t)]
        mn = jnp.maximum(m, mc)
        alpha = jnp.exp2(m - mn)
        p = jnp.exp2(s - mn)
        l = alpha * l + jnp.sum(p, axis=0, keepdims=True)
        acc_ref[idx] = acc_ref[idx] * alpha + jnp.dot(vt, p.astype(BF16), preferred_element_type=F32)
        return mn, l

    n_far = jnp.maximum(qi - (nd - 1), 0)
    last_far = jnp.maximum(n_far - 1, 0)

    def produce(slot, kj):
        s1, s2 = scores(jnp.minimum(kj, last_far))
        s_ref[slot, 0] = s1
        s_ref[slot, 1] = s2
        return colmax(s1), colmax(s2)

    def consume_slot(slot, kj, c, st):
        m1, l1 = consume(0, kj, s_ref[slot, 0], c[0], st[0], st[1])
        m2, l2 = consume(1, kj, s_ref[slot, 1], c[1], st[2], st[3])
        return m1, l1, m2, l2

    def far_pair(i, carry):
        st, c0 = carry
        c1 = produce(1, 2 * i + 1)
        st = consume_slot(0, 2 * i, c0, st)
        c0 = produce(0, 2 * i + 2)
        st = consume_slot(1, 2 * i + 1, c1, st)
        return st, c0

    neg = jnp.full((1, t), NEG_BIG, F32)
    zer = jnp.zeros((1, t), F32)
    st, c0 = lax.fori_loop(0, n_far // 2, far_pair, ((neg, zer, neg, zer), produce(0, 0)))
    st = lax.cond(n_far % 2 == 1, lambda: consume_slot(0, last_far, c0, st), lambda: st)

    def near_step(kj, st):
        s1, s2 = scores(kj)
        b = bias_ref[qi - kj]
        s1 = s1 + b
        s2 = s2 + b
        m1, l1 = consume(0, kj, s1, colmax(s1), st[0], st[1])
        m2, l2 = consume(1, kj, s2, colmax(s2), st[2], st[3])
        return m1, l1, m2, l2

    _, l1, _, l2 = lax.fori_loop(n_far, qi + 1, near_step, st)

    lam4 = lam_ref[...]
    lam = (jnp.exp(jnp.sum(lam4[0:1] * lam4[1:2], axis=-1, keepdims=True))
           - jnp.exp(jnp.sum(lam4[2:3] * lam4[3:4], axis=-1, keepdims=True)) + lam_init)
    o = acc_ref[0] / l1 - lam * (acc_ref[1] / l2)
    ms = jnp.mean(o * o, axis=0, keepdims=True)
    o = o * lax.rsqrt(ms + EPS) * subg_ref[...] * (1.0 - lam_init)
    o_ref[...] = o.T.astype(o_ref.dtype)


def _attn(qt, kn, vt, bias, lam4, subg, lam_init, t=ATT_T):
    bsz, heads, _, s = qt.shape
    nd = bias.shape[1]
    return pl.pallas_call(
        functools.partial(_attn_body, t=t, nd=nd, lam_init=lam_init),
        grid=(bsz, heads, s // t),
        in_specs=[
            pl.BlockSpec((None, None, V_DIM, t), lambda b, h, i: (b, h, 0, i)),
            pl.BlockSpec((None, s, V_DIM), lambda b, h, i: (b, 0, h)),
            pl.BlockSpec((None, None, V_DIM, s), lambda b, h, i: (b, h, 0, 0)),
            pl.BlockSpec((None, nd, t, t), lambda b, h, i: (h, 0, 0, 0)),
            pl.BlockSpec(lam4.shape, lambda b, h, i: (0, 0)),
            pl.BlockSpec(subg.shape, lambda b, h, i: (0, 0)),
        ],
        out_specs=pl.BlockSpec((None, t, V_DIM), lambda b, h, i: (b, i, h)),
        out_shape=jax.ShapeDtypeStruct((bsz, s, heads * V_DIM), BF16),
        scratch_shapes=[pltpu.VMEM((2, V_DIM, t), F32), pltpu.VMEM((2, 2, t, t), F32)],
        compiler_params=_cparams("parallel", "parallel", "arbitrary"),
        name="diff_attn",
    )(qt, kn, vt, bias, lam4, subg)


def _t5_bucket(rel, n_buckets):
    half = n_buckets // 2
    max_exact = half // 2
    ret = jnp.where(rel > 0, half, 0)
    n = jnp.abs(rel)
    nf = jnp.maximum(n, 1).astype(F32)
    large = max_exact + (jnp.log(nf / max_exact) / math.log(MAX_DISTANCE / max_exact)
                         * (half - max_exact)).astype(jnp.int32)
    large = jnp.minimum(large, half - 1)
    return ret + jnp.where(n < max_exact, n, large)


def _bias_body(f_ref, o_ref, *, t):
    x = jnp.broadcast_to(f_ref[...], (t, 2 * t))
    y = pltpu.roll(x, t + 1, 1, stride=1, stride_axis=0)[:, :t]
    j = lax.broadcasted_iota(jnp.int32, (t, t), 0)
    i = lax.broadcasted_iota(jnp.int32, (t, t), 1)
    allowed = ((j // CHUNK) <= (i // CHUNK)) | (pl.program_id(1) > 0)
    o_ref[...] = jnp.where(allowed, y, NEG_BIG)


def _bias_tiles(rel_bias, t, nd):
    n_buckets, heads = rel_bias.shape
    c = jnp.arange(2 * t, dtype=jnp.int32)[None, :]
    delta = jnp.arange(nd, dtype=jnp.int32)[:, None]
    rel = (t - 1 - c) - delta * t
    f = (rel_bias[_t5_bucket(rel, n_buckets)] - rel_bias[n_buckets // 2 - 1]) * LOG2E
    f = jnp.transpose(f, (2, 0, 1))[:, :, None, :]
    return pl.pallas_call(
        functools.partial(_bias_body, t=t),
        grid=(heads, nd),
        in_specs=[pl.BlockSpec((None, None, 1, 2 * t), lambda h, d: (h, d, 0, 0))],
        out_specs=pl.BlockSpec((None, None, t, t), lambda h, d: (h, d, 0, 0)),
        out_shape=jax.ShapeDtypeStruct((heads, nd, t, t), F32),
        compiler_params=_cparams("parallel", "parallel"),
        name="bias_tiles",
    )(f)


def _merge_body(x_ref, ya_ref, yb_ref, yc_ref, g0_ref, g1_ref, g2_ref, bg_ref, wpa_ref, wpb_ref,
                wpc_ref, wo_ref, o_ref):
    def branch(idx, y_ref, w_ref, g_ref):
        p = jnp.dot(y_ref[...], w_ref[...], preferred_element_type=F32)
        return jax.nn.sigmoid(g_ref[...].astype(F32) + bg_ref[idx]) * p

    merged = (branch(0, ya_ref, wpa_ref, g0_ref) + branch(1, yb_ref, wpb_ref, g1_ref)
              + branch(2, yc_ref, wpc_ref, g2_ref))
    o_ref[...] = x_ref[...] + jnp.dot(merged.astype(BF16), wo_ref[...], preferred_element_type=F32)


def _merge(x2, ya, yb, yc, proj, bg, wpa, wpb, wpc, wo, l, tm=512):
    n, d = x2.shape
    rows = lambda a: pl.BlockSpec((tm, a.shape[-1]), lambda i: (i, 0))
    wspec = lambda a: pl.BlockSpec((None,) + a.shape[1:], lambda i: (l, 0, 0))
    gate = lambda g: pl.BlockSpec((tm, d), lambda i: (i, g))
    return pl.pallas_call(
        _merge_body,
        grid=(n // tm,),
        in_specs=[rows(x2), rows(ya), rows(yb), rows(yc), gate(0), gate(1), gate(2),
                  pl.BlockSpec(bg.shape, lambda i: (0, 0, 0)),
                  wspec(wpa), wspec(wpb), wspec(wpc), wspec(wo)],
        out_specs=pl.BlockSpec((tm, d), lambda i: (i, 0)),
        out_shape=jax.ShapeDtypeStruct((n, d), F32),
        compiler_params=_cparams("parallel"),
        name="merge",
    )(x2, ya, yb, yc, proj, proj, proj, bg, wpa, wpb, wpc, wo)


def _ffn_body(x_ref, g_ref, wg_ref, wu_ref, wd_ref, o_ref, h_ref):
    @pl.when(pl.program_id(1) == 0)
    def _():
        x = x_ref[...]
        ms = jnp.mean(x * x, axis=-1, keepdims=True)
        h_ref[...] = (x * lax.rsqrt(ms + EPS) * g_ref[...]).astype(BF16)
        o_ref[...] = x

    h = h_ref[...]
    gate = jnp.dot(h, wg_ref[...], preferred_element_type=F32)
    up = jnp.dot(h, wu_ref[...], preferred_element_type=F32)
    act = (jax.nn.silu(gate) * up).astype(BF16)
    o_ref[...] += jnp.dot(act, wd_ref[...], preferred_element_type=F32)


def _ffn(x2, g, wg, wu, wd, l, tm=512):
    n, d = x2.shape
    hid = wg.shape[-1]
    th = hid // 2 if (hid // 2) % 128 == 0 else hid
    return pl.pallas_call(
        _ffn_body,
        grid=(n // tm, hid // th),
        in_specs=[
            pl.BlockSpec((tm, d), lambda i, j: (i, 0)),
            pl.BlockSpec((1, d), lambda i, j: (0, 0)),
            pl.BlockSpec((None, d, th), lambda i, j: (l, 0, j)),
            pl.BlockSpec((None, d, th), lambda i, j: (l, 0, j)),
            pl.BlockSpec((None, th, d), lambda i, j: (l, j, 0)),
        ],
        out_specs=pl.BlockSpec((tm, d), lambda i, j: (i, 0)),
        out_shape=jax.ShapeDtypeStruct((n, d), F32),
        scratch_shapes=[pltpu.VMEM((tm, d), BF16)],
        compiler_params=_cparams("parallel", "arbitrary"),
        name="ffn",
    )(x2, g, wg, wu, wd)


def _block_diag(w):
    heads, a, b = w.shape
    eye = jnp.eye(heads, dtype=w.dtype)
    return (eye[:, None, :, None] * w[:, :, None, :]).reshape(heads * a, heads * b)


def kernel(x, ln1_g, w_in, b_gate, conv_w, conv_b, lru_wa, lru_ba, lru_wi, lru_bi, lru_lambda, sg_ln_g, sg_ln_b, sg_w, sg_b, q_norm_g, k_norm_g, lambda_q1, lambda_k1, lambda_q2, lambda_k2, subln_g, rel_bias, w_pa, w_pb, w_pc, w_o, ln2_g, w_ff_gate, w_ff_up, w_ff_down):
    bsz, s, d = x.shape
    depth = w_in.shape[0]
    lru_w = conv_w.shape[-1]
    sg_wd = sg_ln_g.shape[-1]
    groups = sg_w.shape[1]
    heads = rel_bias.shape[1]
    wq = heads * 2 * QK_DIM
    n_gate = 3 * d
    branch_cols = w_in.shape[-1] - n_gate
    assert branch_cols == 2 * lru_w + 2 * sg_wd + 3 * wq
    assert lru_w == sg_wd and n_gate % lru_w == 0 and (n_gate + 4 * lru_w) % wq == 0
    assert s % ATT_T == 0 and ATT_T % CHUNK == 0

    w_in_p = jnp.concatenate([w_in[..., branch_cols:], w_in[..., :branch_cols]], axis=-1).astype(BF16)
    xa_blk = n_gate // lru_w
    u_blk = xa_blk + 2
    q_blk = (n_gate + 4 * lru_w) // wq

    nd = -(-(MAX_DISTANCE // 2 + ATT_T) // ATT_T)
    bias = _bias_tiles(rel_bias, ATT_T, nd)
    bd = _block_diag(jnp.ones((wq // QK_DIM, QK_DIM, QK_DIM), BF16))
    w_pa_b, w_pb_b, w_pc_b, w_o_b = (w.astype(BF16) for w in (w_pa, w_pb, w_pc, w_o))
    wg_b, wu_b, wd_b = (w.astype(BF16) for w in (w_ff_gate, w_ff_up, w_ff_down))
    causal = jnp.tril(jnp.ones((SG_LEN, SG_LEN), bool))

    x2 = x.reshape(bsz * s, d)
    for l in range(depth):
        lam_init = 0.8 - 0.6 * math.exp(-0.3 * l)
        proj = _inproj(x2, ln1_g[l][None], w_in_p, l)
        proj3 = proj.reshape(bsz, s, -1)

        cvec = (-LRU_C * jax.nn.softplus(-lru_lambda[l]))[None]
        ya = _lru(proj3, xa_blk, conv_w[l], conv_b[l][None], _block_diag(lru_wa[l]).astype(BF16),
                  lru_ba[l][None], _block_diag(lru_wi[l]).astype(BF16), lru_bi[l][None], cvec)

        ws = jnp.where(causal, sg_w[l], 0.0).astype(BF16)
        ws_cat = jnp.transpose(ws, (1, 0, 2)).reshape(SG_LEN, groups * SG_LEN)
        bs_full = jnp.repeat(sg_b[l].T, sg_wd // groups, axis=1)
        yb = _sgu(proj, u_blk, sg_ln_g[l][None], sg_ln_b[l][None], ws_cat, bs_full, groups)

        gq = jnp.tile(q_norm_g[l] * (QK_DIM ** -0.5 * LOG2E), wq // QK_DIM)[None]
        gk = jnp.tile(k_norm_g[l], wq // QK_DIM)[None]
        qt, kn, vt = _prep(proj3, q_blk, gq, gk, bd, heads)
        lam4 = jnp.stack([lambda_q1[l], lambda_k1[l], lambda_q2[l], lambda_k2[l]])
        subg = jnp.broadcast_to(subln_g[l][:, None], (V_DIM, ATT_T))
        yc = _attn(qt, kn, vt, bias, lam4, subg, lam_init)

        x2 = _merge(x2, ya.reshape(bsz * s, -1), yb, yc.reshape(bsz * s, -1), proj,
                    b_gate[l].reshape(3, 1, d), w_pa_b, w_pb_b, w_pc_b, w_o_b, l)
        x2 = _ffn(x2, ln2_g[l][None], wg_b, wu_b, wd_b, l)
    return x2.reshape(bsz, s, d)
```

```python
import functools
import math

import jax
import jax.numpy as jnp
from jax import lax
from jax.experimental import pallas as pl
from jax.experimental.pallas import tpu as pltpu

F32 = jnp.float32
BF16 = jnp.bfloat16

EPS = 1e-6
CHUNK = 64
LRU_C = 8.0
SG_LEN = 128
QK_DIM = 64
V_DIM = 128
MAX_DISTANCE = 2048
NEG_BIG = -1e30
LOG2E = math.log2(math.e)

ATT_T = 512
VMEM_LIMIT = 56 * 1024 * 1024


def _cparams(*sem):
    return pltpu.CompilerParams(dimension_semantics=sem, vmem_limit_bytes=VMEM_LIMIT)


def _inproj_body(x_ref, g_ref, w_ref, o_ref, h_ref):
    @pl.when(pl.program_id(1) == 0)
    def _():
        x = x_ref[...]
        ms = jnp.mean(x * x, axis=-1, keepdims=True)
        h_ref[...] = (x * lax.rsqrt(ms + EPS) * g_ref[...]).astype(BF16)

    o_ref[...] = jnp.dot(h_ref[...], w_ref[...], preferred_element_type=F32).astype(o_ref.dtype)


def _inproj(x2, g, w, l, tm=1024, tn=512):
    n, d = x2.shape
    c = w.shape[-1]
    return pl.pallas_call(
        _inproj_body,
        grid=(n // tm, c // tn),
        in_specs=[
            pl.BlockSpec((tm, d), lambda i, j: (i, 0)),
            pl.BlockSpec((1, d), lambda i, j: (0, 0)),
            pl.BlockSpec((None, d, tn), lambda i, j: (l, 0, j)),
        ],
        out_specs=pl.BlockSpec((tm, tn), lambda i, j: (i, j)),
        out_shape=jax.ShapeDtypeStruct((n, c), BF16),
        scratch_shapes=[pltpu.VMEM((tm, d), BF16)],
        compiler_params=_cparams("parallel", "arbitrary"),
        name="inproj",
    )(x2, g, w)


def _lru_body(xa_ref, ga_ref, cw_ref, cb_ref, wa_ref, ba_ref, wi_ref, bi_ref, c_ref, o_ref,
              ext_ref, h_ref, *, ts):
    w = xa_ref.shape[-1]

    @pl.when(pl.program_id(1) == 0)
    def _():
        ext_ref[0:8, :] = jnp.zeros((8, w), F32)
        h_ref[...] = jnp.zeros_like(h_ref)

    xa = xa_ref[...].astype(F32)
    ext_ref[8:8 + ts, :] = xa
    xc = (cb_ref[...] + ext_ref[5:5 + ts, :] * cw_ref[0:1, :] + ext_ref[6:6 + ts, :] * cw_ref[1:2, :]
          + ext_ref[7:7 + ts, :] * cw_ref[2:3, :] + xa * cw_ref[3:4, :])
    ext_ref[0:8, :] = ext_ref[ts:ts + 8, :]

    xcb = xc.astype(BF16)
    r = jax.nn.sigmoid(jnp.dot(xcb, wa_ref[...], preferred_element_type=F32) + ba_ref[...])
    i = jax.nn.sigmoid(jnp.dot(xcb, wi_ref[...], preferred_element_type=F32) + bi_ref[...])
    log_a = c_ref[...] * r
    a = jnp.exp(log_a)
    mult = jnp.sqrt(-jnp.tanh(log_a) * (a * a + 1.0))
    b = mult * (i * xc)

    row = lax.broadcasted_iota(jnp.int32, (ts, w), 0)
    d = 1
    while d < ts:
        keep = row >= d
        a_sh = jnp.where(keep, pltpu.roll(a, d, 0), 1.0)
        b_sh = jnp.where(keep, pltpu.roll(b, d, 0), 0.0)
        b = a * b_sh + b
        a = a * a_sh
        d *= 2
    h = a * h_ref[...] + b
    h_ref[...] = h[ts - 1:ts, :]
    o_ref[...] = (h * jax.nn.gelu(ga_ref[...].astype(F32))).astype(o_ref.dtype)


def _lru(proj3, xa_blk, cw, cb, wa, ba, wi, bi, cvec, ts=512):
    bsz, s, _ = proj3.shape
    w = cw.shape[-1]
    full = lambda shp: pl.BlockSpec(shp, lambda b, t: (0,) * len(shp))
    return pl.pallas_call(
        functools.partial(_lru_body, ts=ts),
        grid=(bsz, s // ts),
        in_specs=[
            pl.BlockSpec((None, ts, w), lambda b, t: (b, t, xa_blk)),
            pl.BlockSpec((None, ts, w), lambda b, t: (b, t, xa_blk + 1)),
            full((4, w)), full((1, w)), full((w, w)), full((1, w)), full((w, w)), full((1, w)),
            full((1, w)),
        ],
        out_specs=pl.BlockSpec((None, ts, w), lambda b, t: (b, t, 0)),
        out_shape=jax.ShapeDtypeStruct((bsz, s, w), BF16),
        scratch_shapes=[pltpu.VMEM((ts + 8, w), F32), pltpu.VMEM((1, w), F32)],
        compiler_params=_cparams("parallel", "arbitrary"),
        name="lru",
    )(proj3, proj3, cw, cb, wa, ba, wi, bi, cvec)


def _sgu_body(u_ref, v_ref, g_ref, b_ref, ws_ref, bs_ref, o_ref, *, tr, groups):
    w = v_ref.shape[-1]
    gd = w // groups
    v = v_ref[...].astype(F32)
    mu = jnp.mean(v, axis=-1, keepdims=True)
    vc = v - mu
    var = jnp.mean(vc * vc, axis=-1, keepdims=True)
    vn = (vc * lax.rsqrt(var + EPS) * g_ref[...] + b_ref[...]).astype(BF16)
    lane_grp = lax.broadcasted_iota(jnp.int32, (SG_LEN, w), 1) // gd
    ws = ws_ref[...]
    zero = jnp.zeros((SG_LEN, w), BF16)
    for blk in range(tr // SG_LEN):
        rows = slice(blk * SG_LEN, (blk + 1) * SG_LEN)
        vb = vn[rows]
        stacked = jnp.concatenate([jnp.where(lane_grp == g, vb, zero) for g in range(groups)], axis=0)
        mixed = jnp.dot(ws, stacked, preferred_element_type=F32) + bs_ref[...]
        o_ref[rows, :] = (u_ref[rows, :].astype(F32) * mixed).astype(o_ref.dtype)


def _sgu(proj, u_blk, ln_g, ln_b, ws_cat, bs_full, groups, tr=512):
    n, _ = proj.shape
    w = ln_g.shape[-1]
    full = lambda shp: pl.BlockSpec(shp, lambda i: (0,) * len(shp))
    return pl.pallas_call(
        functools.partial(_sgu_body, tr=tr, groups=groups),
        grid=(n // tr,),
        in_specs=[
            pl.BlockSpec((tr, w), lambda i: (i, u_blk)),
            pl.BlockSpec((tr, w), lambda i: (i, u_blk + 1)),
            full((1, w)), full((1, w)), full(ws_cat.shape), full(bs_full.shape),
        ],
        out_specs=pl.BlockSpec((tr, w), lambda i: (i, 0)),
        out_shape=jax.ShapeDtypeStruct((n, w), BF16),
        compiler_params=_cparams("parallel"),
        name="sgu",
    )(proj, proj, ln_g, ln_b, ws_cat, bs_full)


def _prep_body(q_ref, k_ref, v_ref, gq_ref, gk_ref, bd_ref, qt_ref, kn_ref, vt_ref, *, heads):
    def group_rms(y_ref, g_ref):
        y = y_ref[...].astype(F32)
        y2 = y * y
        hi = y2.astype(BF16)
        lo = (y2 - hi.astype(F32)).astype(BF16)
        ss = (jnp.dot(hi, bd_ref[...], preferred_element_type=F32)
              + jnp.dot(lo, bd_ref[...], preferred_element_type=F32))
        return y * lax.rsqrt(ss * (1.0 / QK_DIM) + EPS) * g_ref[...]

    qn = group_rms(q_ref, gq_ref)
    kn_ref[...] = group_rms(k_ref, gk_ref).astype(BF16)
    v = v_ref[...].astype(F32)
    for h in range(heads):
        cols = slice(h * V_DIM, (h + 1) * V_DIM)
        qt_ref[h] = qn[:, cols].T.astype(BF16)
        vt_ref[h] = v[:, cols].T.astype(BF16)


def _prep(proj3, q_blk, gq, gk, bd, heads, tm=512):
    bsz, s, _ = proj3.shape
    wq = heads * 2 * QK_DIM
    full = lambda shp: pl.BlockSpec(shp, lambda b, t: (0,) * len(shp))
    tspec = pl.BlockSpec((None, heads, V_DIM, tm), lambda b, t: (b, 0, 0, t))
    return pl.pallas_call(
        functools.partial(_prep_body, heads=heads),
        grid=(bsz, s // tm),
        in_specs=[
            pl.BlockSpec((None, tm, wq), lambda b, t: (b, t, q_blk)),
            pl.BlockSpec((None, tm, wq), lambda b, t: (b, t, q_blk + 1)),
            pl.BlockSpec((None, tm, wq), lambda b, t: (b, t, q_blk + 2)),
            full((1, wq)), full((1, wq)), full((wq, wq)),
        ],
        out_specs=[tspec, pl.BlockSpec((None, tm, wq), lambda b, t: (b, t, 0)), tspec],
        out_shape=[
            jax.ShapeDtypeStruct((bsz, heads, V_DIM, s), BF16),
            jax.ShapeDtypeStruct((bsz, s, wq), BF16),
            jax.ShapeDtypeStruct((bsz, heads, V_DIM, s), BF16),
        ],
        compiler_params=_cparams("parallel", "parallel"),
        name="qkv_prep",
    )(proj3, proj3, proj3, gq, gk, bd)


def _attn_body(qt_ref, k_ref, vt_ref, bias_ref, lam_ref, subg_ref, o_ref, acc_ref, s_ref, *, t, nd,
               lam_init):
    qi = pl.program_id(2)
    qt = qt_ref[...]
    row = lax.broadcasted_iota(jnp.int32, qt.shape, 0)
    zero = jnp.zeros_like(qt)
    q1 = jnp.where(row < QK_DIM, qt, zero)
    q2 = jnp.where(row >= QK_DIM, qt, zero)
    acc_ref[...] = jnp.zeros_like(acc_ref)

    def scores(kj):
        k = k_ref[pl.ds(pl.multiple_of(kj * t, t), t), :]
        return (jnp.dot(k, q1, preferred_element_type=F32),
                jnp.dot(k, q2, preferred_element_type=F32))

    def colmax(s):
        return jnp.max(s, axis=0, keepdims=True)

    def consume(idx, kj, s, mc, m, l):
        vt = vt_ref[:, pl.ds(pl.multiple_of(kj * t, t), t)]
        mn = jnp.maximum(m, mc)
        alpha = jnp.exp2(m - mn)
        p = jnp.exp2(s - mn)
        l = alpha * l + jnp.sum(p, axis=0, keepdims=True)
        acc_ref[idx] = acc_ref[idx] * alpha + jnp.dot(vt, p.astype(BF16), preferred_element_type=F32)
        return mn, l

    n_far = jnp.maximum(qi - (nd - 1), 0)
    last_far = jnp.maximum(n_far - 1, 0)

    def produce(slot, kj, bias_idx=None):
        s1, s2 = scores(kj)
        if bias_idx is not None:
            b = bias_ref[bias_idx]
            s1 = s1 + b
            s2 = s2 + b
        s_ref[slot, 0] = s1
        s_ref[slot, 1] = s2
        return colmax(s1), colmax(s2)

    def consume_slot(slot, kj, c, st):
        m1, l1 = consume(0, kj, s_ref[slot, 0], c[0], st[0], st[1])
        m2, l2 = consume(1, kj, s_ref[slot, 1], c[1], st[2], st[3])
        return m1, l1, m2, l2

    neg = jnp.full((1, t), NEG_BIG, F32)
    zer = jnp.zeros((1, t), F32)
    st = (neg, zer, neg, zer)

    pending = None
    for d in range(nd - 1, -1, -1):
        slot = (d + 1) % 2
        kj = jnp.maximum(qi - d, 0)
        c = produce(slot, kj, jnp.where(qi >= d, d, nd))
        if pending is not None:
            st = consume_slot(*pending, st)
        pending = (slot, kj, c)

    def produce_far(slot, kj):
        return produce(slot, jnp.minimum(kj, last_far))

    def far_pair(i, carry):
        st, c0 = carry
        c1 = produce_far(1, 2 * i + 1)
        st = consume_slot(0, 2 * i, c0, st)
        c0 = produce_far(0, 2 * i + 2)
        st = consume_slot(1, 2 * i + 1, c1, st)
        return st, c0

    c0 = produce_far(0, 0)
    st = consume_slot(*pending, st)
    st, c0 = lax.fori_loop(0, n_far // 2, far_pair, (st, c0))
    _, l1, _, l2 = lax.cond(n_far % 2 == 1, lambda: consume_slot(0, last_far, c0, st), lambda: st)

    lam4 = lam_ref[...]
    lam = (jnp.exp(jnp.sum(lam4[0:1] * lam4[1:2], axis=-1, keepdims=True))
           - jnp.exp(jnp.sum(lam4[2:3] * lam4[3:4], axis=-1, keepdims=True)) + lam_init)
    o = acc_ref[0] / l1 - lam * (acc_ref[1] / l2)
    ms = jnp.mean(o * o, axis=0, keepdims=True)
    o = o * lax.rsqrt(ms + EPS) * subg_ref[...] * (1.0 - lam_init)
    o_ref[...] = o.T.astype(o_ref.dtype)


def _attn(qt, kn, vt, bias, lam4, subg, lam_init, t=ATT_T):
    bsz, heads, _, s = qt.shape
    nd = bias.shape[1] - 1
    return pl.pallas_call(
        functools.partial(_attn_body, t=t, nd=nd, lam_init=lam_init),
        grid=(bsz, heads, s // t),
        in_specs=[
            pl.BlockSpec((None, None, V_DIM, t), lambda b, h, i: (b, h, 0, i)),
            pl.BlockSpec((None, s, V_DIM), lambda b, h, i: (b, 0, h)),
            pl.BlockSpec((None, None, V_DIM, s), lambda b, h, i: (b, h, 0, 0)),
            pl.BlockSpec((None, nd + 1, t, t), lambda b, h, i: (h, 0, 0, 0)),
            pl.BlockSpec(lam4.shape, lambda b, h, i: (0, 0)),
            pl.BlockSpec(subg.shape, lambda b, h, i: (0, 0)),
        ],
        out_specs=pl.BlockSpec((None, t, V_DIM), lambda b, h, i: (b, i, h)),
        out_shape=jax.ShapeDtypeStruct((bsz, s, heads * V_DIM), BF16),
        scratch_shapes=[pltpu.VMEM((2, V_DIM, t), F32), pltpu.VMEM((2, 2, t, t), F32)],
        compiler_params=_cparams("parallel", "parallel", "arbitrary"),
        name="diff_attn",
    )(qt, kn, vt, bias, lam4, subg)


def _t5_bucket(rel, n_buckets):
    half = n_buckets // 2
    max_exact = half // 2
    ret = jnp.where(rel > 0, half, 0)
    n = jnp.abs(rel)
    nf = jnp.maximum(n, 1).astype(F32)
    large = max_exact + (jnp.log(nf / max_exact) / math.log(MAX_DISTANCE / max_exact)
                         * (half - max_exact)).astype(jnp.int32)
    large = jnp.minimum(large, half - 1)
    return ret + jnp.where(n < max_exact, n, large)


def _bias_body(f_ref, o_ref, *, t, nd):
    x = jnp.broadcast_to(f_ref[...], (t, 2 * t))
    y = pltpu.roll(x, t + 1, 1, stride=1, stride_axis=0)[:, :t]
    j = lax.broadcasted_iota(jnp.int32, (t, t), 0)
    i = lax.broadcasted_iota(jnp.int32, (t, t), 1)
    delta = pl.program_id(1)
    allowed = (((j // CHUNK) <= (i // CHUNK)) | (delta > 0)) & (delta < nd)
    o_ref[...] = jnp.where(allowed, y, NEG_BIG)


def _bias_tiles(rel_bias, t, nd):
    n_buckets, heads = rel_bias.shape
    c = jnp.arange(2 * t, dtype=jnp.int32)[None, :]
    delta = jnp.arange(nd + 1, dtype=jnp.int32)[:, None]
    rel = (t - 1 - c) - delta * t
    f = (rel_bias[_t5_bucket(rel, n_buckets)] - rel_bias[n_buckets // 2 - 1]) * LOG2E
    f = jnp.transpose(f, (2, 0, 1))[:, :, None, :]
    return pl.pallas_call(
        functools.partial(_bias_body, t=t, nd=nd),
        grid=(heads, nd + 1),
        in_specs=[pl.BlockSpec((None, None, 1, 2 * t), lambda h, d: (h, d, 0, 0))],
        out_specs=pl.BlockSpec((None, None, t, t), lambda h, d: (h, d, 0, 0)),
        out_shape=jax.ShapeDtypeStruct((heads, nd + 1, t, t), F32),
        compiler_params=_cparams("parallel", "parallel"),
        name="bias_tiles",
    )(f)


def _merge_body(x_ref, ya_ref, yb_ref, yc_ref, g0_ref, g1_ref, g2_ref, bg_ref, wpa_ref, wpb_ref,
                wpc_ref, wo_ref, o_ref):
    def branch(idx, y_ref, w_ref, g_ref):
        p = jnp.dot(y_ref[...], w_ref[...], preferred_element_type=F32)
        return jax.nn.sigmoid(g_ref[...].astype(F32) + bg_ref[idx]) * p

    merged = (branch(0, ya_ref, wpa_ref, g0_ref) + branch(1, yb_ref, wpb_ref, g1_ref)
              + branch(2, yc_ref, wpc_ref, g2_ref))
    o_ref[...] = x_ref[...] + jnp.dot(merged.astype(BF16), wo_ref[...], preferred_element_type=F32)


def _merge(x2, ya, yb, yc, proj, bg, wpa, wpb, wpc, wo, l, tm=512):
    n, d = x2.shape
    rows = lambda a: pl.BlockSpec((tm, a.shape[-1]), lambda i: (i, 0))
    wspec = lambda a: pl.BlockSpec((None,) + a.shape[1:], lambda i: (l, 0, 0))
    gate = lambda g: pl.BlockSpec((tm, d), lambda i: (i, g))
    return pl.pallas_call(
        _merge_body,
        grid=(n // tm,),
        in_specs=[rows(x2), rows(ya), rows(yb), rows(yc), gate(0), gate(1), gate(2),
                  pl.BlockSpec(bg.shape, lambda i: (0, 0, 0)),
                  wspec(wpa), wspec(wpb), wspec(wpc), wspec(wo)],
        out_specs=pl.BlockSpec((tm, d), lambda i: (i, 0)),
        out_shape=jax.ShapeDtypeStruct((n, d), F32),
        compiler_params=_cparams("parallel"),
        name="merge",
    )(x2, ya, yb, yc, proj, proj, proj, bg, wpa, wpb, wpc, wo)


def _ffn_body(x_ref, g_ref, wg_ref, wu_ref, wd_ref, o_ref, h_ref):
    @pl.when(pl.program_id(1) == 0)
    def _():
        x = x_ref[...]
        ms = jnp.mean(x * x, axis=-1, keepdims=True)
        h_ref[...] = (x * lax.rsqrt(ms + EPS) * g_ref[...]).astype(BF16)
        o_ref[...] = x

    h = h_ref[...]
    gate = jnp.dot(h, wg_ref[...], preferred_element_type=F32)
    up = jnp.dot(h, wu_ref[...], preferred_element_type=F32)
    act = (jax.nn.silu(gate) * up).astype(BF16)
    o_ref[...] += jnp.dot(act, wd_ref[...], preferred_element_type=F32)


def _ffn(x2, g, wg, wu, wd, l, tm=512):
    n, d = x2.shape
    hid = wg.shape[-1]
    th = hid // 2 if (hid // 2) % 128 == 0 else hid
    return pl.pallas_call(
        _ffn_body,
        grid=(n // tm, hid // th),
        in_specs=[
            pl.BlockSpec((tm, d), lambda i, j: (i, 0)),
            pl.BlockSpec((1, d), lambda i, j: (0, 0)),
            pl.BlockSpec((None, d, th), lambda i, j: (l, 0, j)),
            pl.BlockSpec((None, d, th), lambda i, j: (l, 0, j)),
            pl.BlockSpec((None, th, d), lambda i, j: (l, j, 0)),
        ],
        out_specs=pl.BlockSpec((tm, d), lambda i, j: (i, 0)),
        out_shape=jax.ShapeDtypeStruct((n, d), F32),
        scratch_shapes=[pltpu.VMEM((tm, d), BF16)],
        compiler_params=_cparams("parallel", "arbitrary"),
        name="ffn",
    )(x2, g, wg, wu, wd)


def _block_diag(w):
    heads, a, b = w.shape
    eye = jnp.eye(heads, dtype=w.dtype)
    return (eye[:, None, :, None] * w[:, :, None, :]).reshape(heads * a, heads * b)


def kernel(x, ln1_g, w_in, b_gate, conv_w, conv_b, lru_wa, lru_ba, lru_wi, lru_bi, lru_lambda, sg_ln_g, sg_ln_b, sg_w, sg_b, q_norm_g, k_norm_g, lambda_q1, lambda_k1, lambda_q2, lambda_k2, subln_g, rel_bias, w_pa, w_pb, w_pc, w_o, ln2_g, w_ff_gate, w_ff_up, w_ff_down):
    bsz, s, d = x.shape
    depth = w_in.shape[0]
    lru_w = conv_w.shape[-1]
    sg_wd = sg_ln_g.shape[-1]
    groups = sg_w.shape[1]
    heads = rel_bias.shape[1]
    wq = heads * 2 * QK_DIM
    n_gate = 3 * d
    branch_cols = w_in.shape[-1] - n_gate
    assert branch_cols == 2 * lru_w + 2 * sg_wd + 3 * wq
    assert lru_w == sg_wd and n_gate % lru_w == 0 and (n_gate + 4 * lru_w) % wq == 0
    assert s % ATT_T == 0 and ATT_T % CHUNK == 0

    w_in_p = jnp.concatenate([w_in[..., branch_cols:], w_in[..., :branch_cols]], axis=-1).astype(BF16)
    xa_blk = n_gate // lru_w
    u_blk = xa_blk + 2
    q_blk = (n_gate + 4 * lru_w) // wq

    nd = -(-(MAX_DISTANCE // 2 + ATT_T) // ATT_T)
    bias = _bias_tiles(rel_bias, ATT_T, nd)
    bd = _block_diag(jnp.ones((wq // QK_DIM, QK_DIM, QK_DIM), BF16))
    w_pa_b, w_pb_b, w_pc_b, w_o_b = (w.astype(BF16) for w in (w_pa, w_pb, w_pc, w_o))
    wg_b, wu_b, wd_b = (w.astype(BF16) for w in (w_ff_gate, w_ff_up, w_ff_down))
    causal = jnp.tril(jnp.ones((SG_LEN, SG_LEN), bool))

    x2 = x.reshape(bsz * s, d)
    for l in range(depth):
        lam_init = 0.8 - 0.6 * math.exp(-0.3 * l)
        proj = _inproj(x2, ln1_g[l][None], w_in_p, l)
        proj3 = proj.reshape(bsz, s, -1)

        cvec = (-LRU_C * jax.nn.softplus(-lru_lambda[l]))[None]
        ya = _lru(proj3, xa_blk, conv_w[l], conv_b[l][None], _block_diag(lru_wa[l]).astype(BF16),
                  lru_ba[l][None], _block_diag(lru_wi[l]).astype(BF16), lru_bi[l][None], cvec)

        ws = jnp.where(causal, sg_w[l], 0.0).astype(BF16)
        ws_cat = jnp.transpose(ws, (1, 0, 2)).reshape(SG_LEN, groups * SG_LEN)
        bs_full = jnp.repeat(sg_b[l].T, sg_wd // groups, axis=1)
        yb = _sgu(proj, u_blk, sg_ln_g[l][None], sg_ln_b[l][None], ws_cat, bs_full, groups)

        gq = jnp.tile(q_norm_g[l] * (QK_DIM ** -0.5 * LOG2E), wq // QK_DIM)[None]
        gk = jnp.tile(k_norm_g[l], wq // QK_DIM)[None]
        qt, kn, vt = _prep(proj3, q_blk, gq, gk, bd, heads)
        lam4 = jnp.stack([lambda_q1[l], lambda_k1[l], lambda_q2[l], lambda_k2[l]])
        subg = jnp.broadcast_to(subln_g[l][:, None], (V_DIM, ATT_T))
        yc = _attn(qt, kn, vt, bias, lam4, subg, lam_init)

        x2 = _merge(x2, ya.reshape(bsz * s, -1), yb, yc.reshape(bsz * s, -1), proj,
                    b_gate[l].reshape(3, 1, d), w_pa_b, w_pb_b, w_pc_b, w_o_b, l)
        x2 = _ffn(x2, ln2_g[l][None], wg_b, wu_b, wd_b, l)
    return x2.reshape(bsz, s, d)
```

```python
import functools
import math

import jax
import jax.numpy as jnp
from jax import lax
from jax.experimental import pallas as pl
from jax.experimental.pallas import tpu as pltpu

F32 = jnp.float32
BF16 = jnp.bfloat16

EPS = 1e-6
CHUNK = 64
LRU_C = 8.0
SG_LEN = 128
QK_DIM = 64
V_DIM = 128
MAX_DISTANCE = 2048
NEG_BIG = -1e30
LOG2E = math.log2(math.e)

ATT_T = 512
VMEM_LIMIT = 56 * 1024 * 1024


def _cparams(*sem):
    return pltpu.CompilerParams(dimension_semantics=sem, vmem_limit_bytes=VMEM_LIMIT)


def _rms_norm_bf16(x, g):
    ms = jnp.mean(x * x, axis=-1, keepdims=True)
    return (x * lax.rsqrt(ms + EPS) * g).astype(BF16)


def _resident(shape, index_map):
    return pl.BlockSpec(shape, index_map, pipeline_mode=pl.Buffered(1))


def _inproj_body(x_ref, g_ref, w_ref, o_ref):
    h = _rms_norm_bf16(x_ref[...], g_ref[...])
    o_ref[...] = jnp.dot(h, w_ref[...], preferred_element_type=F32).astype(o_ref.dtype)


def _inproj(x2, g, w, l, tm=512):
    n, d = x2.shape
    c = w.shape[-1]
    return pl.pallas_call(
        _inproj_body,
        grid=(n // tm,),
        in_specs=[
            pl.BlockSpec((tm, d), lambda i: (i, 0)),
            pl.BlockSpec((1, d), lambda i: (0, 0)),
            _resident((None, d, c), lambda i: (l, 0, 0)),
        ],
        out_specs=pl.BlockSpec((tm, c), lambda i: (i, 0)),
        out_shape=jax.ShapeDtypeStruct((n, c), BF16),
        compiler_params=_cparams("parallel"),
        name="inproj",
    )(x2, g, w)


def _lru_body(xa_ref, ga_ref, cw_ref, cb_ref, wa_ref, ba_ref, wi_ref, bi_ref, c_ref, o_ref,
              ext_ref, h_ref, *, ts):
    w = xa_ref.shape[-1]

    @pl.when(pl.program_id(1) == 0)
    def _():
        ext_ref[0:8, :] = jnp.zeros((8, w), F32)
        h_ref[...] = jnp.zeros_like(h_ref)

    xa = xa_ref[...].astype(F32)
    ext_ref[8:8 + ts, :] = xa
    xc = (cb_ref[...] + ext_ref[5:5 + ts, :] * cw_ref[0:1, :] + ext_ref[6:6 + ts, :] * cw_ref[1:2, :]
          + ext_ref[7:7 + ts, :] * cw_ref[2:3, :] + xa * cw_ref[3:4, :])
    ext_ref[0:8, :] = ext_ref[ts:ts + 8, :]

    xcb = xc.astype(BF16)
    r = jax.nn.sigmoid(jnp.dot(xcb, wa_ref[...], preferred_element_type=F32) + ba_ref[...])
    i = jax.nn.sigmoid(jnp.dot(xcb, wi_ref[...], preferred_element_type=F32) + bi_ref[...])
    log_a = c_ref[...] * r
    a = jnp.exp(log_a)
    mult = jnp.sqrt(-jnp.tanh(log_a) * (a * a + 1.0))
    b = mult * (i * xc)

    row = lax.broadcasted_iota(jnp.int32, (ts, w), 0)
    d = 1
    while d < ts:
        keep = row >= d
        a_sh = jnp.where(keep, pltpu.roll(a, d, 0), 1.0)
        b_sh = jnp.where(keep, pltpu.roll(b, d, 0), 0.0)
        b = a * b_sh + b
        a = a * a_sh
        d *= 2
    h = a * h_ref[...] + b
    h_ref[...] = h[ts - 1:ts, :]
    o_ref[...] = (h * jax.nn.gelu(ga_ref[...].astype(F32))).astype(o_ref.dtype)


def _lru(proj3, xa_blk, cw, cb, wa, ba, wi, bi, cvec, ts=512):
    bsz, s, _ = proj3.shape
    w = cw.shape[-1]
    full = lambda shp: pl.BlockSpec(shp, lambda b, t: (0,) * len(shp))
    return pl.pallas_call(
        functools.partial(_lru_body, ts=ts),
        grid=(bsz, s // ts),
        in_specs=[
            pl.BlockSpec((None, ts, w), lambda b, t: (b, t, xa_blk)),
            pl.BlockSpec((None, ts, w), lambda b, t: (b, t, xa_blk + 1)),
            full((4, w)), full((1, w)), full((w, w)), full((1, w)), full((w, w)), full((1, w)),
            full((1, w)),
        ],
        out_specs=pl.BlockSpec((None, ts, w), lambda b, t: (b, t, 0)),
        out_shape=jax.ShapeDtypeStruct((bsz, s, w), BF16),
        scratch_shapes=[pltpu.VMEM((ts + 8, w), F32), pltpu.VMEM((1, w), F32)],
        compiler_params=_cparams("parallel", "arbitrary"),
        name="lru",
    )(proj3, proj3, cw, cb, wa, ba, wi, bi, cvec)


def _sgu_body(u_ref, v_ref, g_ref, b_ref, ws_ref, bs_ref, o_ref, *, tr, groups):
    w = v_ref.shape[-1]
    gd = w // groups
    v = v_ref[...].astype(F32)
    mu = jnp.mean(v, axis=-1, keepdims=True)
    vc = v - mu
    var = jnp.mean(vc * vc, axis=-1, keepdims=True)
    vn = (vc * lax.rsqrt(var + EPS) * g_ref[...] + b_ref[...]).astype(BF16)
    lane_grp = lax.broadcasted_iota(jnp.int32, (SG_LEN, w), 1) // gd
    ws = ws_ref[...]
    zero = jnp.zeros((SG_LEN, w), BF16)
    for blk in range(tr // SG_LEN):
        rows = slice(blk * SG_LEN, (blk + 1) * SG_LEN)
        vb = vn[rows]
        stacked = jnp.concatenate([jnp.where(lane_grp == g, vb, zero) for g in range(groups)], axis=0)
        mixed = jnp.dot(ws, stacked, preferred_element_type=F32) + bs_ref[...]
        o_ref[rows, :] = (u_ref[rows, :].astype(F32) * mixed).astype(o_ref.dtype)


def _sgu(proj, u_blk, ln_g, ln_b, ws_cat, bs_full, groups, tr=512):
    n, _ = proj.shape
    w = ln_g.shape[-1]
    full = lambda shp: pl.BlockSpec(shp, lambda i: (0,) * len(shp))
    return pl.pallas_call(
        functools.partial(_sgu_body, tr=tr, groups=groups),
        grid=(n // tr,),
        in_specs=[
            pl.BlockSpec((tr, w), lambda i: (i, u_blk)),
            pl.BlockSpec((tr, w), lambda i: (i, u_blk + 1)),
            full((1, w)), full((1, w)), full(ws_cat.shape), full(bs_full.shape),
        ],
        out_specs=pl.BlockSpec((tr, w), lambda i: (i, 0)),
        out_shape=jax.ShapeDtypeStruct((n, w), BF16),
        compiler_params=_cparams("parallel"),
        name="sgu",
    )(proj, proj, ln_g, ln_b, ws_cat, bs_full)


def _prep_body(q_ref, k_ref, v_ref, gq_ref, gk_ref, bd_ref, qt_ref, kn_ref, vt_ref, *, heads):
    def group_rms(y_ref, g_ref):
        y = y_ref[...].astype(F32)
        y2 = y * y
        hi = y2.astype(BF16)
        lo = (y2 - hi.astype(F32)).astype(BF16)
        ss = (jnp.dot(hi, bd_ref[...], preferred_element_type=F32)
              + jnp.dot(lo, bd_ref[...], preferred_element_type=F32))
        return y * lax.rsqrt(ss * (1.0 / QK_DIM) + EPS) * g_ref[...]

    qn = group_rms(q_ref, gq_ref)
    kn_ref[...] = group_rms(k_ref, gk_ref).astype(BF16)
    v = v_ref[...].astype(F32)
    for h in range(heads):
        cols = slice(h * V_DIM, (h + 1) * V_DIM)
        qt_ref[h] = qn[:, cols].T.astype(BF16)
        vt_ref[h] = v[:, cols].T.astype(BF16)


def _prep(proj3, q_blk, gq, gk, bd, heads, tm=512):
    bsz, s, _ = proj3.shape
    wq = heads * 2 * QK_DIM
    full = lambda shp: pl.BlockSpec(shp, lambda b, t: (0,) * len(shp))
    tspec = pl.BlockSpec((None, heads, V_DIM, tm), lambda b, t: (b, 0, 0, t))
    return pl.pallas_call(
        functools.partial(_prep_body, heads=heads),
        grid=(bsz, s // tm),
        in_specs=[
            pl.BlockSpec((None, tm, wq), lambda b, t: (b, t, q_blk)),
            pl.BlockSpec((None, tm, wq), lambda b, t: (b, t, q_blk + 1)),
            pl.BlockSpec((None, tm, wq), lambda b, t: (b, t, q_blk + 2)),
            full((1, wq)), full((1, wq)), full((wq, wq)),
        ],
        out_specs=[tspec, pl.BlockSpec((None, tm, wq), lambda b, t: (b, t, 0)), tspec],
        out_shape=[
            jax.ShapeDtypeStruct((bsz, heads, V_DIM, s), BF16),
            jax.ShapeDtypeStruct((bsz, s, wq), BF16),
            jax.ShapeDtypeStruct((bsz, heads, V_DIM, s), BF16),
        ],
        compiler_params=_cparams("parallel", "parallel"),
        name="qkv_prep",
    )(proj3, proj3, proj3, gq, gk, bd)


def _attn_body(qt_ref, k_ref, vt_ref, bias_ref, lam_ref, subg_ref, o_ref, acc_ref, s_ref, *, t, nd,
               lam_init):
    qi = pl.program_id(2)
    qt = qt_ref[...]
    row = lax.broadcasted_iota(jnp.int32, qt.shape, 0)
    zero = jnp.zeros_like(qt)
    q1 = jnp.where(row < QK_DIM, qt, zero)
    q2 = jnp.where(row >= QK_DIM, qt, zero)
    acc_ref[...] = jnp.zeros_like(acc_ref)

    def scores(kj):
        k = k_ref[pl.ds(pl.multiple_of(kj * t, t), t), :]
        return (jnp.dot(k, q1, preferred_element_type=F32),
                jnp.dot(k, q2, preferred_element_type=F32))

    def colmax(s):
        return jnp.max(s, axis=0, keepdims=True)

    def consume(idx, kj, s, mc, m, l):
        vt = vt_ref[:, pl.ds(pl.multiple_of(kj * t, t), t)]
        mn = jnp.maximum(m, mc)
        alpha = jnp.exp2(m - mn)
        p = jnp.exp2(s - mn)
        l = alpha * l + jnp.sum(p, axis=0, keepdims=True)
        acc_ref[idx] = acc_ref[idx] * alpha + jnp.dot(vt, p.astype(BF16), preferred_element_type=F32)
        return mn, l

    n_far = jnp.maximum(qi - (nd - 1), 0)
    last_far = jnp.maximum(n_far - 1, 0)

    def produce(slot, kj, bias_idx=None):
        s1, s2 = scores(kj)
        if bias_idx is not None:
            b = bias_ref[bias_idx]
            s1 = s1 + b
            s2 = s2 + b
        s_ref[slot, 0] = s1
        s_ref[slot, 1] = s2
        return colmax(s1), colmax(s2)

    def consume_slot(slot, kj, c, st):
        m1, l1 = consume(0, kj, s_ref[slot, 0], c[0], st[0], st[1])
        m2, l2 = consume(1, kj, s_ref[slot, 1], c[1], st[2], st[3])
        return m1, l1, m2, l2

    neg = jnp.full((1, t), NEG_BIG, F32)
    zer = jnp.zeros((1, t), F32)
    st = (neg, zer, neg, zer)

    pending = None
    for d in range(nd - 1, -1, -1):
        slot = (d + 1) % 2
        kj = jnp.maximum(qi - d, 0)
        c = produce(slot, kj, jnp.where(qi >= d, d, nd))
        if pending is not None:
            st = consume_slot(*pending, st)
        pending = (slot, kj, c)

    def produce_far(slot, kj):
        return produce(slot, jnp.minimum(kj, last_far))

    def far_pair(i, carry):
        st, c0 = carry
        c1 = produce_far(1, 2 * i + 1)
        st = consume_slot(0, 2 * i, c0, st)
        c0 = produce_far(0, 2 * i + 2)
        st = consume_slot(1, 2 * i + 1, c1, st)
        return st, c0

    def far_quad(i, carry):
        return far_pair(2 * i + 1, far_pair(2 * i, carry))

    c0 = produce_far(0, 0)
    st = consume_slot(*pending, st)
    carry = lax.fori_loop(0, n_far // 4, far_quad, (st, c0))
    st, c0 = lax.cond(n_far % 4 >= 2, lambda: far_pair(n_far // 4 * 2, carry), lambda: carry)
    _, l1, _, l2 = lax.cond(n_far % 2 == 1, lambda: consume_slot(0, last_far, c0, st), lambda: st)

    lam4 = lam_ref[...]
    lam = (jnp.exp(jnp.sum(lam4[0:1] * lam4[1:2], axis=-1, keepdims=True))
           - jnp.exp(jnp.sum(lam4[2:3] * lam4[3:4], axis=-1, keepdims=True)) + lam_init)
    o = acc_ref[0] / l1 - lam * (acc_ref[1] / l2)
    ms = jnp.mean(o * o, axis=0, keepdims=True)
    o = o * lax.rsqrt(ms + EPS) * subg_ref[...] * (1.0 - lam_init)
    o_ref[...] = o.T.astype(o_ref.dtype)


def _attn(qt, kn, vt, bias, lam4, subg, lam_init, t=ATT_T):
    bsz, heads, _, s = qt.shape
    nd = bias.shape[1] - 1
    return pl.pallas_call(
        functools.partial(_attn_body, t=t, nd=nd, lam_init=lam_init),
        grid=(bsz, heads, s // t),
        in_specs=[
            pl.BlockSpec((None, None, V_DIM, t), lambda b, h, i: (b, h, 0, i)),
            pl.BlockSpec((None, s, V_DIM), lambda b, h, i: (b, 0, h)),
            pl.BlockSpec((None, None, V_DIM, s), lambda b, h, i: (b, h, 0, 0)),
            pl.BlockSpec((None, nd + 1, t, t), lambda b, h, i: (h, 0, 0, 0)),
            pl.BlockSpec(lam4.shape, lambda b, h, i: (0, 0)),
            pl.BlockSpec(subg.shape, lambda b, h, i: (0, 0)),
        ],
        out_specs=pl.BlockSpec((None, t, V_DIM), lambda b, h, i: (b, i, h)),
        out_shape=jax.ShapeDtypeStruct((bsz, s, heads * V_DIM), BF16),
        scratch_shapes=[pltpu.VMEM((2, V_DIM, t), F32), pltpu.VMEM((2, 2, t, t), F32)],
        compiler_params=_cparams("parallel", "parallel", "arbitrary"),
        name="diff_attn",
    )(qt, kn, vt, bias, lam4, subg)


def _t5_bucket(rel, n_buckets):
    half = n_buckets // 2
    max_exact = half // 2
    ret = jnp.where(rel > 0, half, 0)
    n = jnp.abs(rel)
    nf = jnp.maximum(n, 1).astype(F32)
    large = max_exact + (jnp.log(nf / max_exact) / math.log(MAX_DISTANCE / max_exact)
                         * (half - max_exact)).astype(jnp.int32)
    large = jnp.minimum(large, half - 1)
    return ret + jnp.where(n < max_exact, n, large)


def _bias_body(f_ref, o_ref, *, t, nd):
    x = jnp.broadcast_to(f_ref[...], (t, 2 * t))
    y = pltpu.roll(x, t + 1, 1, stride=1, stride_axis=0)[:, :t]
    j = lax.broadcasted_iota(jnp.int32, (t, t), 0)
    i = lax.broadcasted_iota(jnp.int32, (t, t), 1)
    delta = pl.program_id(1)
    allowed = (((j // CHUNK) <= (i // CHUNK)) | (delta > 0)) & (delta < nd)
    o_ref[...] = jnp.where(allowed, y, NEG_BIG)


def _bias_tiles(rel_bias, t, nd):
    n_buckets, heads = rel_bias.shape
    c = jnp.arange(2 * t, dtype=jnp.int32)[None, :]
    delta = jnp.arange(nd + 1, dtype=jnp.int32)[:, None]
    rel = (t - 1 - c) - delta * t
    f = (rel_bias[_t5_bucket(rel, n_buckets)] - rel_bias[n_buckets // 2 - 1]) * LOG2E
    f = jnp.transpose(f, (2, 0, 1))[:, :, None, :]
    return pl.pallas_call(
        functools.partial(_bias_body, t=t, nd=nd),
        grid=(heads, nd + 1),
        in_specs=[pl.BlockSpec((None, None, 1, 2 * t), lambda h, d: (h, d, 0, 0))],
        out_specs=pl.BlockSpec((None, None, t, t), lambda h, d: (h, d, 0, 0)),
        out_shape=jax.ShapeDtypeStruct((heads, nd + 1, t, t), F32),
        compiler_params=_cparams("parallel", "parallel"),
        name="bias_tiles",
    )(f)


def _merge_body(x_ref, ya_ref, yb_ref, yc_ref, g0_ref, g1_ref, g2_ref, bg_ref, wpa_ref, wpb_ref,
                wpc_ref, wo_ref, o_ref):
    def branch(idx, y_ref, w_ref, g_ref):
        p = jnp.dot(y_ref[...], w_ref[...], preferred_element_type=F32)
        return jax.nn.sigmoid(g_ref[...].astype(F32) + bg_ref[idx]) * p

    merged = (branch(0, ya_ref, wpa_ref, g0_ref) + branch(1, yb_ref, wpb_ref, g1_ref)
              + branch(2, yc_ref, wpc_ref, g2_ref))
    o_ref[...] = x_ref[...] + jnp.dot(merged.astype(BF16), wo_ref[...], preferred_element_type=F32)


def _merge(x2, ya, yb, yc, proj, bg, wpa, wpb, wpc, wo, l, tm=512):
    n, d = x2.shape
    rows = lambda a: pl.BlockSpec((tm, a.shape[-1]), lambda i: (i, 0))
    wspec = lambda a: pl.BlockSpec((None,) + a.shape[1:], lambda i: (l, 0, 0))
    gate = lambda g: pl.BlockSpec((tm, d), lambda i: (i, g))
    return pl.pallas_call(
        _merge_body,
        grid=(n // tm,),
        in_specs=[rows(x2), rows(ya), rows(yb), rows(yc), gate(0), gate(1), gate(2),
                  pl.BlockSpec(bg.shape, lambda i: (0, 0, 0)),
                  wspec(wpa), wspec(wpb), wspec(wpc), wspec(wo)],
        out_specs=pl.BlockSpec((tm, d), lambda i: (i, 0)),
        out_shape=jax.ShapeDtypeStruct((n, d), F32),
        compiler_params=_cparams("parallel"),
        name="merge",
    )(x2, ya, yb, yc, proj, proj, proj, bg, wpa, wpb, wpc, wo)


def _ffn_body(x_ref, g_ref, wg_ref, wu_ref, wd_ref, o_ref):
    x = x_ref[...]
    h = _rms_norm_bf16(x, g_ref[...])
    gate = jnp.dot(h, wg_ref[...], preferred_element_type=F32)
    up = jnp.dot(h, wu_ref[...], preferred_element_type=F32)
    act = (jax.nn.silu(gate) * up).astype(BF16)
    o_ref[...] = x + jnp.dot(act, wd_ref[...], preferred_element_type=F32)


def _ffn(x2, g, wg, wu, wd, l, tm=512):
    n, d = x2.shape
    hid = wg.shape[-1]
    return pl.pallas_call(
        _ffn_body,
        grid=(n // tm,),
        in_specs=[
            pl.BlockSpec((tm, d), lambda i: (i, 0)),
            pl.BlockSpec((1, d), lambda i: (0, 0)),
            _resident((None, d, hid), lambda i: (l, 0, 0)),
            _resident((None, d, hid), lambda i: (l, 0, 0)),
            _resident((None, hid, d), lambda i: (l, 0, 0)),
        ],
        out_specs=pl.BlockSpec((tm, d), lambda i: (i, 0)),
        out_shape=jax.ShapeDtypeStruct((n, d), F32),
        compiler_params=_cparams("parallel"),
        name="ffn",
    )(x2, g, wg, wu, wd)


def _block_diag(w):
    heads, a, b = w.shape
    eye = jnp.eye(heads, dtype=w.dtype)
    return (eye[:, None, :, None] * w[:, :, None, :]).reshape(heads * a, heads * b)


def kernel(x, ln1_g, w_in, b_gate, conv_w, conv_b, lru_wa, lru_ba, lru_wi, lru_bi, lru_lambda, sg_ln_g, sg_ln_b, sg_w, sg_b, q_norm_g, k_norm_g, lambda_q1, lambda_k1, lambda_q2, lambda_k2, subln_g, rel_bias, w_pa, w_pb, w_pc, w_o, ln2_g, w_ff_gate, w_ff_up, w_ff_down):
    bsz, s, d = x.shape
    depth = w_in.shape[0]
    lru_w = conv_w.shape[-1]
    sg_wd = sg_ln_g.shape[-1]
    groups = sg_w.shape[1]
    heads = rel_bias.shape[1]
    wq = heads * 2 * QK_DIM
    n_gate = 3 * d
    branch_cols = w_in.shape[-1] - n_gate
    assert branch_cols == 2 * lru_w + 2 * sg_wd + 3 * wq
    assert lru_w == sg_wd and n_gate % lru_w == 0 and (n_gate + 4 * lru_w) % wq == 0
    assert s % ATT_T == 0 and ATT_T % CHUNK == 0

    w_in_p = jnp.concatenate([w_in[..., branch_cols:], w_in[..., :branch_cols]], axis=-1).astype(BF16)
    xa_blk = n_gate // lru_w
    u_blk = xa_blk + 2
    q_blk = (n_gate + 4 * lru_w) // wq

    nd = -(-(MAX_DISTANCE // 2 + ATT_T) // ATT_T)
    bias = _bias_tiles(rel_bias, ATT_T, nd)
    bd = _block_diag(jnp.ones((wq // QK_DIM, QK_DIM, QK_DIM), BF16))
    w_pa_b, w_pb_b, w_pc_b, w_o_b = (w.astype(BF16) for w in (w_pa, w_pb, w_pc, w_o))
    wg_b, wu_b, wd_b = (w.astype(BF16) for w in (w_ff_gate, w_ff_up, w_ff_down))
    causal = jnp.tril(jnp.ones((SG_LEN, SG_LEN), bool))

    x2 = x.reshape(bsz * s, d)
    for l in range(depth):
        lam_init = 0.8 - 0.6 * math.exp(-0.3 * l)
        proj = _inproj(x2, ln1_g[l][None], w_in_p, l)
        proj3 = proj.reshape(bsz, s, -1)

        cvec = (-LRU_C * jax.nn.softplus(-lru_lambda[l]))[None]
        ya = _lru(proj3, xa_blk, conv_w[l], conv_b[l][None], _block_diag(lru_wa[l]).astype(BF16),
                  lru_ba[l][None], _block_diag(lru_wi[l]).astype(BF16), lru_bi[l][None], cvec)

        ws = jnp.where(causal, sg_w[l], 0.0).astype(BF16)
        ws_cat = jnp.transpose(ws, (1, 0, 2)).reshape(SG_LEN, groups * SG_LEN)
        bs_full = jnp.repeat(sg_b[l].T, sg_wd // groups, axis=1)
        yb = _sgu(proj, u_blk, sg_ln_g[l][None], sg_ln_b[l][None], ws_cat, bs_full, groups)

        gq = jnp.tile(q_norm_g[l] * (QK_DIM ** -0.5 * LOG2E), wq // QK_DIM)[None]
        gk = jnp.tile(k_norm_g[l], wq // QK_DIM)[None]
        qt, kn, vt = _prep(proj3, q_blk, gq, gk, bd, heads)
        lam4 = jnp.stack([lambda_q1[l], lambda_k1[l], lambda_q2[l], lambda_k2[l]])
        subg = jnp.broadcast_to(subln_g[l][:, None], (V_DIM, ATT_T))
        yc = _attn(qt, kn, vt, bias, lam4, subg, lam_init)

        x2 = _merge(x2, ya.reshape(bsz * s, -1), yb, yc.reshape(bsz * s, -1), proj,
                    b_gate[l].reshape(3, 1, d), w_pa_b, w_pb_b, w_pc_b, w_o_b, l)
        x2 = _ffn(x2, ln2_g[l][None], wg_b, wu_b, wd_b, l)
    return x2.reshape(bsz, s, d)
```

```python
import functools
import math

import jax
import jax.numpy as jnp
from jax import lax
from jax.experimental import pallas as pl
from jax.experimental.pallas import tpu as pltpu

F32 = jnp.float32
BF16 = jnp.bfloat16

EPS = 1e-6
CHUNK = 64
LRU_C = 8.0
SG_LEN = 128
QK_DIM = 64
V_DIM = 128
V_AUG = V_DIM + 16
MAX_DISTANCE = 2048
NEG_BIG = -1e30
LOG2E = math.log2(math.e)

ATT_T = 512
VMEM_LIMIT = 56 * 1024 * 1024


def _cparams(*sem):
    return pltpu.CompilerParams(dimension_semantics=sem, vmem_limit_bytes=VMEM_LIMIT)


def _rms_norm_bf16(x, g):
    ms = jnp.mean(x * x, axis=-1, keepdims=True)
    return (x * lax.rsqrt(ms + EPS) * g).astype(BF16)


def _resident(shape, index_map):
    return pl.BlockSpec(shape, index_map, pipeline_mode=pl.Buffered(1))


def _inproj_body(x_ref, g_ref, w_ref, o_ref):
    h = _rms_norm_bf16(x_ref[...], g_ref[...])
    o_ref[...] = jnp.dot(h, w_ref[...], preferred_element_type=F32).astype(o_ref.dtype)


def _inproj(x2, g, w, l, tm=512):
    n, d = x2.shape
    c = w.shape[-1]
    return pl.pallas_call(
        _inproj_body,
        grid=(n // tm,),
        in_specs=[
            pl.BlockSpec((tm, d), lambda i: (i, 0)),
            pl.BlockSpec((1, d), lambda i: (0, 0)),
            _resident((None, d, c), lambda i: (l, 0, 0)),
        ],
        out_specs=pl.BlockSpec((tm, c), lambda i: (i, 0)),
        out_shape=jax.ShapeDtypeStruct((n, c), BF16),
        compiler_params=_cparams("parallel"),
        name="inproj",
    )(x2, g, w)


def _lru_body(xa_ref, ga_ref, cw_ref, cb_ref, wa_ref, ba_ref, wi_ref, bi_ref, c_ref, o_ref,
              ext_ref, h_ref, *, ts):
    w = xa_ref.shape[-1]

    @pl.when(pl.program_id(1) == 0)
    def _():
        ext_ref[0:8, :] = jnp.zeros((8, w), F32)
        h_ref[...] = jnp.zeros_like(h_ref)

    xa = xa_ref[...].astype(F32)
    ext_ref[8:8 + ts, :] = xa
    xc = (cb_ref[...] + ext_ref[5:5 + ts, :] * cw_ref[0:1, :] + ext_ref[6:6 + ts, :] * cw_ref[1:2, :]
          + ext_ref[7:7 + ts, :] * cw_ref[2:3, :] + xa * cw_ref[3:4, :])
    ext_ref[0:8, :] = ext_ref[ts:ts + 8, :]

    xcb = xc.astype(BF16)
    r = jax.nn.sigmoid(jnp.dot(xcb, wa_ref[...], preferred_element_type=F32) + ba_ref[...])
    i = jax.nn.sigmoid(jnp.dot(xcb, wi_ref[...], preferred_element_type=F32) + bi_ref[...])
    log_a = c_ref[...] * r
    a = jnp.exp(log_a)
    mult = jnp.sqrt(-jnp.tanh(log_a) * (a * a + 1.0))
    b = mult * (i * xc)

    row = lax.broadcasted_iota(jnp.int32, (ts, w), 0)
    d = 1
    while d < ts:
        keep = row >= d
        a_sh = jnp.where(keep, pltpu.roll(a, d, 0), 1.0)
        b_sh = jnp.where(keep, pltpu.roll(b, d, 0), 0.0)
        b = a * b_sh + b
        a = a * a_sh
        d *= 2
    h = a * h_ref[...] + b
    h_ref[...] = h[ts - 1:ts, :]
    o_ref[...] = (h * jax.nn.gelu(ga_ref[...].astype(F32))).astype(o_ref.dtype)


def _lru(proj3, xa_blk, cw, cb, wa, ba, wi, bi, cvec, ts=512):
    bsz, s, _ = proj3.shape
    w = cw.shape[-1]
    full = lambda shp: pl.BlockSpec(shp, lambda b, t: (0,) * len(shp))
    return pl.pallas_call(
        functools.partial(_lru_body, ts=ts),
        grid=(bsz, s // ts),
        in_specs=[
            pl.BlockSpec((None, ts, w), lambda b, t: (b, t, xa_blk)),
            pl.BlockSpec((None, ts, w), lambda b, t: (b, t, xa_blk + 1)),
            full((4, w)), full((1, w)), full((w, w)), full((1, w)), full((w, w)), full((1, w)),
            full((1, w)),
        ],
        out_specs=pl.BlockSpec((None, ts, w), lambda b, t: (b, t, 0)),
        out_shape=jax.ShapeDtypeStruct((bsz, s, w), BF16),
        scratch_shapes=[pltpu.VMEM((ts + 8, w), F32), pltpu.VMEM((1, w), F32)],
        compiler_params=_cparams("parallel", "arbitrary"),
        name="lru",
    )(proj3, proj3, cw, cb, wa, ba, wi, bi, cvec)


def _sgu_body(u_ref, v_ref, g_ref, b_ref, ws_ref, bs_ref, o_ref, *, tr, groups):
    w = v_ref.shape[-1]
    gd = w // groups
    v = v_ref[...].astype(F32)
    mu = jnp.mean(v, axis=-1, keepdims=True)
    vc = v - mu
    var = jnp.mean(vc * vc, axis=-1, keepdims=True)
    vn = (vc * lax.rsqrt(var + EPS) * g_ref[...] + b_ref[...]).astype(BF16)
    lane_grp = lax.broadcasted_iota(jnp.int32, (SG_LEN, w), 1) // gd
    ws = ws_ref[...]
    zero = jnp.zeros((SG_LEN, w), BF16)
    for blk in range(tr // SG_LEN):
        rows = slice(blk * SG_LEN, (blk + 1) * SG_LEN)
        vb = vn[rows]
        stacked = jnp.concatenate([jnp.where(lane_grp == g, vb, zero) for g in range(groups)], axis=0)
        mixed = jnp.dot(ws, stacked, preferred_element_type=F32) + bs_ref[...]
        o_ref[rows, :] = (u_ref[rows, :].astype(F32) * mixed).astype(o_ref.dtype)


def _sgu(proj, u_blk, ln_g, ln_b, ws_cat, bs_full, groups, tr=512):
    n, _ = proj.shape
    w = ln_g.shape[-1]
    full = lambda shp: pl.BlockSpec(shp, lambda i: (0,) * len(shp))
    return pl.pallas_call(
        functools.partial(_sgu_body, tr=tr, groups=groups),
        grid=(n // tr,),
        in_specs=[
            pl.BlockSpec((tr, w), lambda i: (i, u_blk)),
            pl.BlockSpec((tr, w), lambda i: (i, u_blk + 1)),
            full((1, w)), full((1, w)), full(ws_cat.shape), full(bs_full.shape),
        ],
        out_specs=pl.BlockSpec((tr, w), lambda i: (i, 0)),
        out_shape=jax.ShapeDtypeStruct((n, w), BF16),
        compiler_params=_cparams("parallel"),
        name="sgu",
    )(proj, proj, ln_g, ln_b, ws_cat, bs_full)


def _prep_body(q_ref, k_ref, v_ref, gq_ref, gk_ref, bd_ref, qt_ref, kn_ref, vt_ref, *, heads):
    def group_rms(y_ref, g_ref):
        y = y_ref[...].astype(F32)
        y2 = y * y
        hi = y2.astype(BF16)
        lo = (y2 - hi.astype(F32)).astype(BF16)
        ss = (jnp.dot(hi, bd_ref[...], preferred_element_type=F32)
              + jnp.dot(lo, bd_ref[...], preferred_element_type=F32))
        return y * lax.rsqrt(ss * (1.0 / QK_DIM) + EPS) * g_ref[...]

    qn = group_rms(q_ref, gq_ref)
    kn_ref[...] = group_rms(k_ref, gk_ref).astype(BF16)
    v = v_ref[...].astype(F32)
    for h in range(heads):
        cols = slice(h * V_DIM, (h + 1) * V_DIM)
        qt_ref[h] = qn[:, cols].T.astype(BF16)
        vt_ref[h, :V_DIM, :] = v[:, cols].T.astype(BF16)
        vt_ref[h, V_DIM:, :] = jnp.ones((V_AUG - V_DIM, v.shape[0]), BF16)


def _prep(proj3, q_blk, gq, gk, bd, heads, tm=512):
    bsz, s, _ = proj3.shape
    wq = heads * 2 * QK_DIM
    full = lambda shp: pl.BlockSpec(shp, lambda b, t: (0,) * len(shp))
    qspec = pl.BlockSpec((None, heads, V_DIM, tm), lambda b, t: (b, 0, 0, t))
    vspec = pl.BlockSpec((None, heads, V_AUG, tm), lambda b, t: (b, 0, 0, t))
    return pl.pallas_call(
        functools.partial(_prep_body, heads=heads),
        grid=(bsz, s // tm),
        in_specs=[
            pl.BlockSpec((None, tm, wq), lambda b, t: (b, t, q_blk)),
            pl.BlockSpec((None, tm, wq), lambda b, t: (b, t, q_blk + 1)),
            pl.BlockSpec((None, tm, wq), lambda b, t: (b, t, q_blk + 2)),
            full((1, wq)), full((1, wq)), full((wq, wq)),
        ],
        out_specs=[qspec, pl.BlockSpec((None, tm, wq), lambda b, t: (b, t, 0)), vspec],
        out_shape=[
            jax.ShapeDtypeStruct((bsz, heads, V_DIM, s), BF16),
            jax.ShapeDtypeStruct((bsz, s, wq), BF16),
            jax.ShapeDtypeStruct((bsz, heads, V_AUG, s), BF16),
        ],
        compiler_params=_cparams("parallel", "parallel"),
        name="qkv_prep",
    )(proj3, proj3, proj3, gq, gk, bd)


def _attn_body(qt_ref, k_ref, vt_ref, bias_ref, lam_ref, subg_ref, o_ref, acc_ref, s_ref, *, t, nd,
               lam_init):
    qi = pl.program_id(2)
    qt = qt_ref[...]
    row = lax.broadcasted_iota(jnp.int32, qt.shape, 0)
    zero = jnp.zeros_like(qt)
    q1 = jnp.where(row < QK_DIM, qt, zero)
    q2 = jnp.where(row >= QK_DIM, qt, zero)
    acc_ref[...] = jnp.zeros_like(acc_ref)

    def scores(kj):
        k = k_ref[pl.ds(pl.multiple_of(kj * t, t), t), :]
        return (jnp.dot(k, q1, preferred_element_type=F32),
                jnp.dot(k, q2, preferred_element_type=F32))

    def colmax(s):
        return jnp.max(s, axis=0, keepdims=True)

    def consume(idx, kj, s, mc, m):
        vt = vt_ref[:, pl.ds(pl.multiple_of(kj * t, t), t)]
        mn = jnp.maximum(m, mc)
        p = jnp.exp2(s - mn).astype(BF16)
        acc_ref[idx] = acc_ref[idx] * jnp.exp2(m - mn) + jnp.dot(vt, p, preferred_element_type=F32)
        return mn

    n_far = jnp.maximum(qi - (nd - 1), 0)
    last_far = jnp.maximum(n_far - 1, 0)

    def produce(slot, kj, bias_idx=None):
        s1, s2 = scores(kj)
        if bias_idx is not None:
            b = bias_ref[bias_idx]
            s1 = s1 + b
            s2 = s2 + b
        s_ref[slot, 0] = s1
        s_ref[slot, 1] = s2
        return colmax(s1), colmax(s2)

    def consume_slot(slot, kj, c, st):
        return (consume(0, kj, s_ref[slot, 0], c[0], st[0]),
                consume(1, kj, s_ref[slot, 1], c[1], st[1]))

    neg = jnp.full((1, t), NEG_BIG, F32)
    st = (neg, neg)

    pending = None
    for d in range(nd - 1, -1, -1):
        slot = (d + 1) % 2
        kj = jnp.maximum(qi - d, 0)
        c = produce(slot, kj, jnp.where(qi >= d, d, nd))
        if pending is not None:
            st = consume_slot(*pending, st)
        pending = (slot, kj, c)

    def produce_far(slot, kj):
        return produce(slot, jnp.minimum(kj, last_far))

    def far_pair(i, carry):
        st, c0 = carry
        c1 = produce_far(1, 2 * i + 1)
        st = consume_slot(0, 2 * i, c0, st)
        c0 = produce_far(0, 2 * i + 2)
        st = consume_slot(1, 2 * i + 1, c1, st)
        return st, c0

    def far_quad(i, carry):
        return far_pair(2 * i + 1, far_pair(2 * i, carry))

    c0 = produce_far(0, 0)
    st = consume_slot(*pending, st)
    carry = lax.fori_loop(0, n_far // 4, far_quad, (st, c0))
    st, c0 = lax.cond(n_far % 4 >= 2, lambda: far_pair(n_far // 4 * 2, carry), lambda: carry)

    @pl.when(n_far % 2 == 1)
    def _():
        consume_slot(0, last_far, c0, st)

    lam4 = lam_ref[...]
    lam = (jnp.exp(jnp.sum(lam4[0:1] * lam4[1:2], axis=-1, keepdims=True))
           - jnp.exp(jnp.sum(lam4[2:3] * lam4[3:4], axis=-1, keepdims=True)) + lam_init)
    o = (acc_ref[0, :V_DIM, :] / acc_ref[0, V_DIM:V_DIM + 1, :]
         - lam * (acc_ref[1, :V_DIM, :] / acc_ref[1, V_DIM:V_DIM + 1, :]))
    ms = jnp.mean(o * o, axis=0, keepdims=True)
    o = o * lax.rsqrt(ms + EPS) * subg_ref[...] * (1.0 - lam_init)
    o_ref[...] = o.T.astype(o_ref.dtype)


def _attn(qt, kn, vt, bias, lam4, subg, lam_init, t=ATT_T):
    bsz, heads, _, s = qt.shape
    nd = bias.shape[1] - 1
    return pl.pallas_call(
        functools.partial(_attn_body, t=t, nd=nd, lam_init=lam_init),
        grid=(bsz, heads, s // t),
        in_specs=[
            pl.BlockSpec((None, None, V_DIM, t), lambda b, h, i: (b, h, 0, i)),
            pl.BlockSpec((None, s, V_DIM), lambda b, h, i: (b, 0, h)),
            pl.BlockSpec((None, None, V_AUG, s), lambda b, h, i: (b, h, 0, 0)),
            pl.BlockSpec((None, nd + 1, t, t), lambda b, h, i: (h, 0, 0, 0)),
            pl.BlockSpec(lam4.shape, lambda b, h, i: (0, 0)),
            pl.BlockSpec(subg.shape, lambda b, h, i: (0, 0)),
        ],
        out_specs=pl.BlockSpec((None, t, V_DIM), lambda b, h, i: (b, i, h)),
        out_shape=jax.ShapeDtypeStruct((bsz, s, heads * V_DIM), BF16),
        scratch_shapes=[pltpu.VMEM((2, V_AUG, t), F32), pltpu.VMEM((2, 2, t, t), F32)],
        compiler_params=_cparams("parallel", "parallel", "arbitrary"),
        name="diff_attn",
    )(qt, kn, vt, bias, lam4, subg)


def _t5_bucket(rel, n_buckets):
    half = n_buckets // 2
    max_exact = half // 2
    ret = jnp.where(rel > 0, half, 0)
    n = jnp.abs(rel)
    nf = jnp.maximum(n, 1).astype(F32)
    large = max_exact + (jnp.log(nf / max_exact) / math.log(MAX_DISTANCE / max_exact)
                         * (half - max_exact)).astype(jnp.int32)
    large = jnp.minimum(large, half - 1)
    return ret + jnp.where(n < max_exact, n, large)


def _bias_body(f_ref, o_ref, *, t, nd):
    x = jnp.broadcast_to(f_ref[...], (t, 2 * t))
    y = pltpu.roll(x, t + 1, 1, stride=1, stride_axis=0)[:, :t]
    j = lax.broadcasted_iota(jnp.int32, (t, t), 0)
    i = lax.broadcasted_iota(jnp.int32, (t, t), 1)
    delta = pl.program_id(1)
    allowed = (((j // CHUNK) <= (i // CHUNK)) | (delta > 0)) & (delta < nd)
    o_ref[...] = jnp.where(allowed, y, NEG_BIG)


def _bias_tiles(rel_bias, t, nd):
    n_buckets, heads = rel_bias.shape
    c = jnp.arange(2 * t, dtype=jnp.int32)[None, :]
    delta = jnp.arange(nd + 1, dtype=jnp.int32)[:, None]
    rel = (t - 1 - c) - delta * t
    f = (rel_bias[_t5_bucket(rel, n_buckets)] - rel_bias[n_buckets // 2 - 1]) * LOG2E
    f = jnp.transpose(f, (2, 0, 1))[:, :, None, :]
    return pl.pallas_call(
        functools.partial(_bias_body, t=t, nd=nd),
        grid=(heads, nd + 1),
        in_specs=[pl.BlockSpec((None, None, 1, 2 * t), lambda h, d: (h, d, 0, 0))],
        out_specs=pl.BlockSpec((None, None, t, t), lambda h, d: (h, d, 0, 0)),
        out_shape=jax.ShapeDtypeStruct((heads, nd + 1, t, t), F32),
        compiler_params=_cparams("parallel", "parallel"),
        name="bias_tiles",
    )(f)


def _merge_body(x_ref, ya_ref, yb_ref, yc_ref, *rest, parts):
    gate_refs, (bg_ref, wpa_ref, wpb_ref, wpc_ref, wo_ref, o_ref) = rest[:3 * parts], rest[3 * parts:]

    def branch(idx, y_ref, w_ref):
        p = jnp.dot(y_ref[...], w_ref[...], preferred_element_type=F32)
        g = jnp.concatenate([r[...] for r in gate_refs[idx * parts:(idx + 1) * parts]], axis=-1)
        return jax.nn.sigmoid(g.astype(F32) + bg_ref[idx]) * p

    merged = branch(0, ya_ref, wpa_ref) + branch(1, yb_ref, wpb_ref) + branch(2, yc_ref, wpc_ref)
    o_ref[...] = x_ref[...] + jnp.dot(merged.astype(BF16), wo_ref[...], preferred_element_type=F32)


def _merge(x2, ya, yb, yc, proj, gate_col, bg, wpa, wpb, wpc, wo, l, tm=512):
    n, d = x2.shape
    gw = math.gcd(gate_col, d)
    parts = d // gw
    rows = lambda a: pl.BlockSpec((tm, a.shape[-1]), lambda i: (i, 0))
    wspec = lambda a: pl.BlockSpec((None,) + a.shape[1:], lambda i: (l, 0, 0))
    gate = lambda blk: pl.BlockSpec((tm, gw), lambda i: (i, gate_col // gw + blk))
    return pl.pallas_call(
        functools.partial(_merge_body, parts=parts),
        grid=(n // tm,),
        in_specs=[rows(x2), rows(ya), rows(yb), rows(yc)] + [gate(b) for b in range(3 * parts)]
                 + [pl.BlockSpec(bg.shape, lambda i: (0, 0, 0)),
                    wspec(wpa), wspec(wpb), wspec(wpc), wspec(wo)],
        out_specs=pl.BlockSpec((tm, d), lambda i: (i, 0)),
        out_shape=jax.ShapeDtypeStruct((n, d), F32),
        compiler_params=_cparams("parallel"),
        name="merge",
    )(x2, ya, yb, yc, *([proj] * (3 * parts)), bg, wpa, wpb, wpc, wo)


def _ffn_body(x_ref, g_ref, wg_ref, wu_ref, wd_ref, o_ref):
    x = x_ref[...]
    h = _rms_norm_bf16(x, g_ref[...])
    gate = jnp.dot(h, wg_ref[...], preferred_element_type=F32)
    up = jnp.dot(h, wu_ref[...], preferred_element_type=F32)
    act = (jax.nn.silu(gate) * up).astype(BF16)
    o_ref[...] = x + jnp.dot(act, wd_ref[...], preferred_element_type=F32)


def _ffn(x2, g, wg, wu, wd, l, tm=512):
    n, d = x2.shape
    hid = wg.shape[-1]
    return pl.pallas_call(
        _ffn_body,
        grid=(n // tm,),
        in_specs=[
            pl.BlockSpec((tm, d), lambda i: (i, 0)),
            pl.BlockSpec((1, d), lambda i: (0, 0)),
            _resident((None, d, hid), lambda i: (l, 0, 0)),
            _resident((None, d, hid), lambda i: (l, 0, 0)),
            _resident((None, hid, d), lambda i: (l, 0, 0)),
        ],
        out_specs=pl.BlockSpec((tm, d), lambda i: (i, 0)),
        out_shape=jax.ShapeDtypeStruct((n, d), F32),
        compiler_params=_cparams("parallel"),
        name="ffn",
    )(x2, g, wg, wu, wd)


def _block_diag(w):
    heads, a, b = w.shape
    eye = jnp.eye(heads, dtype=w.dtype)
    return (eye[:, None, :, None] * w[:, :, None, :]).reshape(heads * a, heads * b)


def kernel(x, ln1_g, w_in, b_gate, conv_w, conv_b, lru_wa, lru_ba, lru_wi, lru_bi, lru_lambda, sg_ln_g, sg_ln_b, sg_w, sg_b, q_norm_g, k_norm_g, lambda_q1, lambda_k1, lambda_q2, lambda_k2, subln_g, rel_bias, w_pa, w_pb, w_pc, w_o, ln2_g, w_ff_gate, w_ff_up, w_ff_down):
    bsz, s, d = x.shape
    depth = w_in.shape[0]
    lru_w = conv_w.shape[-1]
    sg_wd = sg_ln_g.shape[-1]
    groups = sg_w.shape[1]
    heads = rel_bias.shape[1]
    wq = heads * 2 * QK_DIM
    n_gate = 3 * d
    branch_cols = w_in.shape[-1] - n_gate
    assert branch_cols == 2 * lru_w + 2 * sg_wd + 3 * wq
    assert lru_w == sg_wd and (4 * lru_w) % wq == 0
    assert s % ATT_T == 0 and ATT_T % CHUNK == 0

    w_in_b = w_in.astype(BF16)
    xa_blk, u_blk = 0, 2
    q_blk = 4 * lru_w // wq

    nd = -(-(MAX_DISTANCE // 2 + ATT_T) // ATT_T)
    bias = _bias_tiles(rel_bias, ATT_T, nd)
    bd = _block_diag(jnp.ones((wq // QK_DIM, QK_DIM, QK_DIM), BF16))
    w_pa_b, w_pb_b, w_pc_b, w_o_b = (w.astype(BF16) for w in (w_pa, w_pb, w_pc, w_o))
    wg_b, wu_b, wd_b = (w.astype(BF16) for w in (w_ff_gate, w_ff_up, w_ff_down))
    causal = jnp.tril(jnp.ones((SG_LEN, SG_LEN), bool))

    x2 = x.reshape(bsz * s, d)
    for l in range(depth):
        lam_init = 0.8 - 0.6 * math.exp(-0.3 * l)
        proj = _inproj(x2, ln1_g[l][None], w_in_b, l)
        proj3 = proj.reshape(bsz, s, -1)

        cvec = (-LRU_C * jax.nn.softplus(-lru_lambda[l]))[None]
        ya = _lru(proj3, xa_blk, conv_w[l], conv_b[l][None], _block_diag(lru_wa[l]).astype(BF16),
                  lru_ba[l][None], _block_diag(lru_wi[l]).astype(BF16), lru_bi[l][None], cvec)

        ws = jnp.where(causal, sg_w[l], 0.0).astype(BF16)
        ws_cat = jnp.transpose(ws, (1, 0, 2)).reshape(SG_LEN, groups * SG_LEN)
        bs_full = jnp.repeat(sg_b[l].T, sg_wd // groups, axis=1)
        yb = _sgu(proj, u_blk, sg_ln_g[l][None], sg_ln_b[l][None], ws_cat, bs_full, groups)

        gq = jnp.tile(q_norm_g[l] * (QK_DIM ** -0.5 * LOG2E), wq // QK_DIM)[None]
        gk = jnp.tile(k_norm_g[l], wq // QK_DIM)[None]
        qt, kn, vt = _prep(proj3, q_blk, gq, gk, bd, heads)
        lam4 = jnp.stack([lambda_q1[l], lambda_k1[l], lambda_q2[l], lambda_k2[l]])
        subg = jnp.broadcast_to(subln_g[l][:, None], (V_DIM, ATT_T))
        yc = _attn(qt, kn, vt, bias, lam4, subg, lam_init)

        x2 = _merge(x2, ya.reshape(bsz * s, -1), yb, yc.reshape(bsz * s, -1), proj, branch_cols,
                    b_gate[l].reshape(3, 1, d), w_pa_b, w_pb_b, w_pc_b, w_o_b, l)
        x2 = _ffn(x2, ln2_g[l][None], wg_b, wu_b, wd_b, l)
    return x2.reshape(bsz, s, d)
```

```python
import functools
import math

import jax
import jax.numpy as jnp
from jax import lax
from jax.experimental import pallas as pl
from jax.experimental.pallas import tpu as pltpu

F32 = jnp.float32
BF16 = jnp.bfloat16

EPS = 1e-6
CHUNK = 64
LRU_C = 8.0
SG_LEN = 128
QK_DIM = 64
V_DIM = 128
V_AUG = V_DIM + 16
MAX_DISTANCE = 2048
NEG_BIG = -1e30
LOG2E = math.log2(math.e)

ATT_T = 512
VMEM_LIMIT = 56 * 1024 * 1024


def _cparams(*sem):
    return pltpu.CompilerParams(dimension_semantics=sem, vmem_limit_bytes=VMEM_LIMIT)


def _rms_norm_bf16(x, g):
    ms = jnp.mean(x * x, axis=-1, keepdims=True)
    return (x * lax.rsqrt(ms + EPS) * g).astype(BF16)


def _resident(shape, index_map):
    return pl.BlockSpec(shape, index_map, pipeline_mode=pl.Buffered(1))


def _lru_branch(xa, ga, cw_ref, cb_ref, wa_ref, ba_ref, wi_ref, bi_ref, c_ref, ext_ref, h_ref):
    ts, w = xa.shape
    ext_ref[8:8 + ts, :] = xa
    xc = (cb_ref[...] + ext_ref[5:5 + ts, :] * cw_ref[0:1, :] + ext_ref[6:6 + ts, :] * cw_ref[1:2, :]
          + ext_ref[7:7 + ts, :] * cw_ref[2:3, :] + xa * cw_ref[3:4, :])
    ext_ref[0:8, :] = ext_ref[ts:ts + 8, :]

    xcb = xc.astype(BF16)
    r = jax.nn.sigmoid(jnp.dot(xcb, wa_ref[...], preferred_element_type=F32) + ba_ref[...])
    i = jax.nn.sigmoid(jnp.dot(xcb, wi_ref[...], preferred_element_type=F32) + bi_ref[...])
    log_a = c_ref[...] * r
    a = jnp.exp(log_a)
    mult = jnp.sqrt(-jnp.tanh(log_a) * (a * a + 1.0))
    b = mult * (i * xc)

    sub = 8
    a3 = a.reshape(ts // sub, sub, w)
    b3 = b.reshape(ts // sub, sub, w)
    row = lax.broadcasted_iota(jnp.int32, a3.shape, 1)
    d = 1
    while d < sub:
        keep = row >= d
        a_sh = jnp.where(keep, pltpu.roll(a3, d, 1), 1.0)
        b_sh = jnp.where(keep, pltpu.roll(b3, d, 1), 0.0)
        b3 = a3 * b_sh + b3
        a3 = a3 * a_sh
        d *= 2
    carry = h_ref[...]
    groups = []
    for g in range(ts // sub):
        hg = a3[g] * carry + b3[g]
        groups.append(hg)
        carry = hg[sub - 1:sub, :]
    h_ref[...] = carry
    h = jnp.concatenate(groups, axis=0)
    return h * jax.nn.gelu(ga)


def _sgu_branch(u, v, g_ref, b_ref, ws_ref, bs_ref, groups):
    tr, w = v.shape
    gd = w // groups
    mu = jnp.mean(v, axis=-1, keepdims=True)
    vc = v - mu
    var = jnp.mean(vc * vc, axis=-1, keepdims=True)
    vn = (vc * lax.rsqrt(var + EPS) * g_ref[...] + b_ref[...]).astype(BF16)
    lane_grp = lax.broadcasted_iota(jnp.int32, (SG_LEN, w), 1) // gd
    ws = ws_ref[...]
    zero = jnp.zeros((SG_LEN, w), BF16)
    out = []
    for blk in range(tr // SG_LEN):
        vb = vn[blk * SG_LEN:(blk + 1) * SG_LEN]
        stacked = jnp.concatenate([jnp.where(lane_grp == g, vb, zero) for g in range(groups)], axis=0)
        out.append(jnp.dot(ws, stacked, preferred_element_type=F32) + bs_ref[...])
    return u * jnp.concatenate(out, axis=0)


def _group_rms(y, g_ref, bd_ref):
    ss = jnp.dot((y * y).astype(BF16), bd_ref[...], preferred_element_type=F32)
    return y * lax.rsqrt(ss * (1.0 / QK_DIM) + EPS) * g_ref[...]


def _front_body(x_ref, g1_ref, w_ref, cw_ref, cb_ref, wa_ref, ba_ref, wi_ref, bi_ref, c_ref,
                lng_ref, lnb_ref, ws_ref, bs_ref, gq_ref, gk_ref, bd_ref,
                ya_ref, yb_ref, qt_ref, kn_ref, vt_ref, gates_ref, ext_ref, h_ref, *,
                heads, groups, lru_w, wq):
    @pl.when(pl.program_id(1) == 0)
    def _():
        ext_ref[0:8, :] = jnp.zeros((8, lru_w), F32)
        h_ref[...] = jnp.zeros_like(h_ref)

    h = _rms_norm_bf16(x_ref[...], g1_ref[...])

    def proj(c0, c1):
        return jnp.dot(h, w_ref[:, c0:c1], preferred_element_type=F32)

    c0 = 4 * lru_w
    pa = proj(0, 2 * lru_w)
    ya_ref[...] = _lru_branch(pa[:, :lru_w], pa[:, lru_w:], cw_ref, cb_ref, wa_ref, ba_ref, wi_ref,
                              bi_ref, c_ref, ext_ref, h_ref).astype(ya_ref.dtype)
    pb = proj(2 * lru_w, 4 * lru_w)
    yb_ref[...] = _sgu_branch(pb[:, :lru_w], pb[:, lru_w:], lng_ref, lnb_ref, ws_ref, bs_ref,
                              groups).astype(yb_ref.dtype)
    qn = _group_rms(proj(c0, c0 + wq), gq_ref, bd_ref)
    kn_ref[...] = _group_rms(proj(c0 + wq, c0 + 2 * wq), gk_ref, bd_ref).astype(BF16)
    v = proj(c0 + 2 * wq, c0 + 3 * wq)
    for hd in range(heads):
        cols = slice(hd * V_DIM, (hd + 1) * V_DIM)
        qt_ref[hd] = qn[:, cols].T.astype(BF16)
        vt_ref[hd, :V_DIM, :] = v[:, cols].T.astype(BF16)
        vt_ref[hd, V_DIM:, :] = jnp.ones((V_AUG - V_DIM, v.shape[0]), BF16)
    gates_ref[...] = proj(c0 + 3 * wq, w_ref.shape[-1]).astype(gates_ref.dtype)


def _front(x2, bsz, g1, w, l, lru_params, sgu_params, qk_params, heads, groups, tm=512):
    n, d = x2.shape
    s = n // bsz
    nt = s // tm
    c = w.shape[-1]
    lru_w = lru_params[0].shape[-1]
    wq = heads * 2 * QK_DIM
    n_gate = c - 4 * lru_w - 3 * wq
    small = list(lru_params) + list(sgu_params) + list(qk_params)
    full = lambda a: pl.BlockSpec(a.shape, lambda b, t: (0,) * a.ndim)
    rows = lambda width: pl.BlockSpec((tm, width), lambda b, t: (b * nt + t, 0))
    tspec = lambda r: pl.BlockSpec((None, heads, r, tm), lambda b, t: (b, 0, 0, t))
    return pl.pallas_call(
        functools.partial(_front_body, heads=heads, groups=groups, lru_w=lru_w, wq=wq),
        grid=(bsz, nt),
        in_specs=[rows(d), pl.BlockSpec((1, d), lambda b, t: (0, 0)),
                  _resident((None, d, c), lambda b, t: (l, 0, 0))] + [full(a) for a in small],
        out_specs=[rows(lru_w), rows(lru_w), tspec(V_DIM),
                   pl.BlockSpec((None, tm, wq), lambda b, t: (b, t, 0)), tspec(V_AUG), rows(n_gate)],
        out_shape=[
            jax.ShapeDtypeStruct((n, lru_w), BF16),
            jax.ShapeDtypeStruct((n, lru_w), BF16),
            jax.ShapeDtypeStruct((bsz, heads, V_DIM, s), BF16),
            jax.ShapeDtypeStruct((bsz, s, wq), BF16),
            jax.ShapeDtypeStruct((bsz, heads, V_AUG, s), BF16),
            jax.ShapeDtypeStruct((n, n_gate), BF16),
        ],
        scratch_shapes=[pltpu.VMEM((tm + 8, lru_w), F32), pltpu.VMEM((1, lru_w), F32)],
        compiler_params=_cparams("parallel", "arbitrary"),
        name="front",
    )(x2, g1, w, *small)


def _attn_body(qt_ref, k_ref, vt_ref, bias_ref, lam_ref, subg_ref, o_ref, acc_ref, s_ref, *, t, nd,
               lam_init):
    qi = pl.program_id(2)
    qt = qt_ref[...]
    row = lax.broadcasted_iota(jnp.int32, qt.shape, 0)
    zero = jnp.zeros_like(qt)
    q1 = jnp.where(row < QK_DIM, qt, zero)
    q2 = jnp.where(row >= QK_DIM, qt, zero)
    acc_ref[...] = jnp.zeros_like(acc_ref)

    def scores(kj):
        k = k_ref[pl.ds(pl.multiple_of(kj * t, t), t), :]
        return (jnp.dot(k, q1, preferred_element_type=F32),
                jnp.dot(k, q2, preferred_element_type=F32))

    def colmax(s):
        return jnp.max(s, axis=0, keepdims=True)

    def consume(idx, kj, s, mc, m):
        vt = vt_ref[:, pl.ds(pl.multiple_of(kj * t, t), t)]
        mn = jnp.maximum(m, mc)
        p = jnp.exp2(s - mn).astype(BF16)
        acc_ref[idx] = acc_ref[idx] * jnp.exp2(m - mn) + jnp.dot(vt, p, preferred_element_type=F32)
        return mn

    n_far = jnp.maximum(qi - (nd - 1), 0)
    last_far = jnp.maximum(n_far - 1, 0)

    def produce(slot, kj, bias_idx=None):
        s1, s2 = scores(kj)
        if bias_idx is not None:
            b = bias_ref[bias_idx]
            s1 = s1 + b
            s2 = s2 + b
        s_ref[slot, 0] = s1
        s_ref[slot, 1] = s2
        return colmax(s1), colmax(s2)

    def consume_slot(slot, kj, c, st):
        return (consume(0, kj, s_ref[slot, 0], c[0], st[0]),
                consume(1, kj, s_ref[slot, 1], c[1], st[1]))

    neg = jnp.full((1, t), NEG_BIG, F32)
    st = (neg, neg)

    pending = None
    for d in range(nd - 1, -1, -1):
        slot = (d + 1) % 2
        kj = jnp.maximum(qi - d, 0)
        c = produce(slot, kj, jnp.where(qi >= d, d, nd))
        if pending is not None:
            st = consume_slot(*pending, st)
        pending = (slot, kj, c)

    def produce_far(slot, kj):
        return produce(slot, jnp.minimum(kj, last_far))

    def far_pair(i, carry):
        st, c0 = carry
        c1 = produce_far(1, 2 * i + 1)
        st = consume_slot(0, 2 * i, c0, st)
        c0 = produce_far(0, 2 * i + 2)
        st = consume_slot(1, 2 * i + 1, c1, st)
        return st, c0

    def far_quad(i, carry):
        return far_pair(2 * i + 1, far_pair(2 * i, carry))

    c0 = produce_far(0, 0)
    st = consume_slot(*pending, st)
    carry = lax.fori_loop(0, n_far // 4, far_quad, (st, c0))
    st, c0 = lax.cond(n_far % 4 >= 2, lambda: far_pair(n_far // 4 * 2, carry), lambda: carry)

    @pl.when(n_far % 2 == 1)
    def _():
        consume_slot(0, last_far, c0, st)

    lam4 = lam_ref[...]
    lam = (jnp.exp(jnp.sum(lam4[0:1] * lam4[1:2], axis=-1, keepdims=True))
           - jnp.exp(jnp.sum(lam4[2:3] * lam4[3:4], axis=-1, keepdims=True)) + lam_init)
    o = (acc_ref[0, :V_DIM, :] / acc_ref[0, V_DIM:V_DIM + 1, :]
         - lam * (acc_ref[1, :V_DIM, :] / acc_ref[1, V_DIM:V_DIM + 1, :]))
    ms = jnp.mean(o * o, axis=0, keepdims=True)
    o = o * lax.rsqrt(ms + EPS) * subg_ref[...] * (1.0 - lam_init)
    o_ref[...] = o.T.astype(o_ref.dtype)


def _attn(qt, kn, vt, bias, lam4, subg, lam_init, t=ATT_T):
    bsz, heads, _, s = qt.shape
    nd = bias.shape[1] - 1
    return pl.pallas_call(
        functools.partial(_attn_body, t=t, nd=nd, lam_init=lam_init),
        grid=(bsz, heads, s // t),
        in_specs=[
            pl.BlockSpec((None, None, V_DIM, t), lambda b, h, i: (b, h, 0, i)),
            pl.BlockSpec((None, s, V_DIM), lambda b, h, i: (b, 0, h)),
            pl.BlockSpec((None, None, V_AUG, s), lambda b, h, i: (b, h, 0, 0)),
            pl.BlockSpec((None, nd + 1, t, t), lambda b, h, i: (h, 0, 0, 0)),
            pl.BlockSpec(lam4.shape, lambda b, h, i: (0, 0)),
            pl.BlockSpec(subg.shape, lambda b, h, i: (0, 0)),
        ],
        out_specs=pl.BlockSpec((None, t, V_DIM), lambda b, h, i: (b, i, h)),
        out_shape=jax.ShapeDtypeStruct((bsz, s, heads * V_DIM), BF16),
        scratch_shapes=[pltpu.VMEM((2, V_AUG, t), F32), pltpu.VMEM((2, 2, t, t), F32)],
        compiler_params=_cparams("parallel", "parallel", "arbitrary"),
        name="diff_attn",
    )(qt, kn, vt, bias, lam4, subg)


def _t5_bucket(rel, n_buckets):
    half = n_buckets // 2
    max_exact = half // 2
    ret = jnp.where(rel > 0, half, 0)
    n = jnp.abs(rel)
    nf = jnp.maximum(n, 1).astype(F32)
    large = max_exact + (jnp.log(nf / max_exact) / math.log(MAX_DISTANCE / max_exact)
                         * (half - max_exact)).astype(jnp.int32)
    large = jnp.minimum(large, half - 1)
    return ret + jnp.where(n < max_exact, n, large)


def _bias_body(f_ref, o_ref, *, t, nd):
    x = jnp.broadcast_to(f_ref[...], (t, 2 * t))
    y = pltpu.roll(x, t + 1, 1, stride=1, stride_axis=0)[:, :t]
    j = lax.broadcasted_iota(jnp.int32, (t, t), 0)
    i = lax.broadcasted_iota(jnp.int32, (t, t), 1)
    delta = pl.program_id(1)
    allowed = (((j // CHUNK) <= (i // CHUNK)) | (delta > 0)) & (delta < nd)
    o_ref[...] = jnp.where(allowed, y, NEG_BIG)


def _bias_tiles(rel_bias, t, nd):
    n_buckets, heads = rel_bias.shape
    c = jnp.arange(2 * t, dtype=jnp.int32)[None, :]
    delta = jnp.arange(nd + 1, dtype=jnp.int32)[:, None]
    rel = (t - 1 - c) - delta * t
    f = (rel_bias[_t5_bucket(rel, n_buckets)] - rel_bias[n_buckets // 2 - 1]) * LOG2E
    f = jnp.transpose(f, (2, 0, 1))[:, :, None, :]
    return pl.pallas_call(
        functools.partial(_bias_body, t=t, nd=nd),
        grid=(heads, nd + 1),
        in_specs=[pl.BlockSpec((None, None, 1, 2 * t), lambda h, d: (h, d, 0, 0))],
        out_specs=pl.BlockSpec((None, None, t, t), lambda h, d: (h, d, 0, 0)),
        out_shape=jax.ShapeDtypeStruct((heads, nd + 1, t, t), F32),
        compiler_params=_cparams("parallel", "parallel"),
        name="bias_tiles",
    )(f)


def _merge_body(x_ref, ya_ref, yb_ref, yc_ref, g0_ref, g1_ref, g2_ref, bg_ref, wpa_ref, wpb_ref,
                wpc_ref, wo_ref, o_ref):
    def branch(idx, y_ref, w_ref, g_ref):
        p = jnp.dot(y_ref[...], w_ref[...], preferred_element_type=F32)
        return jax.nn.sigmoid(g_ref[...].astype(F32) + bg_ref[idx]) * p

    merged = (branch(0, ya_ref, wpa_ref, g0_ref) + branch(1, yb_ref, wpb_ref, g1_ref)
              + branch(2, yc_ref, wpc_ref, g2_ref))
    o_ref[...] = x_ref[...] + jnp.dot(merged.astype(BF16), wo_ref[...], preferred_element_type=F32)


def _merge(x2, ya, yb, yc, gates, bg, wpa, wpb, wpc, wo, l, tm=512):
    n, d = x2.shape
    rows = lambda a: pl.BlockSpec((tm, a.shape[-1]), lambda i: (i, 0))
    wspec = lambda a: pl.BlockSpec((None,) + a.shape[1:], lambda i: (l, 0, 0))
    gate = lambda g: pl.BlockSpec((tm, d), lambda i: (i, g))
    return pl.pallas_call(
        _merge_body,
        grid=(n // tm,),
        in_specs=[rows(x2), rows(ya), rows(yb), rows(yc), gate(0), gate(1), gate(2),
                  pl.BlockSpec(bg.shape, lambda i: (0, 0, 0)),
                  wspec(wpa), wspec(wpb), wspec(wpc), wspec(wo)],
        out_specs=pl.BlockSpec((tm, d), lambda i: (i, 0)),
        out_shape=jax.ShapeDtypeStruct((n, d), F32),
        compiler_params=_cparams("parallel"),
        name="merge",
    )(x2, ya, yb, yc, gates, gates, gates, bg, wpa, wpb, wpc, wo)


def _ffn_body(x_ref, g_ref, wg_ref, wu_ref, wd_ref, o_ref):
    x = x_ref[...]
    h = _rms_norm_bf16(x, g_ref[...])
    gate = jnp.dot(h, wg_ref[...], preferred_element_type=F32)
    up = jnp.dot(h, wu_ref[...], preferred_element_type=F32)
    act = (jax.nn.silu(gate) * up).astype(BF16)
    o_ref[...] = x + jnp.dot(act, wd_ref[...], preferred_element_type=F32)


def _ffn(x2, g, wg, wu, wd, l, tm=512):
    n, d = x2.shape
    hid = wg.shape[-1]
    return pl.pallas_call(
        _ffn_body,
        grid=(n // tm,),
        in_specs=[
            pl.BlockSpec((tm, d), lambda i: (i, 0)),
            pl.BlockSpec((1, d), lambda i: (0, 0)),
            _resident((None, d, hid), lambda i: (l, 0, 0)),
            _resident((None, d, hid), lambda i: (l, 0, 0)),
            _resident((None, hid, d), lambda i: (l, 0, 0)),
        ],
        out_specs=pl.BlockSpec((tm, d), lambda i: (i, 0)),
        out_shape=jax.ShapeDtypeStruct((n, d), F32),
        compiler_params=_cparams("parallel"),
        name="ffn",
    )(x2, g, wg, wu, wd)


def _block_diag(w):
    heads, a, b = w.shape
    eye = jnp.eye(heads, dtype=w.dtype)
    return (eye[:, None, :, None] * w[:, :, None, :]).reshape(heads * a, heads * b)


def kernel(x, ln1_g, w_in, b_gate, conv_w, conv_b, lru_wa, lru_ba, lru_wi, lru_bi, lru_lambda, sg_ln_g, sg_ln_b, sg_w, sg_b, q_norm_g, k_norm_g, lambda_q1, lambda_k1, lambda_q2, lambda_k2, subln_g, rel_bias, w_pa, w_pb, w_pc, w_o, ln2_g, w_ff_gate, w_ff_up, w_ff_down):
    bsz, s, d = x.shape
    depth = w_in.shape[0]
    lru_w = conv_w.shape[-1]
    sg_wd = sg_ln_g.shape[-1]
    groups = sg_w.shape[1]
    heads = rel_bias.shape[1]
    wq = heads * 2 * QK_DIM
    assert w_in.shape[-1] == 2 * lru_w + 2 * sg_wd + 3 * wq + 3 * d and lru_w == sg_wd
    assert s % ATT_T == 0 and ATT_T % CHUNK == 0

    nd = -(-(MAX_DISTANCE // 2 + ATT_T) // ATT_T)
    bias = _bias_tiles(rel_bias, ATT_T, nd)
    bd = _block_diag(jnp.ones((wq // QK_DIM, QK_DIM, QK_DIM), BF16))
    w_in_b, w_pa_b, w_pb_b, w_pc_b, w_o_b = (w.astype(BF16) for w in (w_in, w_pa, w_pb, w_pc, w_o))
    wg_b, wu_b, wd_b = (w.astype(BF16) for w in (w_ff_gate, w_ff_up, w_ff_down))
    causal = jnp.tril(jnp.ones((SG_LEN, SG_LEN), bool))

    x2 = x.reshape(bsz * s, d)
    for l in range(depth):
        lam_init = 0.8 - 0.6 * math.exp(-0.3 * l)
        cvec = (-LRU_C * jax.nn.softplus(-lru_lambda[l]))[None]
        lru_params = (conv_w[l], conv_b[l][None], _block_diag(lru_wa[l]).astype(BF16), lru_ba[l][None],
                      _block_diag(lru_wi[l]).astype(BF16), lru_bi[l][None], cvec)
        ws = jnp.where(causal, sg_w[l], 0.0).astype(BF16)
        ws_cat = jnp.transpose(ws, (1, 0, 2)).reshape(SG_LEN, groups * SG_LEN)
        bs_full = jnp.repeat(sg_b[l].T, sg_wd // groups, axis=1)
        sgu_params = (sg_ln_g[l][None], sg_ln_b[l][None], ws_cat, bs_full)
        gq = jnp.tile(q_norm_g[l] * (QK_DIM ** -0.5 * LOG2E), wq // QK_DIM)[None]
        gk = jnp.tile(k_norm_g[l], wq // QK_DIM)[None]
        ya, yb, qt, kn, vt, gates = _front(x2, bsz, ln1_g[l][None], w_in_b, l, lru_params, sgu_params,
                                           (gq, gk, bd), heads, groups)
        lam4 = jnp.stack([lambda_q1[l], lambda_k1[l], lambda_q2[l], lambda_k2[l]])
        subg = jnp.broadcast_to(subln_g[l][:, None], (V_DIM, ATT_T))
        yc = _attn(qt, kn, vt, bias, lam4, subg, lam_init)

        x2 = _merge(x2, ya, yb, yc.reshape(bsz * s, -1), gates, b_gate[l].reshape(3, 1, d),
                    w_pa_b, w_pb_b, w_pc_b, w_o_b, l)
        x2 = _ffn(x2, ln2_g[l][None], wg_b, wu_b, wd_b, l)
    return x2.reshape(bsz, s, d)
```

```python
import functools
import math

import jax
import jax.numpy as jnp
from jax import lax
from jax.experimental import pallas as pl
from jax.experimental.pallas import tpu as pltpu

F32 = jnp.float32
BF16 = jnp.bfloat16

EPS = 1e-6
CHUNK = 64
LRU_C = 8.0
SG_LEN = 128
QK_DIM = 64
V_DIM = 128
V_AUG = V_DIM + 16
MAX_DISTANCE = 2048
NEG_BIG = -1e30
LOG2E = math.log2(math.e)

ATT_T = 512
ATT_TILES = 2
VMEM_LIMIT = 56 * 1024 * 1024


def _cparams(*sem):
    return pltpu.CompilerParams(dimension_semantics=sem, vmem_limit_bytes=VMEM_LIMIT)


def _rms_norm_bf16(x, g):
    ms = jnp.mean(x * x, axis=-1, keepdims=True)
    return (x * lax.rsqrt(ms + EPS) * g).astype(BF16)


def _resident(shape, index_map):
    return pl.BlockSpec(shape, index_map, pipeline_mode=pl.Buffered(1))


def _lru_branch(xa, ga, cw_ref, cb_ref, wa_ref, ba_ref, wi_ref, bi_ref, c_ref, ext_ref, h_ref):
    ts, w = xa.shape
    ext_ref[8:8 + ts, :] = xa
    xc = (cb_ref[...] + ext_ref[5:5 + ts, :] * cw_ref[0:1, :] + ext_ref[6:6 + ts, :] * cw_ref[1:2, :]
          + ext_ref[7:7 + ts, :] * cw_ref[2:3, :] + xa * cw_ref[3:4, :])
    ext_ref[0:8, :] = ext_ref[ts:ts + 8, :]

    xcb = xc.astype(BF16)
    r = jax.nn.sigmoid(jnp.dot(xcb, wa_ref[...], preferred_element_type=F32) + ba_ref[...])
    i = jax.nn.sigmoid(jnp.dot(xcb, wi_ref[...], preferred_element_type=F32) + bi_ref[...])
    log_a = c_ref[...] * r
    a = jnp.exp(log_a)
    mult = jnp.sqrt(-jnp.tanh(log_a) * (a * a + 1.0))
    b = mult * (i * xc)

    sub = 8
    a3 = a.reshape(ts // sub, sub, w)
    b3 = b.reshape(ts // sub, sub, w)
    row = lax.broadcasted_iota(jnp.int32, a3.shape, 1)
    d = 1
    while d < sub:
        keep = row >= d
        a_sh = jnp.where(keep, pltpu.roll(a3, d, 1), 1.0)
        b_sh = jnp.where(keep, pltpu.roll(b3, d, 1), 0.0)
        b3 = a3 * b_sh + b3
        a3 = a3 * a_sh
        d *= 2
    carry = h_ref[...]
    groups = []
    for g in range(ts // sub):
        hg = a3[g] * carry + b3[g]
        groups.append(hg)
        carry = hg[sub - 1:sub, :]
    h_ref[...] = carry
    h = jnp.concatenate(groups, axis=0)
    return h * jax.nn.gelu(ga)


def _sgu_branch(u, v, g_ref, b_ref, ws_ref, bs_ref, groups):
    tr, w = v.shape
    gd = w // groups
    mu = jnp.mean(v, axis=-1, keepdims=True)
    vc = v - mu
    var = jnp.mean(vc * vc, axis=-1, keepdims=True)
    vn = (vc * lax.rsqrt(var + EPS) * g_ref[...] + b_ref[...]).astype(BF16)
    lane_grp = lax.broadcasted_iota(jnp.int32, (SG_LEN, w), 1) // gd
    ws = ws_ref[...]
    zero = jnp.zeros((SG_LEN, w), BF16)
    out = []
    for blk in range(tr // SG_LEN):
        vb = vn[blk * SG_LEN:(blk + 1) * SG_LEN]
        stacked = jnp.concatenate([jnp.where(lane_grp == g, vb, zero) for g in range(groups)], axis=0)
        out.append(jnp.dot(ws, stacked, preferred_element_type=F32) + bs_ref[...])
    return u * jnp.concatenate(out, axis=0)


def _group_rms(y, g_ref, bd_ref):
    ss = jnp.dot((y * y).astype(BF16), bd_ref[...], preferred_element_type=F32)
    return y * lax.rsqrt(ss * (1.0 / QK_DIM) + EPS) * g_ref[...]


def _front_body(x_ref, g1_ref, w_ref, cw_ref, cb_ref, wa_ref, ba_ref, wi_ref, bi_ref, c_ref,
                lng_ref, lnb_ref, ws_ref, bs_ref, gq_ref, gk_ref, bd_ref,
                ya_ref, yb_ref, qt_ref, kn_ref, vt_ref, gates_ref, ext_ref, h_ref, *,
                heads, groups, lru_w, wq):
    @pl.when(pl.program_id(1) == 0)
    def _():
        ext_ref[0:8, :] = jnp.zeros((8, lru_w), F32)
        h_ref[...] = jnp.zeros_like(h_ref)

    h = _rms_norm_bf16(x_ref[...], g1_ref[...])

    def proj(c0, c1):
        return jnp.dot(h, w_ref[:, c0:c1], preferred_element_type=F32)

    c0 = 4 * lru_w
    pa = proj(0, 2 * lru_w)
    ya_ref[...] = _lru_branch(pa[:, :lru_w], pa[:, lru_w:], cw_ref, cb_ref, wa_ref, ba_ref, wi_ref,
                              bi_ref, c_ref, ext_ref, h_ref).astype(ya_ref.dtype)
    pb = proj(2 * lru_w, 4 * lru_w)
    yb_ref[...] = _sgu_branch(pb[:, :lru_w], pb[:, lru_w:], lng_ref, lnb_ref, ws_ref, bs_ref,
                              groups).astype(yb_ref.dtype)
    qn = _group_rms(proj(c0, c0 + wq), gq_ref, bd_ref)
    kn_ref[...] = _group_rms(proj(c0 + wq, c0 + 2 * wq), gk_ref, bd_ref).astype(BF16)
    v = proj(c0 + 2 * wq, c0 + 3 * wq)
    for hd in range(heads):
        cols = slice(hd * V_DIM, (hd + 1) * V_DIM)
        qt_ref[hd] = qn[:, cols].T.astype(BF16)
        vt_ref[hd, :V_DIM, :] = v[:, cols].T.astype(BF16)
        vt_ref[hd, V_DIM:, :] = jnp.ones((V_AUG - V_DIM, v.shape[0]), BF16)
    gates_ref[...] = proj(c0 + 3 * wq, w_ref.shape[-1]).astype(gates_ref.dtype)


def _front(x2, bsz, g1, w, l, lru_params, sgu_params, qk_params, heads, groups, tm=512):
    n, d = x2.shape
    s = n // bsz
    nt = s // tm
    c = w.shape[-1]
    lru_w = lru_params[0].shape[-1]
    wq = heads * 2 * QK_DIM
    n_gate = c - 4 * lru_w - 3 * wq
    small = list(lru_params) + list(sgu_params) + list(qk_params)
    full = lambda a: pl.BlockSpec(a.shape, lambda b, t: (0,) * a.ndim)
    rows = lambda width: pl.BlockSpec((tm, width), lambda b, t: (b * nt + t, 0))
    tspec = lambda r: pl.BlockSpec((None, heads, r, tm), lambda b, t: (b, 0, 0, t))
    return pl.pallas_call(
        functools.partial(_front_body, heads=heads, groups=groups, lru_w=lru_w, wq=wq),
        grid=(bsz, nt),
        in_specs=[rows(d), pl.BlockSpec((1, d), lambda b, t: (0, 0)),
                  _resident((None, d, c), lambda b, t: (l, 0, 0))] + [full(a) for a in small],
        out_specs=[rows(lru_w), rows(lru_w), tspec(V_DIM),
                   pl.BlockSpec((None, tm, wq), lambda b, t: (b, t, 0)), tspec(V_AUG), rows(n_gate)],
        out_shape=[
            jax.ShapeDtypeStruct((n, lru_w), BF16),
            jax.ShapeDtypeStruct((n, lru_w), BF16),
            jax.ShapeDtypeStruct((bsz, heads, V_DIM, s), BF16),
            jax.ShapeDtypeStruct((bsz, s, wq), BF16),
            jax.ShapeDtypeStruct((bsz, heads, V_AUG, s), BF16),
            jax.ShapeDtypeStruct((n, n_gate), BF16),
        ],
        scratch_shapes=[pltpu.VMEM((tm + 8, lru_w), F32), pltpu.VMEM((1, lru_w), F32)],
        compiler_params=_cparams("parallel", "arbitrary"),
        name="front",
    )(x2, g1, w, *small)


def _attn_body(qt_ref, k_ref, vt_ref, bias_ref, lam_ref, subg_ref, o_ref, acc_ref, s_ref, *, t, nd,
               lam_init, tiles):
    def one_tile(j, carry):
        cols = pl.ds(pl.multiple_of(j * t, t), t)
        _attn_tile(pl.program_id(2) * tiles + j, qt_ref[:, cols], k_ref, vt_ref, bias_ref, lam_ref,
                   subg_ref, o_ref.at[cols], acc_ref, s_ref, t=t, nd=nd, lam_init=lam_init)
        return carry

    lax.fori_loop(0, tiles, one_tile, 0)


def _attn_tile(qi, qt, k_ref, vt_ref, bias_ref, lam_ref, subg_ref, o_ref, acc_ref, s_ref, *, t, nd,
               lam_init):
    row = lax.broadcasted_iota(jnp.int32, qt.shape, 0)
    zero = jnp.zeros_like(qt)
    q1 = jnp.where(row < QK_DIM, qt, zero)
    q2 = jnp.where(row >= QK_DIM, qt, zero)
    acc_ref[...] = jnp.zeros_like(acc_ref)

    def scores(kj):
        k = k_ref[pl.ds(pl.multiple_of(kj * t, t), t), :]
        return (jnp.dot(k, q1, preferred_element_type=F32),
                jnp.dot(k, q2, preferred_element_type=F32))

    def colmax(s):
        return jnp.max(s, axis=0, keepdims=True)

    def consume(idx, kj, s, mc, m):
        vt = vt_ref[:, pl.ds(pl.multiple_of(kj * t, t), t)]
        mn = jnp.maximum(m, mc)
        p = jnp.exp2(s - mn).astype(BF16)
        acc_ref[idx] = acc_ref[idx] * jnp.exp2(m - mn) + jnp.dot(vt, p, preferred_element_type=F32)
        return mn

    n_far = jnp.maximum(qi - (nd - 1), 0)
    last_far = jnp.maximum(n_far - 1, 0)

    def produce(slot, kj, bias_idx=None):
        s1, s2 = scores(kj)
        if bias_idx is not None:
            b = bias_ref[bias_idx]
            s1 = s1 + b
            s2 = s2 + b
        s_ref[slot, 0] = s1
        s_ref[slot, 1] = s2
        return colmax(s1), colmax(s2)

    def consume_slot(slot, kj, c, st):
        return (consume(0, kj, s_ref[slot, 0], c[0], st[0]),
                consume(1, kj, s_ref[slot, 1], c[1], st[1]))

    neg = jnp.full((1, t), NEG_BIG, F32)
    st = (neg, neg)

    pending = None
    for d in range(nd - 1, -1, -1):
        slot = (d + 1) % 2
        kj = jnp.maximum(qi - d, 0)
        c = produce(slot, kj, jnp.where(qi >= d, d, nd))
        if pending is not None:
            st = consume_slot(*pending, st)
        pending = (slot, kj, c)

    def produce_far(slot, kj):
        return produce(slot, jnp.minimum(kj, last_far))

    def far_pair(i, carry):
        st, c0 = carry
        c1 = produce_far(1, 2 * i + 1)
        st = consume_slot(0, 2 * i, c0, st)
        c0 = produce_far(0, 2 * i + 2)
        st = consume_slot(1, 2 * i + 1, c1, st)
        return st, c0

    def far_quad(i, carry):
        return far_pair(2 * i + 1, far_pair(2 * i, carry))

    c0 = produce_far(0, 0)
    st = consume_slot(*pending, st)
    carry = lax.fori_loop(0, n_far // 4, far_quad, (st, c0))
    st, c0 = lax.cond(n_far % 4 >= 2, lambda: far_pair(n_far // 4 * 2, carry), lambda: carry)

    @pl.when(n_far % 2 == 1)
    def _():
        consume_slot(0, last_far, c0, st)

    lam4 = lam_ref[...]
    lam = (jnp.exp(jnp.sum(lam4[0:1] * lam4[1:2], axis=-1, keepdims=True))
           - jnp.exp(jnp.sum(lam4[2:3] * lam4[3:4], axis=-1, keepdims=True)) + lam_init)
    o = (acc_ref[0, :V_DIM, :] / acc_ref[0, V_DIM:V_DIM + 1, :]
         - lam * (acc_ref[1, :V_DIM, :] / acc_ref[1, V_DIM:V_DIM + 1, :]))
    ms = jnp.mean(o * o, axis=0, keepdims=True)
    o = o * lax.rsqrt(ms + EPS) * subg_ref[...] * (1.0 - lam_init)
    o_ref[...] = o.T.astype(o_ref.dtype)


def _attn(qt, kn, vt, bias, lam4, subg, lam_init, t=ATT_T, tiles=ATT_TILES):
    bsz, heads, _, s = qt.shape
    nd = bias.shape[1] - 1
    return pl.pallas_call(
        functools.partial(_attn_body, t=t, nd=nd, lam_init=lam_init, tiles=tiles),
        grid=(bsz, heads, s // (tiles * t)),
        in_specs=[
            pl.BlockSpec((None, None, V_DIM, tiles * t), lambda b, h, i: (b, h, 0, i)),
            pl.BlockSpec((None, s, V_DIM), lambda b, h, i: (b, 0, h)),
            pl.BlockSpec((None, None, V_AUG, s), lambda b, h, i: (b, h, 0, 0)),
            pl.BlockSpec((None, nd + 1, t, t), lambda b, h, i: (h, 0, 0, 0)),
            pl.BlockSpec(lam4.shape, lambda b, h, i: (0, 0)),
            pl.BlockSpec(subg.shape, lambda b, h, i: (0, 0)),
        ],
        out_specs=pl.BlockSpec((None, tiles * t, V_DIM), lambda b, h, i: (b, i, h)),
        out_shape=jax.ShapeDtypeStruct((bsz, s, heads * V_DIM), BF16),
        scratch_shapes=[pltpu.VMEM((2, V_AUG, t), F32), pltpu.VMEM((2, 2, t, t), F32)],
        compiler_params=_cparams("parallel", "parallel", "arbitrary"),
        name="diff_attn",
    )(qt, kn, vt, bias, lam4, subg)


def _t5_bucket(rel, n_buckets):
    half = n_buckets // 2
    max_exact = half // 2
    ret = jnp.where(rel > 0, half, 0)
    n = jnp.abs(rel)
    nf = jnp.maximum(n, 1).astype(F32)
    large = max_exact + (jnp.log(nf / max_exact) / math.log(MAX_DISTANCE / max_exact)
                         * (half - max_exact)).astype(jnp.int32)
    large = jnp.minimum(large, half - 1)
    return ret + jnp.where(n < max_exact, n, large)


def _bias_body(f_ref, o_ref, *, t, nd):
    x = jnp.broadcast_to(f_ref[...], (t, 2 * t))
    y = pltpu.roll(x, t + 1, 1, stride=1, stride_axis=0)[:, :t]
    j = lax.broadcasted_iota(jnp.int32, (t, t), 0)
    i = lax.broadcasted_iota(jnp.int32, (t, t), 1)
    delta = pl.program_id(1)
    allowed = (((j // CHUNK) <= (i // CHUNK)) | (delta > 0)) & (delta < nd)
    o_ref[...] = jnp.where(allowed, y, NEG_BIG)


def _bias_tiles(rel_bias, t, nd):
    n_buckets, heads = rel_bias.shape
    c = jnp.arange(2 * t, dtype=jnp.int32)[None, :]
    delta = jnp.arange(nd + 1, dtype=jnp.int32)[:, None]
    rel = (t - 1 - c) - delta * t
    f = (rel_bias[_t5_bucket(rel, n_buckets)] - rel_bias[n_buckets // 2 - 1]) * LOG2E
    f = jnp.transpose(f, (2, 0, 1))[:, :, None, :]
    return pl.pallas_call(
        functools.partial(_bias_body, t=t, nd=nd),
        grid=(heads, nd + 1),
        in_specs=[pl.BlockSpec((None, None, 1, 2 * t), lambda h, d: (h, d, 0, 0))],
        out_specs=pl.BlockSpec((None, None, t, t), lambda h, d: (h, d, 0, 0)),
        out_shape=jax.ShapeDtypeStruct((heads, nd + 1, t, t), F32),
        compiler_params=_cparams("parallel", "parallel"),
        name="bias_tiles",
    )(f)


def _merge_body(x_ref, ya_ref, yb_ref, yc_ref, g0_ref, g1_ref, g2_ref, bg_ref, wpa_ref, wpb_ref,
                wpc_ref, wo_ref, o_ref):
    def branch(idx, y_ref, w_ref, g_ref):
        p = jnp.dot(y_ref[...], w_ref[...], preferred_element_type=F32)
        return jax.nn.sigmoid(g_ref[...].astype(F32) + bg_ref[idx]) * p

    merged = (branch(0, ya_ref, wpa_ref, g0_ref) + branch(1, yb_ref, wpb_ref, g1_ref)
              + branch(2, yc_ref, wpc_ref, g2_ref))
    o_ref[...] = x_ref[...] + jnp.dot(merged.astype(BF16), wo_ref[...], preferred_element_type=F32)


def _merge(x2, ya, yb, yc, gates, bg, wpa, wpb, wpc, wo, l, tm=512):
    n, d = x2.shape
    rows = lambda a: pl.BlockSpec((tm, a.shape[-1]), lambda i: (i, 0))
    wspec = lambda a: pl.BlockSpec((None,) + a.shape[1:], lambda i: (l, 0, 0))
    gate = lambda g: pl.BlockSpec((tm, d), lambda i: (i, g))
    return pl.pallas_call(
        _merge_body,
        grid=(n // tm,),
        in_specs=[rows(x2), rows(ya), rows(yb), rows(yc), gate(0), gate(1), gate(2),
                  pl.BlockSpec(bg.shape, lambda i: (0, 0, 0)),
                  wspec(wpa), wspec(wpb), wspec(wpc), wspec(wo)],
        out_specs=pl.BlockSpec((tm, d), lambda i: (i, 0)),
        out_shape=jax.ShapeDtypeStruct((n, d), F32),
        compiler_params=_cparams("parallel"),
        name="merge",
    )(x2, ya, yb, yc, gates, gates, gates, bg, wpa, wpb, wpc, wo)


def _ffn_body(x_ref, g_ref, wg_ref, wu_ref, wd_ref, o_ref):
    x = x_ref[...]
    h = _rms_norm_bf16(x, g_ref[...])
    gate = jnp.dot(h, wg_ref[...], preferred_element_type=F32)
    up = jnp.dot(h, wu_ref[...], preferred_element_type=F32)
    act = (jax.nn.silu(gate) * up).astype(BF16)
    o_ref[...] = x + jnp.dot(act, wd_ref[...], preferred_element_type=F32)


def _ffn(x2, g, wg, wu, wd, l, tm=512):
    n, d = x2.shape
    hid = wg.shape[-1]
    return pl.pallas_call(
        _ffn_body,
        grid=(n // tm,),
        in_specs=[
            pl.BlockSpec((tm, d), lambda i: (i, 0)),
            pl.BlockSpec((1, d), lambda i: (0, 0)),
            _resident((None, d, hid), lambda i: (l, 0, 0)),
            _resident((None, d, hid), lambda i: (l, 0, 0)),
            _resident((None, hid, d), lambda i: (l, 0, 0)),
        ],
        out_specs=pl.BlockSpec((tm, d), lambda i: (i, 0)),
        out_shape=jax.ShapeDtypeStruct((n, d), F32),
        compiler_params=_cparams("parallel"),
        name="ffn",
    )(x2, g, wg, wu, wd)


def _block_diag(w):
    heads, a, b = w.shape
    eye = jnp.eye(heads, dtype=w.dtype)
    return (eye[:, None, :, None] * w[:, :, None, :]).reshape(heads * a, heads * b)


def kernel(x, ln1_g, w_in, b_gate, conv_w, conv_b, lru_wa, lru_ba, lru_wi, lru_bi, lru_lambda, sg_ln_g, sg_ln_b, sg_w, sg_b, q_norm_g, k_norm_g, lambda_q1, lambda_k1, lambda_q2, lambda_k2, subln_g, rel_bias, w_pa, w_pb, w_pc, w_o, ln2_g, w_ff_gate, w_ff_up, w_ff_down):
    bsz, s, d = x.shape
    depth = w_in.shape[0]
    lru_w = conv_w.shape[-1]
    sg_wd = sg_ln_g.shape[-1]
    groups = sg_w.shape[1]
    heads = rel_bias.shape[1]
    wq = heads * 2 * QK_DIM
    assert w_in.shape[-1] == 2 * lru_w + 2 * sg_wd + 3 * wq + 3 * d and lru_w == sg_wd
    assert s % (ATT_TILES * ATT_T) == 0 and ATT_T % CHUNK == 0

    nd = -(-(MAX_DISTANCE // 2 + ATT_T) // ATT_T)
    bias = _bias_tiles(rel_bias, ATT_T, nd)
    bd = _block_diag(jnp.ones((wq // QK_DIM, QK_DIM, QK_DIM), BF16))
    w_in_b, w_pa_b, w_pb_b, w_pc_b, w_o_b = (w.astype(BF16) for w in (w_in, w_pa, w_pb, w_pc, w_o))
    wg_b, wu_b, wd_b = (w.astype(BF16) for w in (w_ff_gate, w_ff_up, w_ff_down))
    causal = jnp.tril(jnp.ones((SG_LEN, SG_LEN), bool))

    x2 = x.reshape(bsz * s, d)
    for l in range(depth):
        lam_init = 0.8 - 0.6 * math.exp(-0.3 * l)
        cvec = (-LRU_C * jax.nn.softplus(-lru_lambda[l]))[None]
        lru_params = (conv_w[l], conv_b[l][None], _block_diag(lru_wa[l]).astype(BF16), lru_ba[l][None],
                      _block_diag(lru_wi[l]).astype(BF16), lru_bi[l][None], cvec)
        ws = jnp.where(causal, sg_w[l], 0.0).astype(BF16)
        ws_cat = jnp.transpose(ws, (1, 0, 2)).reshape(SG_LEN, groups * SG_LEN)
        bs_full = jnp.repeat(sg_b[l].T, sg_wd // groups, axis=1)
        sgu_params = (sg_ln_g[l][None], sg_ln_b[l][None], ws_cat, bs_full)
        gq = jnp.tile(q_norm_g[l] * (QK_DIM ** -0.5 * LOG2E), wq // QK_DIM)[None]
        gk = jnp.tile(k_norm_g[l], wq // QK_DIM)[None]
        ya, yb, qt, kn, vt, gates = _front(x2, bsz, ln1_g[l][None], w_in_b, l, lru_params, sgu_params,
                                           (gq, gk, bd), heads, groups)
        lam4 = jnp.stack([lambda_q1[l], lambda_k1[l], lambda_q2[l], lambda_k2[l]])
        subg = jnp.broadcast_to(subln_g[l][:, None], (V_DIM, ATT_T))
        yc = _attn(qt, kn, vt, bias, lam4, subg, lam_init)

        x2 = _merge(x2, ya, yb, yc.reshape(bsz * s, -1), gates, b_gate[l].reshape(3, 1, d),
                    w_pa_b, w_pb_b, w_pc_b, w_o_b, l)
        x2 = _ffn(x2, ln2_g[l][None], wg_b, wu_b, wd_b, l)
    return x2.reshape(bsz, s, d)
```

```python
import functools
import math

import jax
import jax.numpy as jnp
from jax import lax
from jax.experimental import pallas as pl
from jax.experimental.pallas import tpu as pltpu

F32 = jnp.float32
BF16 = jnp.bfloat16

EPS = 1e-6
CHUNK = 64
LRU_C = 8.0
SG_LEN = 128
QK_DIM = 64
V_DIM = 128
V_AUG = V_DIM + 16
MAX_DISTANCE = 2048
NEG_BIG = -1e30
LOG2E = math.log2(math.e)

ATT_T = 512
VMEM_LIMIT = 56 * 1024 * 1024


def _cparams(*sem):
    return pltpu.CompilerParams(dimension_semantics=sem, vmem_limit_bytes=VMEM_LIMIT)


def _rms_norm_bf16(x, g):
    ms = jnp.mean(x * x, axis=-1, keepdims=True)
    return (x * lax.rsqrt(ms + EPS) * g).astype(BF16)


def _resident(shape, index_map):
    return pl.BlockSpec(shape, index_map, pipeline_mode=pl.Buffered(1))


def _lru_branch(xa, ga, cw_ref, cb_ref, wa_ref, ba_ref, wi_ref, bi_ref, c_ref, ext_ref, h_ref):
    ts, w = xa.shape
    ext_ref[8:8 + ts, :] = xa
    xc = (cb_ref[...] + ext_ref[5:5 + ts, :] * cw_ref[0:1, :] + ext_ref[6:6 + ts, :] * cw_ref[1:2, :]
          + ext_ref[7:7 + ts, :] * cw_ref[2:3, :] + xa * cw_ref[3:4, :])
    ext_ref[0:8, :] = ext_ref[ts:ts + 8, :]

    xcb = xc.astype(BF16)
    r = jax.nn.sigmoid(jnp.dot(xcb, wa_ref[...], preferred_element_type=F32) + ba_ref[...])
    i = jax.nn.sigmoid(jnp.dot(xcb, wi_ref[...], preferred_element_type=F32) + bi_ref[...])
    log_a = c_ref[...] * r
    a = jnp.exp(log_a)
    m2 = -jnp.tanh(log_a) * (a * a + 1.0)
    mult = jnp.where(m2 > 0.0, m2 * lax.rsqrt(m2), 0.0)
    b = mult * (i * xc)

    sub = 8
    a3 = a.reshape(ts // sub, sub, w)
    b3 = b.reshape(ts // sub, sub, w)
    row = lax.broadcasted_iota(jnp.int32, a3.shape, 1)
    d = 1
    while d < sub:
        keep = row >= d
        a_sh = jnp.where(keep, pltpu.roll(a3, d, 1), 1.0)
        b_sh = jnp.where(keep, pltpu.roll(b3, d, 1), 0.0)
        b3 = a3 * b_sh + b3
        a3 = a3 * a_sh
        d *= 2
    carry = h_ref[...]
    groups = []
    for g in range(ts // sub):
        hg = a3[g] * carry + b3[g]
        groups.append(hg)
        carry = hg[sub - 1:sub, :]
    h_ref[...] = carry
    h = jnp.concatenate(groups, axis=0)
    return h * jax.nn.gelu(ga)


def _sgu_branch(u, v, g_ref, b_ref, ws_ref, bs_ref, groups):
    tr, w = v.shape
    gd = w // groups
    mu = jnp.mean(v, axis=-1, keepdims=True)
    vc = v - mu
    var = jnp.mean(vc * vc, axis=-1, keepdims=True)
    vn = (vc * lax.rsqrt(var + EPS) * g_ref[...] + b_ref[...]).astype(BF16)
    lane_grp = lax.broadcasted_iota(jnp.int32, (SG_LEN, w), 1) // gd
    ws = ws_ref[...]
    zero = jnp.zeros((SG_LEN, w), BF16)
    out = []
    for blk in range(tr // SG_LEN):
        vb = vn[blk * SG_LEN:(blk + 1) * SG_LEN]
        stacked = jnp.concatenate([jnp.where(lane_grp == g, vb, zero) for g in range(groups)], axis=0)
        out.append(jnp.dot(ws, stacked, preferred_element_type=F32) + bs_ref[...])
    return u * jnp.concatenate(out, axis=0)


def _group_rms(y, g_ref, bd_ref):
    ss = jnp.dot((y * y).astype(BF16), bd_ref[...], preferred_element_type=F32)
    return y * lax.rsqrt(ss * (1.0 / QK_DIM) + EPS) * g_ref[...]


def _front_body(x_ref, g1_ref, w_ref, cw_ref, cb_ref, wa_ref, ba_ref, wi_ref, bi_ref, c_ref,
                lng_ref, lnb_ref, ws_ref, bs_ref, gq_ref, gk_ref, bd_ref,
                ya_ref, yb_ref, qt_ref, kn_ref, vt_ref, gates_ref, ext_ref, h_ref, *,
                heads, groups, lru_w, wq):
    @pl.when(pl.program_id(1) == 0)
    def _():
        ext_ref[0:8, :] = jnp.zeros((8, lru_w), F32)
        h_ref[...] = jnp.zeros_like(h_ref)

    h = _rms_norm_bf16(x_ref[...], g1_ref[...])

    def proj(c0, c1):
        return jnp.dot(h, w_ref[:, c0:c1], preferred_element_type=F32)

    c0 = 4 * lru_w
    pa = proj(0, 2 * lru_w)
    ya_ref[...] = _lru_branch(pa[:, :lru_w], pa[:, lru_w:], cw_ref, cb_ref, wa_ref, ba_ref, wi_ref,
                              bi_ref, c_ref, ext_ref, h_ref).astype(ya_ref.dtype)
    pb = proj(2 * lru_w, 4 * lru_w)
    yb_ref[...] = _sgu_branch(pb[:, :lru_w], pb[:, lru_w:], lng_ref, lnb_ref, ws_ref, bs_ref,
                              groups).astype(yb_ref.dtype)
    qn = _group_rms(proj(c0, c0 + wq), gq_ref, bd_ref)
    kn_ref[...] = _group_rms(proj(c0 + wq, c0 + 2 * wq), gk_ref, bd_ref).astype(BF16)
    v = proj(c0 + 2 * wq, c0 + 3 * wq)
    for hd in range(heads):
        cols = slice(hd * V_DIM, (hd + 1) * V_DIM)
        qt_ref[hd] = qn[:, cols].T.astype(BF16)
        vt_ref[hd, :V_DIM, :] = v[:, cols].T.astype(BF16)
        vt_ref[hd, V_DIM:, :] = jnp.ones((V_AUG - V_DIM, v.shape[0]), BF16)
    gates_ref[...] = proj(c0 + 3 * wq, w_ref.shape[-1]).astype(gates_ref.dtype)


def _front(x2, bsz, g1, w, l, lru_params, sgu_params, qk_params, heads, groups, tm=512):
    n, d = x2.shape
    s = n // bsz
    nt = s // tm
    c = w.shape[-1]
    lru_w = lru_params[0].shape[-1]
    wq = heads * 2 * QK_DIM
    n_gate = c - 4 * lru_w - 3 * wq
    small = list(lru_params) + list(sgu_params) + list(qk_params)
    full = lambda a: pl.BlockSpec(a.shape, lambda b, t: (0,) * a.ndim)
    rows = lambda width: pl.BlockSpec((tm, width), lambda b, t: (b * nt + t, 0))
    tspec = lambda r: pl.BlockSpec((None, heads, r, tm), lambda b, t: (b, 0, 0, t))
    return pl.pallas_call(
        functools.partial(_front_body, heads=heads, groups=groups, lru_w=lru_w, wq=wq),
        grid=(bsz, nt),
        in_specs=[rows(d), pl.BlockSpec((1, d), lambda b, t: (0, 0)),
                  _resident((None, d, c), lambda b, t: (l, 0, 0))] + [full(a) for a in small],
        out_specs=[rows(lru_w), rows(lru_w), tspec(V_DIM),
                   pl.BlockSpec((None, tm, wq), lambda b, t: (b, t, 0)), tspec(V_AUG), rows(n_gate)],
        out_shape=[
            jax.ShapeDtypeStruct((n, lru_w), BF16),
            jax.ShapeDtypeStruct((n, lru_w), BF16),
            jax.ShapeDtypeStruct((bsz, heads, V_DIM, s), BF16),
            jax.ShapeDtypeStruct((bsz, s, wq), BF16),
            jax.ShapeDtypeStruct((bsz, heads, V_AUG, s), BF16),
            jax.ShapeDtypeStruct((n, n_gate), BF16),
        ],
        scratch_shapes=[pltpu.VMEM((tm + 8, lru_w), F32), pltpu.VMEM((1, lru_w), F32)],
        compiler_params=_cparams("parallel", "arbitrary"),
        name="front",
    )(x2, g1, w, *small)


def _attn_body(qt_ref, k_ref, vt_ref, bias_ref, lam_ref, subg_ref, o_ref, acc_ref, s_ref, *, t, nd,
               lam_init):
    step = pl.program_id(2)
    n_tiles = pl.num_programs(2) - 1
    qi = jnp.minimum(step, n_tiles - 1)
    acc_cur = acc_ref.at[step % 2]
    acc_prev = acc_ref.at[1 - step % 2]

    @pl.when(step == 0)
    def _():
        acc_prev[...] = jnp.ones_like(acc_prev)

    lam4 = lam_ref[...]
    lam = (jnp.exp(jnp.sum(lam4[0:1] * lam4[1:2], axis=-1, keepdims=True))
           - jnp.exp(jnp.sum(lam4[2:3] * lam4[3:4], axis=-1, keepdims=True)) + lam_init)
    o = (acc_prev[0, :V_DIM, :] / acc_prev[0, V_DIM:V_DIM + 1, :]
         - lam * (acc_prev[1, :V_DIM, :] / acc_prev[1, V_DIM:V_DIM + 1, :]))
    ms = jnp.mean(o * o, axis=0, keepdims=True)
    o = o * lax.rsqrt(ms + EPS) * subg_ref[...] * (1.0 - lam_init)
    o_ref[...] = o.T.astype(o_ref.dtype)

    qt = qt_ref[...]
    row = lax.broadcasted_iota(jnp.int32, qt.shape, 0)
    zero = jnp.zeros_like(qt)
    q1 = jnp.where(row < QK_DIM, qt, zero)
    q2 = jnp.where(row >= QK_DIM, qt, zero)

    def scores(kj):
        k = k_ref[pl.ds(pl.multiple_of(kj * t, t), t), :]
        return (jnp.dot(k, q1, preferred_element_type=F32),
                jnp.dot(k, q2, preferred_element_type=F32))

    def colmax(s):
        return jnp.max(s, axis=0, keepdims=True)

    def consume(idx, kj, s, mc, m):
        vt = vt_ref[:, pl.ds(pl.multiple_of(kj * t, t), t)]
        mn = jnp.maximum(m, mc)
        p = jnp.exp2(s - mn).astype(BF16)
        acc_cur[idx] = acc_cur[idx] * jnp.exp2(m - mn) + jnp.dot(vt, p, preferred_element_type=F32)
        return mn

    n_far = jnp.maximum(qi - (nd - 1), 0)
    last_far = jnp.maximum(n_far - 1, 0)

    def produce(slot, kj, bias_idx=None):
        s1, s2 = scores(kj)
        if bias_idx is not None:
            b = bias_ref[bias_idx]
            s1 = s1 + b
            s2 = s2 + b
        s_ref[slot, 0] = s1
        s_ref[slot, 1] = s2
        return colmax(s1), colmax(s2)

    def consume_slot(slot, kj, c, st):
        return (consume(0, kj, s_ref[slot, 0], c[0], st[0]),
                consume(1, kj, s_ref[slot, 1], c[1], st[1]))

    def near_tile(d):
        slot = (d + 1) % 2
        kj = jnp.maximum(qi - d, 0)
        return slot, kj, produce(slot, kj, jnp.where(qi >= d, d, nd))

    first = near_tile(nd - 1)

    @pl.when(step < n_tiles)
    def _():
        acc_cur[...] = jnp.zeros_like(acc_cur)
        neg = jnp.full((1, t), NEG_BIG, F32)
        st = (neg, neg)
        pending = first
        for d in range(nd - 2, -1, -1):
            nxt = near_tile(d)
            st = consume_slot(*pending, st)
            pending = nxt

        def produce_far(slot, kj):
            return produce(slot, jnp.minimum(kj, last_far))

        def far_pair(i, carry):
            st, c0 = carry
            c1 = produce_far(1, 2 * i + 1)
            st = consume_slot(0, 2 * i, c0, st)
            c0 = produce_far(0, 2 * i + 2)
            st = consume_slot(1, 2 * i + 1, c1, st)
            return st, c0

        def far_quad(i, carry):
            return far_pair(2 * i + 1, far_pair(2 * i, carry))

        c0 = produce_far(0, 0)
        st = consume_slot(*pending, st)
        carry = lax.fori_loop(0, n_far // 4, far_quad, (st, c0))
        st, c0 = lax.cond(n_far % 4 >= 2, lambda: far_pair(n_far // 4 * 2, carry), lambda: carry)

        @pl.when(n_far % 2 == 1)
        def _():
            consume_slot(0, last_far, c0, st)


def _attn(qt, kn, vt, bias, lam4, subg, lam_init, t=ATT_T):
    bsz, heads, _, s = qt.shape
    nd = bias.shape[1] - 1
    n_tiles = s // t
    return pl.pallas_call(
        functools.partial(_attn_body, t=t, nd=nd, lam_init=lam_init),
        grid=(bsz, heads, n_tiles + 1),
        in_specs=[
            pl.BlockSpec((None, None, V_DIM, t), lambda b, h, i: (b, h, 0, jnp.minimum(i, n_tiles - 1))),
            pl.BlockSpec((None, s, V_DIM), lambda b, h, i: (b, 0, h)),
            pl.BlockSpec((None, None, V_AUG, s), lambda b, h, i: (b, h, 0, 0)),
            pl.BlockSpec((None, nd + 1, t, t), lambda b, h, i: (h, 0, 0, 0)),
            pl.BlockSpec(lam4.shape, lambda b, h, i: (0, 0)),
            pl.BlockSpec(subg.shape, lambda b, h, i: (0, 0)),
        ],
        out_specs=pl.BlockSpec((None, t, V_DIM), lambda b, h, i: (b, jnp.maximum(i - 1, 0), h)),
        out_shape=jax.ShapeDtypeStruct((bsz, s, heads * V_DIM), BF16),
        scratch_shapes=[pltpu.VMEM((2, 2, V_AUG, t), F32), pltpu.VMEM((2, 2, t, t), F32)],
        compiler_params=_cparams("parallel", "parallel", "arbitrary"),
        name="diff_attn",
    )(qt, kn, vt, bias, lam4, subg)


def _t5_bucket(rel, n_buckets):
    half = n_buckets // 2
    max_exact = half // 2
    ret = jnp.where(rel > 0, half, 0)
    n = jnp.abs(rel)
    nf = jnp.maximum(n, 1).astype(F32)
    large = max_exact + (jnp.log(nf / max_exact) / math.log(MAX_DISTANCE / max_exact)
                         * (half - max_exact)).astype(jnp.int32)
    large = jnp.minimum(large, half - 1)
    return ret + jnp.where(n < max_exact, n, large)


def _bias_body(f_ref, o_ref, *, t, nd):
    x = jnp.broadcast_to(f_ref[...], (t, 2 * t))
    y = pltpu.roll(x, t + 1, 1, stride=1, stride_axis=0)[:, :t]
    j = lax.broadcasted_iota(jnp.int32, (t, t), 0)
    i = lax.broadcasted_iota(jnp.int32, (t, t), 1)
    delta = pl.program_id(1)
    allowed = (((j // CHUNK) <= (i // CHUNK)) | (delta > 0)) & (delta < nd)
    o_ref[...] = jnp.where(allowed, y, NEG_BIG)


def _bias_tiles(rel_bias, t, nd):
    n_buckets, heads = rel_bias.shape
    c = jnp.arange(2 * t, dtype=jnp.int32)[None, :]
    delta = jnp.arange(nd + 1, dtype=jnp.int32)[:, None]
    rel = (t - 1 - c) - delta * t
    f = (rel_bias[_t5_bucket(rel, n_buckets)] - rel_bias[n_buckets // 2 - 1]) * LOG2E
    f = jnp.transpose(f, (2, 0, 1))[:, :, None, :]
    return pl.pallas_call(
        functools.partial(_bias_body, t=t, nd=nd),
        grid=(heads, nd + 1),
        in_specs=[pl.BlockSpec((None, None, 1, 2 * t), lambda h, d: (h, d, 0, 0))],
        out_specs=pl.BlockSpec((None, None, t, t), lambda h, d: (h, d, 0, 0)),
        out_shape=jax.ShapeDtypeStruct((heads, nd + 1, t, t), F32),
        compiler_params=_cparams("parallel", "parallel"),
        name="bias_tiles",
    )(f)


def _merge_body(x_ref, ya_ref, yb_ref, yc_ref, g0_ref, g1_ref, g2_ref, bg_ref, wpa_ref, wpb_ref,
                wpc_ref, wo_ref, o_ref):
    def branch(idx, y_ref, w_ref, g_ref):
        p = jnp.dot(y_ref[...], w_ref[...], preferred_element_type=F32)
        return jax.nn.sigmoid(g_ref[...].astype(F32) + bg_ref[idx]) * p

    merged = (branch(0, ya_ref, wpa_ref, g0_ref) + branch(1, yb_ref, wpb_ref, g1_ref)
              + branch(2, yc_ref, wpc_ref, g2_ref))
    o_ref[...] = x_ref[...] + jnp.dot(merged.astype(BF16), wo_ref[...], preferred_element_type=F32)


def _merge(x2, ya, yb, yc, gates, bg, wpa, wpb, wpc, wo, l, tm=1024):
    n, d = x2.shape
    rows = lambda a: pl.BlockSpec((tm, a.shape[-1]), lambda i: (i, 0))
    wspec = lambda a: pl.BlockSpec((None,) + a.shape[1:], lambda i: (l, 0, 0))
    gate = lambda g: pl.BlockSpec((tm, d), lambda i: (i, g))
    return pl.pallas_call(
        _merge_body,
        grid=(n // tm,),
        in_specs=[rows(x2), rows(ya), rows(yb), rows(yc), gate(0), gate(1), gate(2),
                  pl.BlockSpec(bg.shape, lambda i: (0, 0, 0)),
                  wspec(wpa), wspec(wpb), wspec(wpc), wspec(wo)],
        out_specs=pl.BlockSpec((tm, d), lambda i: (i, 0)),
        out_shape=jax.ShapeDtypeStruct((n, d), F32),
        compiler_params=_cparams("parallel"),
        name="merge",
    )(x2, ya, yb, yc, gates, gates, gates, bg, wpa, wpb, wpc, wo)


def _ffn_body(x_ref, g_ref, wg_ref, wu_ref, wd_ref, o_ref):
    x = x_ref[...]
    h = _rms_norm_bf16(x, g_ref[...])
    gate = jnp.dot(h, wg_ref[...], preferred_element_type=F32)
    up = jnp.dot(h, wu_ref[...], preferred_element_type=F32)
    act = (jax.nn.silu(gate) * up).astype(BF16)
    o_ref[...] = x + jnp.dot(act, wd_ref[...], preferred_element_type=F32)


def _ffn(x2, g, wg, wu, wd, l, tm=512):
    n, d = x2.shape
    hid = wg.shape[-1]
    return pl.pallas_call(
        _ffn_body,
        grid=(n // tm,),
        in_specs=[
            pl.BlockSpec((tm, d), lambda i: (i, 0)),
            pl.BlockSpec((1, d), lambda i: (0, 0)),
            _resident((None, d, hid), lambda i: (l, 0, 0)),
            _resident((None, d, hid), lambda i: (l, 0, 0)),
            _resident((None, hid, d), lambda i: (l, 0, 0)),
        ],
        out_specs=pl.BlockSpec((tm, d), lambda i: (i, 0)),
        out_shape=jax.ShapeDtypeStruct((n, d), F32),
        compiler_params=_cparams("parallel"),
        name="ffn",
    )(x2, g, wg, wu, wd)


def _block_diag(w):
    heads, a, b = w.shape
    eye = jnp.eye(heads, dtype=w.dtype)
    return (eye[:, None, :, None] * w[:, :, None, :]).reshape(heads * a, heads * b)


def kernel(x, ln1_g, w_in, b_gate, conv_w, conv_b, lru_wa, lru_ba, lru_wi, lru_bi, lru_lambda, sg_ln_g, sg_ln_b, sg_w, sg_b, q_norm_g, k_norm_g, lambda_q1, lambda_k1, lambda_q2, lambda_k2, subln_g, rel_bias, w_pa, w_pb, w_pc, w_o, ln2_g, w_ff_gate, w_ff_up, w_ff_down):
    bsz, s, d = x.shape
    depth = w_in.shape[0]
    lru_w = conv_w.shape[-1]
    sg_wd = sg_ln_g.shape[-1]
    groups = sg_w.shape[1]
    heads = rel_bias.shape[1]
    wq = heads * 2 * QK_DIM
    assert w_in.shape[-1] == 2 * lru_w + 2 * sg_wd + 3 * wq + 3 * d and lru_w == sg_wd
    assert s % ATT_T == 0 and ATT_T % CHUNK == 0

    nd = -(-(MAX_DISTANCE // 2 + ATT_T) // ATT_T)
    bias = _bias_tiles(rel_bias, ATT_T, nd)
    bd = _block_diag(jnp.ones((wq // QK_DIM, QK_DIM, QK_DIM), BF16))
    w_in_b, w_pa_b, w_pb_b, w_pc_b, w_o_b = (w.astype(BF16) for w in (w_in, w_pa, w_pb, w_pc, w_o))
    wg_b, wu_b, wd_b = (w.astype(BF16) for w in (w_ff_gate, w_ff_up, w_ff_down))
    causal = jnp.tril(jnp.ones((SG_LEN, SG_LEN), bool))

    x2 = x.reshape(bsz * s, d)
    for l in range(depth):
        lam_init = 0.8 - 0.6 * math.exp(-0.3 * l)
        cvec = (-LRU_C * jax.nn.softplus(-lru_lambda[l]))[None]
        lru_params = (conv_w[l], conv_b[l][None], _block_diag(lru_wa[l]).astype(BF16), lru_ba[l][None],
                      _block_diag(lru_wi[l]).astype(BF16), lru_bi[l][None], cvec)
        ws = jnp.where(causal, sg_w[l], 0.0).astype(BF16)
        ws_cat = jnp.transpose(ws, (1, 0, 2)).reshape(SG_LEN, groups * SG_LEN)
        bs_full = jnp.repeat(sg_b[l].T, sg_wd // groups, axis=1)
        sgu_params = (sg_ln_g[l][None], sg_ln_b[l][None], ws_cat, bs_full)
        gq = jnp.tile(q_norm_g[l] * (QK_DIM ** -0.5 * LOG2E), wq // QK_DIM)[None]
        gk = jnp.tile(k_norm_g[l], wq // QK_DIM)[None]
        ya, yb, qt, kn, vt, gates = _front(x2, bsz, ln1_g[l][None], w_in_b, l, lru_params, sgu_params,
                                           (gq, gk, bd), heads, groups)
        lam4 = jnp.stack([lambda_q1[l], lambda_k1[l], lambda_q2[l], lambda_k2[l]])
        subg = jnp.broadcast_to(subln_g[l][:, None], (V_DIM, ATT_T))
        yc = _attn(qt, kn, vt, bias, lam4, subg, lam_init)

        x2 = _merge(x2, ya, yb, yc.reshape(bsz * s, -1), gates, b_gate[l].reshape(3, 1, d),
                    w_pa_b, w_pb_b, w_pc_b, w_o_b, l)
        x2 = _ffn(x2, ln2_g[l][None], wg_b, wu_b, wd_b, l)
    return x2.reshape(bsz, s, d)
```

```python
import functools
import math

import jax
import jax.numpy as jnp
from jax import lax
from jax.experimental import pallas as pl
from jax.experimental.pallas import tpu as pltpu

F32 = jnp.float32
BF16 = jnp.bfloat16

EPS = 1e-6
CHUNK = 64
LRU_C = 8.0
SG_LEN = 128
QK_DIM = 64
V_DIM = 128
V_AUG = V_DIM + 16
MAX_DISTANCE = 2048
NEG_BIG = -1e30
LOG2E = math.log2(math.e)

ATT_T = 512
VMEM_LIMIT = 56 * 1024 * 1024


def _cparams(*sem):
    return pltpu.CompilerParams(dimension_semantics=sem, vmem_limit_bytes=VMEM_LIMIT)


def _rms_norm_bf16(x, g):
    ms = jnp.mean(x * x, axis=-1, keepdims=True)
    return (x * lax.rsqrt(ms + EPS) * g).astype(BF16)


def _resident(shape, index_map):
    return pl.BlockSpec(shape, index_map, pipeline_mode=pl.Buffered(1))


def _lru_branch(xa, ga, cw_ref, cb_ref, wa_ref, ba_ref, wi_ref, bi_ref, c_ref, ext_ref, h_ref):
    ts, w = xa.shape
    ext_ref[8:8 + ts, :] = xa
    xc = (cb_ref[...] + ext_ref[5:5 + ts, :] * cw_ref[0:1, :] + ext_ref[6:6 + ts, :] * cw_ref[1:2, :]
          + ext_ref[7:7 + ts, :] * cw_ref[2:3, :] + xa * cw_ref[3:4, :])
    ext_ref[0:8, :] = ext_ref[ts:ts + 8, :]

    xcb = xc.astype(BF16)
    r = jax.nn.sigmoid(jnp.dot(xcb, wa_ref[...], preferred_element_type=F32) + ba_ref[...])
    i = jax.nn.sigmoid(jnp.dot(xcb, wi_ref[...], preferred_element_type=F32) + bi_ref[...])
    log_a = c_ref[...] * r
    a = jnp.exp(log_a)
    mult = jnp.sqrt(-jnp.tanh(log_a) * (a * a + 1.0))
    b = mult * (i * xc)

    sub = 8
    a3 = a.reshape(ts // sub, sub, w)
    b3 = b.reshape(ts // sub, sub, w)
    row = lax.broadcasted_iota(jnp.int32, a3.shape, 1)
    d = 1
    while d < sub:
        keep = row >= d
        a_sh = jnp.where(keep, pltpu.roll(a3, d, 1), 1.0)
        b_sh = jnp.where(keep, pltpu.roll(b3, d, 1), 0.0)
        b3 = a3 * b_sh + b3
        a3 = a3 * a_sh
        d *= 2
    carry = h_ref[...]
    groups = []
    for g in range(ts // sub):
        hg = a3[g] * carry + b3[g]
        groups.append(hg)
        carry = hg[sub - 1:sub, :]
    h_ref[...] = carry
    h = jnp.concatenate(groups, axis=0)
    return h * jax.nn.gelu(ga)


def _sgu_branch(u, v, g_ref, b_ref, ws_ref, bs_ref, groups):
    tr, w = v.shape
    gd = w // groups
    mu = jnp.mean(v, axis=-1, keepdims=True)
    vc = v - mu
    var = jnp.mean(vc * vc, axis=-1, keepdims=True)
    vn = (vc * lax.rsqrt(var + EPS) * g_ref[...] + b_ref[...]).astype(BF16)
    lane_grp = lax.broadcasted_iota(jnp.int32, (SG_LEN, w), 1) // gd
    ws = ws_ref[...]
    zero = jnp.zeros((SG_LEN, w), BF16)
    out = []
    for blk in range(tr // SG_LEN):
        vb = vn[blk * SG_LEN:(blk + 1) * SG_LEN]
        stacked = jnp.concatenate([jnp.where(lane_grp == g, vb, zero) for g in range(groups)], axis=0)
        out.append(jnp.dot(ws, stacked, preferred_element_type=F32) + bs_ref[...])
    return u * jnp.concatenate(out, axis=0)


def _group_rms(y, g_ref, bd_ref):
    ss = jnp.dot((y * y).astype(BF16), bd_ref[...], preferred_element_type=F32)
    return y * lax.rsqrt(ss * (1.0 / QK_DIM) + EPS) * g_ref[...]


def _front_body(x_ref, g1_ref, w_ref, cw_ref, cb_ref, wa_ref, ba_ref, wi_ref, bi_ref, c_ref,
                lng_ref, lnb_ref, ws_ref, bs_ref, gq_ref, gk_ref, bd_ref,
                ya_ref, yb_ref, qt_ref, kn_ref, vt_ref, gates_ref, ext_ref, h_ref, *,
                heads, groups, lru_w, wq):
    @pl.when(pl.program_id(1) == 0)
    def _():
        ext_ref[0:8, :] = jnp.zeros((8, lru_w), F32)
        h_ref[...] = jnp.zeros_like(h_ref)

    h = _rms_norm_bf16(x_ref[...], g1_ref[...])

    def proj(c0, c1):
        return jnp.dot(h, w_ref[:, c0:c1], preferred_element_type=F32)

    c0 = 4 * lru_w
    pa = proj(0, 2 * lru_w)
    ya_ref[...] = _lru_branch(pa[:, :lru_w], pa[:, lru_w:], cw_ref, cb_ref, wa_ref, ba_ref, wi_ref,
                              bi_ref, c_ref, ext_ref, h_ref).astype(ya_ref.dtype)
    pb = proj(2 * lru_w, 4 * lru_w)
    yb_ref[...] = _sgu_branch(pb[:, :lru_w], pb[:, lru_w:], lng_ref, lnb_ref, ws_ref, bs_ref,
                              groups).astype(yb_ref.dtype)
    qn = _group_rms(proj(c0, c0 + wq), gq_ref, bd_ref)
    kn_ref[...] = _group_rms(proj(c0 + wq, c0 + 2 * wq), gk_ref, bd_ref).astype(BF16)
    v = proj(c0 + 2 * wq, c0 + 3 * wq)
    for hd in range(heads):
        cols = slice(hd * V_DIM, (hd + 1) * V_DIM)
        qt_ref[hd] = qn[:, cols].T.astype(BF16)
        vt_ref[hd, :V_DIM, :] = v[:, cols].T.astype(BF16)
        vt_ref[hd, V_DIM:, :] = jnp.ones((V_AUG - V_DIM, v.shape[0]), BF16)
    gates_ref[...] = proj(c0 + 3 * wq, w_ref.shape[-1]).astype(gates_ref.dtype)


def _front(x2, bsz, g1, w, l, lru_params, sgu_params, qk_params, heads, groups, tm=512):
    n, d = x2.shape
    s = n // bsz
    nt = s // tm
    c = w.shape[-1]
    lru_w = lru_params[0].shape[-1]
    wq = heads * 2 * QK_DIM
    n_gate = c - 4 * lru_w - 3 * wq
    small = list(lru_params) + list(sgu_params) + list(qk_params)
    full = lambda a: pl.BlockSpec(a.shape, lambda b, t: (0,) * a.ndim)
    rows = lambda width: pl.BlockSpec((tm, width), lambda b, t: (b * nt + t, 0))
    tspec = lambda r: pl.BlockSpec((None, heads, r, tm), lambda b, t: (b, 0, 0, t))
    return pl.pallas_call(
        functools.partial(_front_body, heads=heads, groups=groups, lru_w=lru_w, wq=wq),
        grid=(bsz, nt),
        in_specs=[rows(d), pl.BlockSpec((1, d), lambda b, t: (0, 0)),
                  _resident((None, d, c), lambda b, t: (l, 0, 0))] + [full(a) for a in small],
        out_specs=[rows(lru_w), rows(lru_w), tspec(V_DIM),
                   pl.BlockSpec((None, tm, wq), lambda b, t: (b, t, 0)), tspec(V_AUG), rows(n_gate)],
        out_shape=[
            jax.ShapeDtypeStruct((n, lru_w), BF16),
            jax.ShapeDtypeStruct((n, lru_w), BF16),
            jax.ShapeDtypeStruct((bsz, heads, V_DIM, s), BF16),
            jax.ShapeDtypeStruct((bsz, s, wq), BF16),
            jax.ShapeDtypeStruct((bsz, heads, V_AUG, s), BF16),
            jax.ShapeDtypeStruct((n, n_gate), BF16),
        ],
        scratch_shapes=[pltpu.VMEM((tm + 8, lru_w), F32), pltpu.VMEM((1, lru_w), F32)],
        compiler_params=_cparams("parallel", "arbitrary"),
        name="front",
    )(x2, g1, w, *small)


def _attn_body(qt_ref, k_ref, vt_ref, bias_ref, lam_ref, subg_ref, o_ref, acc_ref, s_ref, *, t, nd,
               lam_init):
    qi = pl.program_id(2)
    qt = qt_ref[...]
    row = lax.broadcasted_iota(jnp.int32, qt.shape, 0)
    zero = jnp.zeros_like(qt)
    q1 = jnp.where(row < QK_DIM, qt, zero)
    q2 = jnp.where(row >= QK_DIM, qt, zero)
    acc_ref[...] = jnp.zeros_like(acc_ref)

    def scores(kj):
        k = k_ref[pl.ds(pl.multiple_of(kj * t, t), t), :]
        return (jnp.dot(k, q1, preferred_element_type=F32),
                jnp.dot(k, q2, preferred_element_type=F32))

    def colmax(s):
        return jnp.max(s, axis=0, keepdims=True)

    def consume(idx, kj, s, mc, m):
        vt = vt_ref[:, pl.ds(pl.multiple_of(kj * t, t), t)]
        mn = jnp.maximum(m, mc)
        p = jnp.exp2(s - mn).astype(BF16)
        acc_ref[idx] = acc_ref[idx] * jnp.exp2(m - mn) + jnp.dot(vt, p, preferred_element_type=F32)
        return mn

    n_far = jnp.maximum(qi - (nd - 1), 0)
    last_far = jnp.maximum(n_far - 1, 0)

    def produce(slot, kj, bias_idx=None):
        s1, s2 = scores(kj)
        if bias_idx is not None:
            b = bias_ref[bias_idx]
            s1 = s1 + b
            s2 = s2 + b
        s_ref[slot, 0] = s1
        s_ref[slot, 1] = s2
        return colmax(s1), colmax(s2)

    def consume_slot(slot, kj, c, st):
        return (consume(0, kj, s_ref[slot, 0], c[0], st[0]),
                consume(1, kj, s_ref[slot, 1], c[1], st[1]))

    neg = jnp.full((1, t), NEG_BIG, F32)
    st = (neg, neg)

    pending = None
    for d in range(nd - 1, -1, -1):
        slot = (d + 1) % 2
        kj = jnp.maximum(qi - d, 0)
        c = produce(slot, kj, jnp.where(qi >= d, d, nd))
        if pending is not None:
            st = consume_slot(*pending, st)
        pending = (slot, kj, c)

    def produce_far(slot, kj):
        return produce(slot, jnp.minimum(kj, last_far))

    def far_pair(i, carry):
        st, c0 = carry
        c1 = produce_far(1, 2 * i + 1)
        st = consume_slot(0, 2 * i, c0, st)
        c0 = produce_far(0, 2 * i + 2)
        st = consume_slot(1, 2 * i + 1, c1, st)
        return st, c0

    def far_quad(i, carry):
        return far_pair(2 * i + 1, far_pair(2 * i, carry))

    c0 = produce_far(0, 0)
    st = consume_slot(*pending, st)
    carry = lax.fori_loop(0, n_far // 4, far_quad, (st, c0))
    st, c0 = lax.cond(n_far % 4 >= 2, lambda: far_pair(n_far // 4 * 2, carry), lambda: carry)

    @pl.when(n_far % 2 == 1)
    def _():
        consume_slot(0, last_far, c0, st)

    lam4 = lam_ref[...]
    lam = (jnp.exp(jnp.sum(lam4[0:1] * lam4[1:2], axis=-1, keepdims=True))
           - jnp.exp(jnp.sum(lam4[2:3] * lam4[3:4], axis=-1, keepdims=True)) + lam_init)
    o = (acc_ref[0, :V_DIM, :] / acc_ref[0, V_DIM:V_DIM + 1, :]
         - lam * (acc_ref[1, :V_DIM, :] / acc_ref[1, V_DIM:V_DIM + 1, :]))
    ms = jnp.mean(o * o, axis=0, keepdims=True)
    o = o * lax.rsqrt(ms + EPS) * subg_ref[...] * (1.0 - lam_init)
    o_ref[...] = o.T.astype(o_ref.dtype)


def _attn(qt, kn, vt, bias, lam4, subg, lam_init, t=ATT_T):
    bsz, heads, _, s = qt.shape
    nd = bias.shape[1] - 1
    return pl.pallas_call(
        functools.partial(_attn_body, t=t, nd=nd, lam_init=lam_init),
        grid=(bsz, heads, s // t),
        in_specs=[
            pl.BlockSpec((None, None, V_DIM, t), lambda b, h, i: (b, h, 0, i)),
            pl.BlockSpec((None, s, V_DIM), lambda b, h, i: (b, 0, h)),
            pl.BlockSpec((None, None, V_AUG, s), lambda b, h, i: (b, h, 0, 0)),
            pl.BlockSpec((None, nd + 1, t, t), lambda b, h, i: (h, 0, 0, 0)),
            pl.BlockSpec(lam4.shape, lambda b, h, i: (0, 0)),
            pl.BlockSpec(subg.shape, lambda b, h, i: (0, 0)),
        ],
        out_specs=pl.BlockSpec((None, t, V_DIM), lambda b, h, i: (b, i, h)),
        out_shape=jax.ShapeDtypeStruct((bsz, s, heads * V_DIM), BF16),
        scratch_shapes=[pltpu.VMEM((2, V_AUG, t), F32), pltpu.VMEM((2, 2, t, t), F32)],
        compiler_params=_cparams("parallel", "parallel", "arbitrary"),
        name="diff_attn",
    )(qt, kn, vt, bias, lam4, subg)


def _t5_bucket(rel, n_buckets):
    half = n_buckets // 2
    max_exact = half // 2
    ret = jnp.where(rel > 0, half, 0)
    n = jnp.abs(rel)
    nf = jnp.maximum(n, 1).astype(F32)
    large = max_exact + (jnp.log(nf / max_exact) / math.log(MAX_DISTANCE / max_exact)
                         * (half - max_exact)).astype(jnp.int32)
    large = jnp.minimum(large, half - 1)
    return ret + jnp.where(n < max_exact, n, large)


def _bias_body(f_ref, o_ref, *, t, nd):
    x = jnp.broadcast_to(f_ref[...], (t, 2 * t))
    y = pltpu.roll(x, t + 1, 1, stride=1, stride_axis=0)[:, :t]
    j = lax.broadcasted_iota(jnp.int32, (t, t), 0)
    i = lax.broadcasted_iota(jnp.int32, (t, t), 1)
    delta = pl.program_id(1)
    allowed = (((j // CHUNK) <= (i // CHUNK)) | (delta > 0)) & (delta < nd)
    o_ref[...] = jnp.where(allowed, y, NEG_BIG)


def _bias_tiles(rel_bias, t, nd):
    n_buckets, heads = rel_bias.shape
    c = jnp.arange(2 * t, dtype=jnp.int32)[None, :]
    delta = jnp.arange(nd + 1, dtype=jnp.int32)[:, None]
    rel = (t - 1 - c) - delta * t
    f = (rel_bias[_t5_bucket(rel, n_buckets)] - rel_bias[n_buckets // 2 - 1]) * LOG2E
    f = jnp.transpose(f, (2, 0, 1))[:, :, None, :]
    return pl.pallas_call(
        functools.partial(_bias_body, t=t, nd=nd),
        grid=(heads, nd + 1),
        in_specs=[pl.BlockSpec((None, None, 1, 2 * t), lambda h, d: (h, d, 0, 0))],
        out_specs=pl.BlockSpec((None, None, t, t), lambda h, d: (h, d, 0, 0)),
        out_shape=jax.ShapeDtypeStruct((heads, nd + 1, t, t), F32),
        compiler_params=_cparams("parallel", "parallel"),
        name="bias_tiles",
    )(f)


def _merge_body(x_ref, ya_ref, yb_ref, yc_ref, gates_ref, bg_ref, wpa_ref, wpb_ref, wpc_ref, wo_ref,
                o_ref):
    d = x_ref.shape[-1]

    def branch(idx, y_ref, w_ref):
        p = jnp.dot(y_ref[...], w_ref[...], preferred_element_type=F32)
        g = gates_ref[:, idx * d:(idx + 1) * d].astype(F32)
        return jax.nn.sigmoid(g + bg_ref[idx]) * p

    merged = branch(0, ya_ref, wpa_ref) + branch(1, yb_ref, wpb_ref) + branch(2, yc_ref, wpc_ref)
    o_ref[...] = x_ref[...] + jnp.dot(merged.astype(BF16), wo_ref[...], preferred_element_type=F32)


def _merge(x2, ya, yb, yc, gates, bg, wpa, wpb, wpc, wo, l, tm=1024):
    n, d = x2.shape
    rows = lambda a: pl.BlockSpec((tm, a.shape[-1]), lambda i: (i, 0))
    wspec = lambda a: pl.BlockSpec((None,) + a.shape[1:], lambda i: (l, 0, 0))
    return pl.pallas_call(
        _merge_body,
        grid=(n // tm,),
        in_specs=[rows(x2), rows(ya), rows(yb), rows(yc), rows(gates),
                  pl.BlockSpec(bg.shape, lambda i: (0, 0, 0)),
                  wspec(wpa), wspec(wpb), wspec(wpc), wspec(wo)],
        out_specs=pl.BlockSpec((tm, d), lambda i: (i, 0)),
        out_shape=jax.ShapeDtypeStruct((n, d), F32),
        compiler_params=_cparams("parallel"),
        name="merge",
    )(x2, ya, yb, yc, gates, bg, wpa, wpb, wpc, wo)


def _ffn_body(x_ref, g_ref, wg_ref, wu_ref, wd_ref, o_ref):
    x = x_ref[...]
    h = _rms_norm_bf16(x, g_ref[...])
    gate = jnp.dot(h, wg_ref[...], preferred_element_type=F32)
    up = jnp.dot(h, wu_ref[...], preferred_element_type=F32)
    act = (jax.nn.silu(gate) * up).astype(BF16)
    o_ref[...] = x + jnp.dot(act, wd_ref[...], preferred_element_type=F32)


def _ffn(x2, g, wg, wu, wd, l, tm=512):
    n, d = x2.shape
    hid = wg.shape[-1]
    return pl.pallas_call(
        _ffn_body,
        grid=(n // tm,),
        in_specs=[
            pl.BlockSpec((tm, d), lambda i: (i, 0)),
            pl.BlockSpec((1, d), lambda i: (0, 0)),
            _resident((None, d, hid), lambda i: (l, 0, 0)),
            _resident((None, d, hid), lambda i: (l, 0, 0)),
            _resident((None, hid, d), lambda i: (l, 0, 0)),
        ],
        out_specs=pl.BlockSpec((tm, d), lambda i: (i, 0)),
        out_shape=jax.ShapeDtypeStruct((n, d), F32),
        compiler_params=_cparams("parallel"),
        name="ffn",
    )(x2, g, wg, wu, wd)


def _block_diag(w):
    heads, a, b = w.shape
    eye = jnp.eye(heads, dtype=w.dtype)
    return (eye[:, None, :, None] * w[:, :, None, :]).reshape(heads * a, heads * b)


def kernel(x, ln1_g, w_in, b_gate, conv_w, conv_b, lru_wa, lru_ba, lru_wi, lru_bi, lru_lambda, sg_ln_g, sg_ln_b, sg_w, sg_b, q_norm_g, k_norm_g, lambda_q1, lambda_k1, lambda_q2, lambda_k2, subln_g, rel_bias, w_pa, w_pb, w_pc, w_o, ln2_g, w_ff_gate, w_ff_up, w_ff_down):
    bsz, s, d = x.shape
    depth = w_in.shape[0]
    lru_w = conv_w.shape[-1]
    sg_wd = sg_ln_g.shape[-1]
    groups = sg_w.shape[1]
    heads = rel_bias.shape[1]
    wq = heads * 2 * QK_DIM
    assert w_in.shape[-1] == 2 * lru_w + 2 * sg_wd + 3 * wq + 3 * d and lru_w == sg_wd
    assert s % ATT_T == 0 and ATT_T % CHUNK == 0

    nd = -(-(MAX_DISTANCE // 2 + ATT_T) // ATT_T)
    bias = _bias_tiles(rel_bias, ATT_T, nd)
    bd = _block_diag(jnp.ones((wq // QK_DIM, QK_DIM, QK_DIM), BF16))
    w_in_b, w_pa_b, w_pb_b, w_pc_b, w_o_b = (w.astype(BF16) for w in (w_in, w_pa, w_pb, w_pc, w_o))
    wg_b, wu_b, wd_b = (w.astype(BF16) for w in (w_ff_gate, w_ff_up, w_ff_down))
    causal = jnp.tril(jnp.ones((SG_LEN, SG_LEN), bool))

    x2 = x.reshape(bsz * s, d)
    for l in range(depth):
        lam_init = 0.8 - 0.6 * math.exp(-0.3 * l)
        cvec = (-LRU_C * jax.nn.softplus(-lru_lambda[l]))[None]
        lru_params = (conv_w[l], conv_b[l][None], _block_diag(lru_wa[l]).astype(BF16), lru_ba[l][None],
                      _block_diag(lru_wi[l]).astype(BF16), lru_bi[l][None], cvec)
        ws = jnp.where(causal, sg_w[l], 0.0).astype(BF16)
        ws_cat = jnp.transpose(ws, (1, 0, 2)).reshape(SG_LEN, groups * SG_LEN)
        bs_full = jnp.repeat(sg_b[l].T, sg_wd // groups, axis=1)
        sgu_params = (sg_ln_g[l][None], sg_ln_b[l][None], ws_cat, bs_full)
        gq = jnp.tile(q_norm_g[l] * (QK_DIM ** -0.5 * LOG2E), wq // QK_DIM)[None]
        gk = jnp.tile(k_norm_g[l], wq // QK_DIM)[None]
        ya, yb, qt, kn, vt, gates = _front(x2, bsz, ln1_g[l][None], w_in_b, l, lru_params, sgu_params,
                                           (gq, gk, bd), heads, groups)
        lam4 = jnp.stack([lambda_q1[l], lambda_k1[l], lambda_q2[l], lambda_k2[l]])
        subg = jnp.broadcast_to(subln_g[l][:, None], (V_DIM, ATT_T))
        yc = _attn(qt, kn, vt, bias, lam4, subg, lam_init)

        x2 = _merge(x2, ya, yb, yc.reshape(bsz * s, -1), gates, b_gate[l].reshape(3, 1, d),
                    w_pa_b, w_pb_b, w_pc_b, w_o_b, l)
        x2 = _ffn(x2, ln2_g[l][None], wg_b, wu_b, wd_b, l)
    return x2.reshape(bsz, s, d)
```

```python
import functools
import math

import jax
import jax.numpy as jnp
from jax import lax
from jax.experimental import pallas as pl
from jax.experimental.pallas import tpu as pltpu

F32 = jnp.float32
BF16 = jnp.bfloat16

EPS = 1e-6
CHUNK = 64
LRU_C = 8.0
SG_LEN = 128
QK_DIM = 64
V_DIM = 128
V_AUG = V_DIM + 16
MAX_DISTANCE = 2048
NEG_BIG = -1e30
LOG2E = math.log2(math.e)

ATT_T = 512
VMEM_LIMIT = 56 * 1024 * 1024


def _cparams(*sem):
    return pltpu.CompilerParams(dimension_semantics=sem, vmem_limit_bytes=VMEM_LIMIT)


def _rms_norm_bf16(x, g):
    ms = jnp.mean(x * x, axis=-1, keepdims=True)
    return (x * lax.rsqrt(ms + EPS) * g).astype(BF16)


def _resident(shape, index_map):
    return pl.BlockSpec(shape, index_map, pipeline_mode=pl.Buffered(1))


def _lru_branch(xa, ga, cw_ref, cb_ref, wa_ref, ba_ref, wi_ref, bi_ref, c_ref, ext_ref, h_ref):
    ts, w = xa.shape
    ext_ref[8:8 + ts, :] = xa
    xc = (cb_ref[...] + ext_ref[5:5 + ts, :] * cw_ref[0:1, :] + ext_ref[6:6 + ts, :] * cw_ref[1:2, :]
          + ext_ref[7:7 + ts, :] * cw_ref[2:3, :] + xa * cw_ref[3:4, :])
    ext_ref[0:8, :] = ext_ref[ts:ts + 8, :]

    xcb = xc.astype(BF16)
    r = jax.nn.sigmoid(jnp.dot(xcb, wa_ref[...], preferred_element_type=F32) + ba_ref[...])
    i = jax.nn.sigmoid(jnp.dot(xcb, wi_ref[...], preferred_element_type=F32) + bi_ref[...])
    log_a = c_ref[...] * r
    a = jnp.exp(log_a)
    mult = jnp.sqrt(-jnp.tanh(log_a) * (a * a + 1.0))
    b = mult * (i * xc)

    sub = 8
    a3 = a.reshape(ts // sub, sub, w)
    b3 = b.reshape(ts // sub, sub, w)
    row = lax.broadcasted_iota(jnp.int32, a3.shape, 1)
    d = 1
    while d < sub:
        keep = row >= d
        a_sh = jnp.where(keep, pltpu.roll(a3, d, 1), 1.0)
        b_sh = jnp.where(keep, pltpu.roll(b3, d, 1), 0.0)
        b3 = a3 * b_sh + b3
        a3 = a3 * a_sh
        d *= 2
    carry = h_ref[...]
    groups = []
    for g in range(ts // sub):
        hg = a3[g] * carry + b3[g]
        groups.append(hg)
        carry = hg[sub - 1:sub, :]
    h_ref[...] = carry
    h = jnp.concatenate(groups, axis=0)
    return h * jax.nn.gelu(ga)


def _sgu_branch(u, v, g_ref, b_ref, ws_ref, bs_ref, groups):
    tr, w = v.shape
    gd = w // groups
    mu = jnp.mean(v, axis=-1, keepdims=True)
    vc = v - mu
    var = jnp.mean(vc * vc, axis=-1, keepdims=True)
    vn = (vc * lax.rsqrt(var + EPS) * g_ref[...] + b_ref[...]).astype(BF16)
    lane_grp = lax.broadcasted_iota(jnp.int32, (SG_LEN, w), 1) // gd
    ws = ws_ref[...]
    zero = jnp.zeros((SG_LEN, w), BF16)
    out = []
    for blk in range(tr // SG_LEN):
        vb = vn[blk * SG_LEN:(blk + 1) * SG_LEN]
        stacked = jnp.concatenate([jnp.where(lane_grp == g, vb, zero) for g in range(groups)], axis=0)
        out.append(jnp.dot(ws, stacked, preferred_element_type=F32) + bs_ref[...])
    return u * jnp.concatenate(out, axis=0)


def _group_rms(y, g_ref, bd_ref):
    ss = jnp.dot((y * y).astype(BF16), bd_ref[...], preferred_element_type=F32)
    return y * lax.rsqrt(ss * (1.0 / QK_DIM) + EPS) * g_ref[...]


def _front_body(x_ref, g1_ref, w_ref, cw_ref, cb_ref, wa_ref, ba_ref, wi_ref, bi_ref, c_ref,
                lng_ref, lnb_ref, ws_ref, bs_ref, gq_ref, gk_ref, bd_ref,
                ya_ref, yb_ref, qt_ref, kn_ref, vt_ref, gates_ref, ext_ref, h_ref, *,
                heads, groups, lru_w, wq):
    @pl.when(pl.program_id(1) == 0)
    def _():
        ext_ref[0:8, :] = jnp.zeros((8, lru_w), F32)
        h_ref[...] = jnp.zeros_like(h_ref)

    h = _rms_norm_bf16(x_ref[...], g1_ref[...])

    def proj(c0, c1):
        return jnp.dot(h, w_ref[:, c0:c1], preferred_element_type=F32)

    c0 = 4 * lru_w
    pa = proj(0, 2 * lru_w)
    ya_ref[...] = _lru_branch(pa[:, :lru_w], pa[:, lru_w:], cw_ref, cb_ref, wa_ref, ba_ref, wi_ref,
                              bi_ref, c_ref, ext_ref, h_ref).astype(ya_ref.dtype)
    pb = proj(2 * lru_w, 4 * lru_w)
    yb_ref[...] = _sgu_branch(pb[:, :lru_w], pb[:, lru_w:], lng_ref, lnb_ref, ws_ref, bs_ref,
                              groups).astype(yb_ref.dtype)
    qn = _group_rms(proj(c0, c0 + wq), gq_ref, bd_ref)
    kn_ref[...] = _group_rms(proj(c0 + wq, c0 + 2 * wq), gk_ref, bd_ref).astype(BF16)
    v = proj(c0 + 2 * wq, c0 + 3 * wq)
    for hd in range(heads):
        cols = slice(hd * V_DIM, (hd + 1) * V_DIM)
        qt_ref[hd] = qn[:, cols].T.astype(BF16)
        vt_ref[hd, :V_DIM, :] = v[:, cols].T.astype(BF16)
        vt_ref[hd, V_DIM:, :] = jnp.ones((V_AUG - V_DIM, v.shape[0]), BF16)
    gates_ref[...] = proj(c0 + 3 * wq, w_ref.shape[-1]).astype(gates_ref.dtype)


def _front(x2, bsz, g1, w, l, lru_params, sgu_params, qk_params, heads, groups, tm=512):
    n, d = x2.shape
    s = n // bsz
    nt = s // tm
    c = w.shape[-1]
    lru_w = lru_params[0].shape[-1]
    wq = heads * 2 * QK_DIM
    n_gate = c - 4 * lru_w - 3 * wq
    small = list(lru_params) + list(sgu_params) + list(qk_params)
    full = lambda a: pl.BlockSpec(a.shape, lambda b, t: (0,) * a.ndim)
    rows = lambda width: pl.BlockSpec((tm, width), lambda b, t: (b * nt + t, 0))
    tspec = lambda r: pl.BlockSpec((None, heads, r, tm), lambda b, t: (b, 0, 0, t))
    return pl.pallas_call(
        functools.partial(_front_body, heads=heads, groups=groups, lru_w=lru_w, wq=wq),
        grid=(bsz, nt),
        in_specs=[rows(d), pl.BlockSpec((1, d), lambda b, t: (0, 0)),
                  _resident((None, d, c), lambda b, t: (l, 0, 0))] + [full(a) for a in small],
        out_specs=[rows(lru_w), rows(lru_w), tspec(V_DIM),
                   pl.BlockSpec((None, tm, wq), lambda b, t: (b, t, 0)), tspec(V_AUG), rows(n_gate)],
        out_shape=[
            jax.ShapeDtypeStruct((n, lru_w), BF16),
            jax.ShapeDtypeStruct((n, lru_w), BF16),
            jax.ShapeDtypeStruct((bsz, heads, V_DIM, s), BF16),
            jax.ShapeDtypeStruct((bsz, s, wq), BF16),
            jax.ShapeDtypeStruct((bsz, heads, V_AUG, s), BF16),
            jax.ShapeDtypeStruct((n, n_gate), BF16),
        ],
        scratch_shapes=[pltpu.VMEM((tm + 8, lru_w), F32), pltpu.VMEM((1, lru_w), F32)],
        compiler_params=_cparams("parallel", "arbitrary"),
        name="front",
    )(x2, g1, w, *small)


def _attn_body(qt_ref, k_ref, vt_ref, bias_ref, lam_ref, subg_ref, o_ref, acc_ref, s_ref, *, t, nd,
               lam_init):
    qi = pl.program_id(2)
    qt = qt_ref[...]
    row = lax.broadcasted_iota(jnp.int32, qt.shape, 0)
    zero = jnp.zeros_like(qt)
    q1 = jnp.where(row < QK_DIM, qt, zero)
    q2 = jnp.where(row >= QK_DIM, qt, zero)
    acc_ref[...] = jnp.zeros_like(acc_ref)

    def scores(kj):
        k = k_ref[pl.ds(pl.multiple_of(kj * t, t), t), :]
        return (jnp.dot(k, q1, preferred_element_type=F32),
                jnp.dot(k, q2, preferred_element_type=F32))

    def colmax(s):
        return jnp.max(s, axis=0, keepdims=True)

    def consume(idx, kj, s, mc, m):
        vt = vt_ref[:, pl.ds(pl.multiple_of(kj * t, t), t)]
        mn = jnp.maximum(m, mc)
        p = jnp.exp2(s - mn).astype(BF16)
        acc_ref[idx] = acc_ref[idx] * jnp.exp2(m - mn) + jnp.dot(vt, p, preferred_element_type=F32)
        return mn

    n_far = jnp.maximum(qi - (nd - 1), 0)
    last_far = jnp.maximum(n_far - 1, 0)

    def produce(slot, kj, bias_idx=None):
        s1, s2 = scores(kj)
        if bias_idx is not None:
            b = bias_ref[bias_idx]
            s1 = s1 + b
            s2 = s2 + b
        s_ref[slot, 0, :, :t] = s1
        s_ref[slot, 1, :, :t] = s2
        return colmax(s1), colmax(s2)

    def consume_slot(slot, kj, c, st):
        return (consume(0, kj, s_ref[slot, 0, :, :t], c[0], st[0]),
                consume(1, kj, s_ref[slot, 1, :, :t], c[1], st[1]))

    neg = jnp.full((1, t), NEG_BIG, F32)
    st = (neg, neg)

    pending = None
    for d in range(nd - 1, -1, -1):
        slot = (d + 1) % 2
        kj = jnp.maximum(qi - d, 0)
        c = produce(slot, kj, jnp.where(qi >= d, d, nd))
        if pending is not None:
            st = consume_slot(*pending, st)
        pending = (slot, kj, c)

    def produce_far(slot, kj):
        return produce(slot, jnp.minimum(kj, last_far))

    def far_pair(i, carry):
        st, c0 = carry
        c1 = produce_far(1, 2 * i + 1)
        st = consume_slot(0, 2 * i, c0, st)
        c0 = produce_far(0, 2 * i + 2)
        st = consume_slot(1, 2 * i + 1, c1, st)
        return st, c0

    def far_quad(i, carry):
        return far_pair(2 * i + 1, far_pair(2 * i, carry))

    c0 = produce_far(0, 0)
    st = consume_slot(*pending, st)
    carry = lax.fori_loop(0, n_far // 4, far_quad, (st, c0))
    st, c0 = lax.cond(n_far % 4 >= 2, lambda: far_pair(n_far // 4 * 2, carry), lambda: carry)

    @pl.when(n_far % 2 == 1)
    def _():
        consume_slot(0, last_far, c0, st)

    lam4 = lam_ref[...]
    lam = (jnp.exp(jnp.sum(lam4[0:1] * lam4[1:2], axis=-1, keepdims=True))
           - jnp.exp(jnp.sum(lam4[2:3] * lam4[3:4], axis=-1, keepdims=True)) + lam_init)
    o = (acc_ref[0, :V_DIM, :] / acc_ref[0, V_DIM:V_DIM + 1, :]
         - lam * (acc_ref[1, :V_DIM, :] / acc_ref[1, V_DIM:V_DIM + 1, :]))
    ms = jnp.mean(o * o, axis=0, keepdims=True)
    o = o * lax.rsqrt(ms + EPS) * subg_ref[...] * (1.0 - lam_init)
    o_ref[...] = o.T.astype(o_ref.dtype)


def _attn(qt, kn, vt, bias, lam4, subg, lam_init, t=ATT_T):
    bsz, heads, _, s = qt.shape
    nd = bias.shape[1] - 1
    return pl.pallas_call(
        functools.partial(_attn_body, t=t, nd=nd, lam_init=lam_init),
        grid=(bsz, heads, s // t),
        in_specs=[
            pl.BlockSpec((None, None, V_DIM, t), lambda b, h, i: (b, h, 0, i)),
            pl.BlockSpec((None, s, V_DIM), lambda b, h, i: (b, 0, h)),
            pl.BlockSpec((None, None, V_AUG, s), lambda b, h, i: (b, h, 0, 0)),
            pl.BlockSpec((None, nd + 1, t, t), lambda b, h, i: (h, 0, 0, 0)),
            pl.BlockSpec(lam4.shape, lambda b, h, i: (0, 0)),
            pl.BlockSpec(subg.shape, lambda b, h, i: (0, 0)),
        ],
        out_specs=pl.BlockSpec((None, t, V_DIM), lambda b, h, i: (b, i, h)),
        out_shape=jax.ShapeDtypeStruct((bsz, s, heads * V_DIM), BF16),
        scratch_shapes=[pltpu.VMEM((2, V_AUG, t), F32), pltpu.VMEM((2, 2, t, t + 128), F32)],
        compiler_params=_cparams("parallel", "parallel", "arbitrary"),
        name="diff_attn",
    )(qt, kn, vt, bias, lam4, subg)


def _t5_bucket(rel, n_buckets):
    half = n_buckets // 2
    max_exact = half // 2
    ret = jnp.where(rel > 0, half, 0)
    n = jnp.abs(rel)
    nf = jnp.maximum(n, 1).astype(F32)
    large = max_exact + (jnp.log(nf / max_exact) / math.log(MAX_DISTANCE / max_exact)
                         * (half - max_exact)).astype(jnp.int32)
    large = jnp.minimum(large, half - 1)
    return ret + jnp.where(n < max_exact, n, large)


def _bias_body(f_ref, o_ref, *, t, nd):
    x = jnp.broadcast_to(f_ref[...], (t, 2 * t))
    y = pltpu.roll(x, t + 1, 1, stride=1, stride_axis=0)[:, :t]
    j = lax.broadcasted_iota(jnp.int32, (t, t), 0)
    i = lax.broadcasted_iota(jnp.int32, (t, t), 1)
    delta = pl.program_id(1)
    allowed = (((j // CHUNK) <= (i // CHUNK)) | (delta > 0)) & (delta < nd)
    o_ref[...] = jnp.where(allowed, y, NEG_BIG)


def _bias_tiles(rel_bias, t, nd):
    n_buckets, heads = rel_bias.shape
    c = jnp.arange(2 * t, dtype=jnp.int32)[None, :]
    delta = jnp.arange(nd + 1, dtype=jnp.int32)[:, None]
    rel = (t - 1 - c) - delta * t
    f = (rel_bias[_t5_bucket(rel, n_buckets)] - rel_bias[n_buckets // 2 - 1]) * LOG2E
    f = jnp.transpose(f, (2, 0, 1))[:, :, None, :]
    return pl.pallas_call(
        functools.partial(_bias_body, t=t, nd=nd),
        grid=(heads, nd + 1),
        in_specs=[pl.BlockSpec((None, None, 1, 2 * t), lambda h, d: (h, d, 0, 0))],
        out_specs=pl.BlockSpec((None, None, t, t), lambda h, d: (h, d, 0, 0)),
        out_shape=jax.ShapeDtypeStruct((heads, nd + 1, t, t), F32),
        compiler_params=_cparams("parallel", "parallel"),
        name="bias_tiles",
    )(f)


def _merge_body(x_ref, ya_ref, yb_ref, yc_ref, gates_ref, bg_ref, wpa_ref, wpb_ref, wpc_ref, wo_ref,
                o_ref):
    d = x_ref.shape[-1]

    def branch(idx, y_ref, w_ref):
        p = jnp.dot(y_ref[...], w_ref[...], preferred_element_type=F32)
        g = gates_ref[:, idx * d:(idx + 1) * d].astype(F32)
        return jax.nn.sigmoid(g + bg_ref[idx]) * p

    merged = branch(0, ya_ref, wpa_ref) + branch(1, yb_ref, wpb_ref) + branch(2, yc_ref, wpc_ref)
    o_ref[...] = x_ref[...] + jnp.dot(merged.astype(BF16), wo_ref[...], preferred_element_type=F32)


def _merge(x2, ya, yb, yc, gates, bg, wpa, wpb, wpc, wo, l, tm=1024):
    n, d = x2.shape
    rows = lambda a: pl.BlockSpec((tm, a.shape[-1]), lambda i: (i, 0))
    wspec = lambda a: pl.BlockSpec((None,) + a.shape[1:], lambda i: (l, 0, 0))
    return pl.pallas_call(
        _merge_body,
        grid=(n // tm,),
        in_specs=[rows(x2), rows(ya), rows(yb), rows(yc), rows(gates),
                  pl.BlockSpec(bg.shape, lambda i: (0, 0, 0)),
                  wspec(wpa), wspec(wpb), wspec(wpc), wspec(wo)],
        out_specs=pl.BlockSpec((tm, d), lambda i: (i, 0)),
        out_shape=jax.ShapeDtypeStruct((n, d), F32),
        compiler_params=_cparams("parallel"),
        name="merge",
    )(x2, ya, yb, yc, gates, bg, wpa, wpb, wpc, wo)


def _ffn_body(x_ref, g_ref, wg_ref, wu_ref, wd_ref, o_ref):
    x = x_ref[...]
    h = _rms_norm_bf16(x, g_ref[...])
    gate = jnp.dot(h, wg_ref[...], preferred_element_type=F32)
    up = jnp.dot(h, wu_ref[...], preferred_element_type=F32)
    act = (jax.nn.silu(gate) * up).astype(BF16)
    o_ref[...] = x + jnp.dot(act, wd_ref[...], preferred_element_type=F32)


def _ffn(x2, g, wg, wu, wd, l, tm=512):
    n, d = x2.shape
    hid = wg.shape[-1]
    return pl.pallas_call(
        _ffn_body,
        grid=(n // tm,),
        in_specs=[
            pl.BlockSpec((tm, d), lambda i: (i, 0)),
            pl.BlockSpec((1, d), lambda i: (0, 0)),
            _resident((None, d, hid), lambda i: (l, 0, 0)),
            _resident((None, d, hid), lambda i: (l, 0, 0)),
            _resident((None, hid, d), lambda i: (l, 0, 0)),
        ],
        out_specs=pl.BlockSpec((tm, d), lambda i: (i, 0)),
        out_shape=jax.ShapeDtypeStruct((n, d), F32),
        compiler_params=_cparams("parallel"),
        name="ffn",
    )(x2, g, wg, wu, wd)


def _block_diag(w):
    heads, a, b = w.shape
    eye = jnp.eye(heads, dtype=w.dtype)
    return (eye[:, None, :, None] * w[:, :, None, :]).reshape(heads * a, heads * b)


def kernel(x, ln1_g, w_in, b_gate, conv_w, conv_b, lru_wa, lru_ba, lru_wi, lru_bi, lru_lambda, sg_ln_g, sg_ln_b, sg_w, sg_b, q_norm_g, k_norm_g, lambda_q1, lambda_k1, lambda_q2, lambda_k2, subln_g, rel_bias, w_pa, w_pb, w_pc, w_o, ln2_g, w_ff_gate, w_ff_up, w_ff_down):
    bsz, s, d = x.shape
    depth = w_in.shape[0]
    lru_w = conv_w.shape[-1]
    sg_wd = sg_ln_g.shape[-1]
    groups = sg_w.shape[1]
    heads = rel_bias.shape[1]
    wq = heads * 2 * QK_DIM
    assert w_in.shape[-1] == 2 * lru_w + 2 * sg_wd + 3 * wq + 3 * d and lru_w == sg_wd
    assert s % ATT_T == 0 and ATT_T % CHUNK == 0

    nd = -(-(MAX_DISTANCE // 2 + ATT_T) // ATT_T)
    bias = _bias_tiles(rel_bias, ATT_T, nd)
    bd = _block_diag(jnp.ones((wq // QK_DIM, QK_DIM, QK_DIM), BF16))
    w_in_b, w_pa_b, w_pb_b, w_pc_b, w_o_b = (w.astype(BF16) for w in (w_in, w_pa, w_pb, w_pc, w_o))
    wg_b, wu_b, wd_b = (w.astype(BF16) for w in (w_ff_gate, w_ff_up, w_ff_down))
    causal = jnp.tril(jnp.ones((SG_LEN, SG_LEN), bool))

    x2 = x.reshape(bsz * s, d)
    for l in range(depth):
        lam_init = 0.8 - 0.6 * math.exp(-0.3 * l)
        cvec = (-LRU_C * jax.nn.softplus(-lru_lambda[l]))[None]
        lru_params = (conv_w[l], conv_b[l][None], _block_diag(lru_wa[l]).astype(BF16), lru_ba[l][None],
                      _block_diag(lru_wi[l]).astype(BF16), lru_bi[l][None], cvec)
        ws = jnp.where(causal, sg_w[l], 0.0).astype(BF16)
        ws_cat = jnp.transpose(ws, (1, 0, 2)).reshape(SG_LEN, groups * SG_LEN)
        bs_full = jnp.repeat(sg_b[l].T, sg_wd // groups, axis=1)
        sgu_params = (sg_ln_g[l][None], sg_ln_b[l][None], ws_cat, bs_full)
        gq = jnp.tile(q_norm_g[l] * (QK_DIM ** -0.5 * LOG2E), wq // QK_DIM)[None]
        gk = jnp.tile(k_norm_g[l], wq // QK_DIM)[None]
        ya, yb, qt, kn, vt, gates = _front(x2, bsz, ln1_g[l][None], w_in_b, l, lru_params, sgu_params,
                                           (gq, gk, bd), heads, groups)
        lam4 = jnp.stack([lambda_q1[l], lambda_k1[l], lambda_q2[l], lambda_k2[l]])
        subg = jnp.broadcast_to(subln_g[l][:, None], (V_DIM, ATT_T))
        yc = _attn(qt, kn, vt, bias, lam4, subg, lam_init)

        x2 = _merge(x2, ya, yb, yc.reshape(bsz * s, -1), gates, b_gate[l].reshape(3, 1, d),
                    w_pa_b, w_pb_b, w_pc_b, w_o_b, l)
        x2 = _ffn(x2, ln2_g[l][None], wg_b, wu_b, wd_b, l)
    return x2.reshape(bsz, s, d)
```

```python
import functools
import math

import jax
import jax.numpy as jnp
from jax import lax
from jax.experimental import pallas as pl
from jax.experimental.pallas import tpu as pltpu

F32 = jnp.float32
BF16 = jnp.bfloat16

EPS = 1e-6
CHUNK = 64
LRU_C = 8.0
SG_LEN = 128
QK_DIM = 64
V_DIM = 128
V_AUG = V_DIM + 16
MAX_DISTANCE = 2048
NEG_BIG = -1e30
LOG2E = math.log2(math.e)

ATT_T = 512
VMEM_LIMIT = 56 * 1024 * 1024


def _cparams(*sem):
    return pltpu.CompilerParams(dimension_semantics=sem, vmem_limit_bytes=VMEM_LIMIT)


def _rms_norm_bf16(x, g):
    ms = jnp.mean(x * x, axis=-1, keepdims=True)
    return (x * lax.rsqrt(ms + EPS) * g).astype(BF16)


def _resident(shape, index_map):
    return pl.BlockSpec(shape, index_map, pipeline_mode=pl.Buffered(1))


def _lru_branch(xa, ga, cw_ref, cb_ref, wa_ref, ba_ref, wi_ref, bi_ref, c_ref, ext_ref, h_ref):
    ts, w = xa.shape
    ext_ref[8:8 + ts, :] = xa
    xc = (cb_ref[...] + ext_ref[5:5 + ts, :] * cw_ref[0:1, :] + ext_ref[6:6 + ts, :] * cw_ref[1:2, :]
          + ext_ref[7:7 + ts, :] * cw_ref[2:3, :] + xa * cw_ref[3:4, :])
    ext_ref[0:8, :] = ext_ref[ts:ts + 8, :]

    xcb = xc.astype(BF16)
    r = jax.nn.sigmoid(jnp.dot(xcb, wa_ref[...], preferred_element_type=F32) + ba_ref[...])
    i = jax.nn.sigmoid(jnp.dot(xcb, wi_ref[...], preferred_element_type=F32) + bi_ref[...])
    log_a = c_ref[...] * r
    a = jnp.exp(log_a)
    mult = jnp.sqrt(-jnp.tanh(log_a) * (a * a + 1.0))
    b = mult * (i * xc)

    sub = 8
    a3 = a.reshape(ts // sub, sub, w)
    b3 = b.reshape(ts // sub, sub, w)
    row = lax.broadcasted_iota(jnp.int32, a3.shape, 1)
    d = 1
    while d < sub:
        keep = row >= d
        a_sh = jnp.where(keep, pltpu.roll(a3, d, 1), 1.0)
        b_sh = jnp.where(keep, pltpu.roll(b3, d, 1), 0.0)
        b3 = a3 * b_sh + b3
        a3 = a3 * a_sh
        d *= 2
    carry = h_ref[...]
    groups = []
    for g in range(ts // sub):
        hg = a3[g] * carry + b3[g]
        groups.append(hg)
        carry = hg[sub - 1:sub, :]
    h_ref[...] = carry
    h = jnp.concatenate(groups, axis=0)
    return h * jax.nn.gelu(ga)


def _sgu_branch(u, v, g_ref, b_ref, ws_ref, bs_ref, groups):
    tr, w = v.shape
    gd = w // groups
    mu = jnp.mean(v, axis=-1, keepdims=True)
    vc = v - mu
    var = jnp.mean(vc * vc, axis=-1, keepdims=True)
    vn = (vc * lax.rsqrt(var + EPS) * g_ref[...] + b_ref[...]).astype(BF16)
    lane_grp = lax.broadcasted_iota(jnp.int32, (SG_LEN, w), 1) // gd
    ws = ws_ref[...]
    zero = jnp.zeros((SG_LEN, w), BF16)
    out = []
    for blk in range(tr // SG_LEN):
        vb = vn[blk * SG_LEN:(blk + 1) * SG_LEN]
        stacked = jnp.concatenate([jnp.where(lane_grp == g, vb, zero) for g in range(groups)], axis=0)
        out.append(jnp.dot(ws, stacked, preferred_element_type=F32) + bs_ref[...])
    return u * jnp.concatenate(out, axis=0)


def _group_rms(y, g_ref, bd_ref):
    ss = jnp.dot((y * y).astype(BF16), bd_ref[...], preferred_element_type=F32)
    return y * lax.rsqrt(ss * (1.0 / QK_DIM) + EPS) * g_ref[...]


def _front_body(x_ref, g1_ref, w_ref, cw_ref, cb_ref, wa_ref, ba_ref, wi_ref, bi_ref, c_ref,
                lng_ref, lnb_ref, ws_ref, bs_ref, gq_ref, gk_ref, bd_ref,
                ya_ref, yb_ref, qt_ref, kn_ref, vt_ref, ext_ref, h_ref, *,
                heads, groups, lru_w, wq):
    @pl.when(pl.program_id(1) == 0)
    def _():
        ext_ref[0:8, :] = jnp.zeros((8, lru_w), F32)
        h_ref[...] = jnp.zeros_like(h_ref)

    h = _rms_norm_bf16(x_ref[...], g1_ref[...])

    def proj(c0, c1):
        return jnp.dot(h, w_ref[:, c0:c1], preferred_element_type=F32)

    c0 = 4 * lru_w
    pa = proj(0, 2 * lru_w)
    ya_ref[...] = _lru_branch(pa[:, :lru_w], pa[:, lru_w:], cw_ref, cb_ref, wa_ref, ba_ref, wi_ref,
                              bi_ref, c_ref, ext_ref, h_ref).astype(ya_ref.dtype)
    pb = proj(2 * lru_w, 4 * lru_w)
    yb_ref[...] = _sgu_branch(pb[:, :lru_w], pb[:, lru_w:], lng_ref, lnb_ref, ws_ref, bs_ref,
                              groups).astype(yb_ref.dtype)
    qn = _group_rms(proj(c0, c0 + wq), gq_ref, bd_ref)
    kn_ref[...] = _group_rms(proj(c0 + wq, c0 + 2 * wq), gk_ref, bd_ref).astype(BF16)
    v = proj(c0 + 2 * wq, c0 + 3 * wq)
    for hd in range(heads):
        cols = slice(hd * V_DIM, (hd + 1) * V_DIM)
        qt_ref[hd] = qn[:, cols].T.astype(BF16)
        vt_ref[hd, :V_DIM, :] = v[:, cols].T.astype(BF16)
        vt_ref[hd, V_DIM:, :] = jnp.ones((V_AUG - V_DIM, v.shape[0]), BF16)


def _front(x2, bsz, g1, w, l, lru_params, sgu_params, qk_params, heads, groups, tm=512):
    n, d = x2.shape
    s = n // bsz
    nt = s // tm
    c = w.shape[-1]
    lru_w = lru_params[0].shape[-1]
    wq = heads * 2 * QK_DIM
    small = list(lru_params) + list(sgu_params) + list(qk_params)
    full = lambda a: pl.BlockSpec(a.shape, lambda b, t: (0,) * a.ndim)
    rows = lambda width: pl.BlockSpec((tm, width), lambda b, t: (b * nt + t, 0))
    tspec = lambda r: pl.BlockSpec((None, heads, r, tm), lambda b, t: (b, 0, 0, t))
    return pl.pallas_call(
        functools.partial(_front_body, heads=heads, groups=groups, lru_w=lru_w, wq=wq),
        grid=(bsz, nt),
        in_specs=[rows(d), pl.BlockSpec((1, d), lambda b, t: (0, 0)),
                  _resident((None, d, c), lambda b, t: (l, 0, 0))] + [full(a) for a in small],
        out_specs=[rows(lru_w), rows(lru_w), tspec(V_DIM),
                   pl.BlockSpec((None, tm, wq), lambda b, t: (b, t, 0)), tspec(V_AUG)],
        out_shape=[
            jax.ShapeDtypeStruct((n, lru_w), BF16),
            jax.ShapeDtypeStruct((n, lru_w), BF16),
            jax.ShapeDtypeStruct((bsz, heads, V_DIM, s), BF16),
            jax.ShapeDtypeStruct((bsz, s, wq), BF16),
            jax.ShapeDtypeStruct((bsz, heads, V_AUG, s), BF16),
        ],
        scratch_shapes=[pltpu.VMEM((tm + 8, lru_w), F32), pltpu.VMEM((1, lru_w), F32)],
        compiler_params=_cparams("parallel", "arbitrary"),
        name="front",
    )(x2, g1, w, *small)


def _attn_body(qt_ref, k_ref, vt_ref, bias_ref, lam_ref, subg_ref, o_ref, acc_ref, s_ref, *, t, nd,
               lam_init):
    qi = pl.program_id(2)
    qt = qt_ref[...]
    row = lax.broadcasted_iota(jnp.int32, qt.shape, 0)
    zero = jnp.zeros_like(qt)
    q1 = jnp.where(row < QK_DIM, qt, zero)
    q2 = jnp.where(row >= QK_DIM, qt, zero)
    acc_ref[...] = jnp.zeros_like(acc_ref)

    def scores(kj):
        k = k_ref[pl.ds(pl.multiple_of(kj * t, t), t), :]
        return (jnp.dot(k, q1, preferred_element_type=F32),
                jnp.dot(k, q2, preferred_element_type=F32))

    def colmax(s):
        return jnp.max(s, axis=0, keepdims=True)

    def consume(idx, kj, s, mc, m):
        vt = vt_ref[:, pl.ds(pl.multiple_of(kj * t, t), t)]
        mn = jnp.maximum(m, mc)
        p = jnp.exp2(s - mn).astype(BF16)
        acc_ref[idx] = acc_ref[idx] * jnp.exp2(m - mn) + jnp.dot(vt, p, preferred_element_type=F32)
        return mn

    n_far = jnp.maximum(qi - (nd - 1), 0)
    last_far = jnp.maximum(n_far - 1, 0)

    def produce(slot, kj, bias_idx=None):
        s1, s2 = scores(kj)
        if bias_idx is not None:
            b = bias_ref[bias_idx]
            s1 = s1 + b
            s2 = s2 + b
        s_ref[slot, 0, :, :t] = s1
        s_ref[slot, 1, :, :t] = s2
        return colmax(s1), colmax(s2)

    def consume_slot(slot, kj, c, st):
        return (consume(0, kj, s_ref[slot, 0, :, :t], c[0], st[0]),
                consume(1, kj, s_ref[slot, 1, :, :t], c[1], st[1]))

    neg = jnp.full((1, t), NEG_BIG, F32)
    st = (neg, neg)

    pending = None
    for d in range(nd - 1, -1, -1):
        slot = (d + 1) % 2
        kj = jnp.maximum(qi - d, 0)
        c = produce(slot, kj, jnp.where(qi >= d, d, nd))
        if pending is not None:
            st = consume_slot(*pending, st)
        pending = (slot, kj, c)

    def produce_far(slot, kj):
        return produce(slot, jnp.minimum(kj, last_far))

    def far_pair(i, carry):
        st, c0 = carry
        c1 = produce_far(1, 2 * i + 1)
        st = consume_slot(0, 2 * i, c0, st)
        c0 = produce_far(0, 2 * i + 2)
        st = consume_slot(1, 2 * i + 1, c1, st)
        return st, c0

    def far_quad(i, carry):
        return far_pair(2 * i + 1, far_pair(2 * i, carry))

    c0 = produce_far(0, 0)
    st = consume_slot(*pending, st)
    carry = lax.fori_loop(0, n_far // 4, far_quad, (st, c0))
    st, c0 = lax.cond(n_far % 4 >= 2, lambda: far_pair(n_far // 4 * 2, carry), lambda: carry)

    @pl.when(n_far % 2 == 1)
    def _():
        consume_slot(0, last_far, c0, st)

    lam4 = lam_ref[...]
    lam = (jnp.exp(jnp.sum(lam4[0:1] * lam4[1:2], axis=-1, keepdims=True))
           - jnp.exp(jnp.sum(lam4[2:3] * lam4[3:4], axis=-1, keepdims=True)) + lam_init)
    o = (acc_ref[0, :V_DIM, :] / acc_ref[0, V_DIM:V_DIM + 1, :]
         - lam * (acc_ref[1, :V_DIM, :] / acc_ref[1, V_DIM:V_DIM + 1, :]))
    ms = jnp.mean(o * o, axis=0, keepdims=True)
    o = o * lax.rsqrt(ms + EPS) * subg_ref[...] * (1.0 - lam_init)
    o_ref[...] = o.T.astype(o_ref.dtype)


def _attn(qt, kn, vt, bias, lam4, subg, lam_init, t=ATT_T):
    bsz, heads, _, s = qt.shape
    nd = bias.shape[1] - 1
    return pl.pallas_call(
        functools.partial(_attn_body, t=t, nd=nd, lam_init=lam_init),
        grid=(bsz, heads, s // t),
        in_specs=[
            pl.BlockSpec((None, None, V_DIM, t), lambda b, h, i: (b, h, 0, i)),
            pl.BlockSpec((None, s, V_DIM), lambda b, h, i: (b, 0, h)),
            pl.BlockSpec((None, None, V_AUG, s), lambda b, h, i: (b, h, 0, 0)),
            pl.BlockSpec((None, nd + 1, t, t), lambda b, h, i: (h, 0, 0, 0)),
            pl.BlockSpec(lam4.shape, lambda b, h, i: (0, 0)),
            pl.BlockSpec(subg.shape, lambda b, h, i: (0, 0)),
        ],
        out_specs=pl.BlockSpec((None, t, V_DIM), lambda b, h, i: (b, i, h)),
        out_shape=jax.ShapeDtypeStruct((bsz, s, heads * V_DIM), BF16),
        scratch_shapes=[pltpu.VMEM((2, V_AUG, t), F32), pltpu.VMEM((2, 2, t, t + 128), F32)],
        compiler_params=_cparams("parallel", "parallel", "arbitrary"),
        name="diff_attn",
    )(qt, kn, vt, bias, lam4, subg)


def _t5_bucket(rel, n_buckets):
    half = n_buckets // 2
    max_exact = half // 2
    ret = jnp.where(rel > 0, half, 0)
    n = jnp.abs(rel)
    nf = jnp.maximum(n, 1).astype(F32)
    large = max_exact + (jnp.log(nf / max_exact) / math.log(MAX_DISTANCE / max_exact)
                         * (half - max_exact)).astype(jnp.int32)
    large = jnp.minimum(large, half - 1)
    return ret + jnp.where(n < max_exact, n, large)


def _bias_body(f_ref, o_ref, *, t, nd):
    x = jnp.broadcast_to(f_ref[...], (t, 2 * t))
    y = pltpu.roll(x, t + 1, 1, stride=1, stride_axis=0)[:, :t]
    j = lax.broadcasted_iota(jnp.int32, (t, t), 0)
    i = lax.broadcasted_iota(jnp.int32, (t, t), 1)
    delta = pl.program_id(1)
    allowed = (((j // CHUNK) <= (i // CHUNK)) | (delta > 0)) & (delta < nd)
    o_ref[...] = jnp.where(allowed, y, NEG_BIG)


def _bias_tiles(rel_bias, t, nd):
    n_buckets, heads = rel_bias.shape
    c = jnp.arange(2 * t, dtype=jnp.int32)[None, :]
    delta = jnp.arange(nd + 1, dtype=jnp.int32)[:, None]
    rel = (t - 1 - c) - delta * t
    f = (rel_bias[_t5_bucket(rel, n_buckets)] - rel_bias[n_buckets // 2 - 1]) * LOG2E
    f = jnp.transpose(f, (2, 0, 1))[:, :, None, :]
    return pl.pallas_call(
        functools.partial(_bias_body, t=t, nd=nd),
        grid=(heads, nd + 1),
        in_specs=[pl.BlockSpec((None, None, 1, 2 * t), lambda h, d: (h, d, 0, 0))],
        out_specs=pl.BlockSpec((None, None, t, t), lambda h, d: (h, d, 0, 0)),
        out_shape=jax.ShapeDtypeStruct((heads, nd + 1, t, t), F32),
        compiler_params=_cparams("parallel", "parallel"),
        name="bias_tiles",
    )(f)


def _merge_body(x_ref, g1_ref, wg_ref, ya_ref, yb_ref, yc_ref, bg_ref, wpa_ref, wpb_ref, wpc_ref, wo_ref,
                o_ref):
    x = x_ref[...]
    d = x.shape[-1]
    h = _rms_norm_bf16(x, g1_ref[...])

    def branch(idx, y_ref, w_ref):
        p = jnp.dot(y_ref[...], w_ref[...], preferred_element_type=F32)
        g = jnp.dot(h, wg_ref[:, idx * d:(idx + 1) * d], preferred_element_type=F32)
        return jax.nn.sigmoid(g + bg_ref[idx]) * p

    merged = branch(0, ya_ref, wpa_ref) + branch(1, yb_ref, wpb_ref) + branch(2, yc_ref, wpc_ref)
    o_ref[...] = x + jnp.dot(merged.astype(BF16), wo_ref[...], preferred_element_type=F32)


def _merge(x2, g1, w_gate, ya, yb, yc, bg, wpa, wpb, wpc, wo, l, tm=512):
    n, d = x2.shape
    rows = lambda a: pl.BlockSpec((tm, a.shape[-1]), lambda i: (i, 0))
    wspec = lambda a: _resident((None,) + a.shape[1:], lambda i: (l, 0, 0))
    return pl.pallas_call(
        _merge_body,
        grid=(n // tm,),
        in_specs=[rows(x2), pl.BlockSpec((1, d), lambda i: (0, 0)), wspec(w_gate),
                  rows(ya), rows(yb), rows(yc), pl.BlockSpec(bg.shape, lambda i: (0, 0, 0)),
                  wspec(wpa), wspec(wpb), wspec(wpc), wspec(wo)],
        out_specs=pl.BlockSpec((tm, d), lambda i: (i, 0)),
        out_shape=jax.ShapeDtypeStruct((n, d), F32),
        compiler_params=_cparams("parallel"),
        name="merge",
    )(x2, g1, w_gate, ya, yb, yc, bg, wpa, wpb, wpc, wo)


def _ffn_body(x_ref, g_ref, wg_ref, wu_ref, wd_ref, o_ref):
    x = x_ref[...]
    h = _rms_norm_bf16(x, g_ref[...])
    gate = jnp.dot(h, wg_ref[...], preferred_element_type=F32)
    up = jnp.dot(h, wu_ref[...], preferred_element_type=F32)
    act = (jax.nn.silu(gate) * up).astype(BF16)
    o_ref[...] = x + jnp.dot(act, wd_ref[...], preferred_element_type=F32)


def _ffn(x2, g, wg, wu, wd, l, tm=512):
    n, d = x2.shape
    hid = wg.shape[-1]
    return pl.pallas_call(
        _ffn_body,
        grid=(n // tm,),
        in_specs=[
            pl.BlockSpec((tm, d), lambda i: (i, 0)),
            pl.BlockSpec((1, d), lambda i: (0, 0)),
            _resident((None, d, hid), lambda i: (l, 0, 0)),
            _resident((None, d, hid), lambda i: (l, 0, 0)),
            _resident((None, hid, d), lambda i: (l, 0, 0)),
        ],
        out_specs=pl.BlockSpec((tm, d), lambda i: (i, 0)),
        out_shape=jax.ShapeDtypeStruct((n, d), F32),
        compiler_params=_cparams("parallel"),
        name="ffn",
    )(x2, g, wg, wu, wd)


def _block_diag(w):
    heads, a, b = w.shape
    eye = jnp.eye(heads, dtype=w.dtype)
    return (eye[:, None, :, None] * w[:, :, None, :]).reshape(heads * a, heads * b)


def kernel(x, ln1_g, w_in, b_gate, conv_w, conv_b, lru_wa, lru_ba, lru_wi, lru_bi, lru_lambda, sg_ln_g, sg_ln_b, sg_w, sg_b, q_norm_g, k_norm_g, lambda_q1, lambda_k1, lambda_q2, lambda_k2, subln_g, rel_bias, w_pa, w_pb, w_pc, w_o, ln2_g, w_ff_gate, w_ff_up, w_ff_down):
    bsz, s, d = x.shape
    depth = w_in.shape[0]
    lru_w = conv_w.shape[-1]
    sg_wd = sg_ln_g.shape[-1]
    groups = sg_w.shape[1]
    heads = rel_bias.shape[1]
    wq = heads * 2 * QK_DIM
    assert w_in.shape[-1] == 2 * lru_w + 2 * sg_wd + 3 * wq + 3 * d and lru_w == sg_wd
    assert s % ATT_T == 0 and ATT_T % CHUNK == 0

    nd = -(-(MAX_DISTANCE // 2 + ATT_T) // ATT_T)
    bias = _bias_tiles(rel_bias, ATT_T, nd)
    bd = _block_diag(jnp.ones((wq // QK_DIM, QK_DIM, QK_DIM), BF16))
    n_branch = w_in.shape[-1] - 3 * d
    w_in_b, w_gate_b, w_pa_b, w_pb_b, w_pc_b, w_o_b = (
        w.astype(BF16) for w in (w_in[..., :n_branch], w_in[..., n_branch:], w_pa, w_pb, w_pc, w_o))
    wg_b, wu_b, wd_b = (w.astype(BF16) for w in (w_ff_gate, w_ff_up, w_ff_down))
    causal = jnp.tril(jnp.ones((SG_LEN, SG_LEN), bool))

    x2 = x.reshape(bsz * s, d)
    for l in range(depth):
        lam_init = 0.8 - 0.6 * math.exp(-0.3 * l)
        cvec = (-LRU_C * jax.nn.softplus(-lru_lambda[l]))[None]
        lru_params = (conv_w[l], conv_b[l][None], _block_diag(lru_wa[l]).astype(BF16), lru_ba[l][None],
                      _block_diag(lru_wi[l]).astype(BF16), lru_bi[l][None], cvec)
        ws = jnp.where(causal, sg_w[l], 0.0).astype(BF16)
        ws_cat = jnp.transpose(ws, (1, 0, 2)).reshape(SG_LEN, groups * SG_LEN)
        bs_full = jnp.repeat(sg_b[l].T, sg_wd // groups, axis=1)
        sgu_params = (sg_ln_g[l][None], sg_ln_b[l][None], ws_cat, bs_full)
        gq = jnp.tile(q_norm_g[l] * (QK_DIM ** -0.5 * LOG2E), wq // QK_DIM)[None]
        gk = jnp.tile(k_norm_g[l], wq // QK_DIM)[None]
        ya, yb, qt, kn, vt = _front(x2, bsz, ln1_g[l][None], w_in_b, l, lru_params, sgu_params,
                                           (gq, gk, bd), heads, groups)
        lam4 = jnp.stack([lambda_q1[l], lambda_k1[l], lambda_q2[l], lambda_k2[l]])
        subg = jnp.broadcast_to(subln_g[l][:, None], (V_DIM, ATT_T))
        yc = _attn(qt, kn, vt, bias, lam4, subg, lam_init)

        x2 = _merge(x2, ln1_g[l][None], w_gate_b, ya, yb, yc.reshape(bsz * s, -1),
                    b_gate[l].reshape(3, 1, d), w_pa_b, w_pb_b, w_pc_b, w_o_b, l)
        x2 = _ffn(x2, ln2_g[l][None], wg_b, wu_b, wd_b, l)
    return x2.reshape(bsz, s, d)
```

```python
import functools
import math

import jax
import jax.numpy as jnp
from jax import lax
from jax.experimental import pallas as pl
from jax.experimental.pallas import tpu as pltpu

F32 = jnp.float32
BF16 = jnp.bfloat16

EPS = 1e-6
CHUNK = 64
LRU_C = 8.0
SG_LEN = 128
QK_DIM = 64
V_DIM = 128
V_AUG = V_DIM + 16
MAX_DISTANCE = 2048
NEG_BIG = -1e30
LOG2E = math.log2(math.e)

ATT_T = 512
VMEM_LIMIT = 56 * 1024 * 1024


def _cparams(*sem):
    return pltpu.CompilerParams(dimension_semantics=sem, vmem_limit_bytes=VMEM_LIMIT)


def _rms_norm_bf16(x, g):
    ms = jnp.mean(x * x, axis=-1, keepdims=True)
    return (x * lax.rsqrt(ms + EPS) * g).astype(BF16)


def _resident(shape, index_map):
    return pl.BlockSpec(shape, index_map, pipeline_mode=pl.Buffered(1))


def _lru_branch(xa, ga, cw_ref, cb_ref, wa_ref, ba_ref, wi_ref, bi_ref, c_ref, ext_ref, h_ref):
    ts, w = xa.shape
    ext_ref[8:8 + ts, :] = xa
    xc = (cb_ref[...] + ext_ref[5:5 + ts, :] * cw_ref[0:1, :] + ext_ref[6:6 + ts, :] * cw_ref[1:2, :]
          + ext_ref[7:7 + ts, :] * cw_ref[2:3, :] + xa * cw_ref[3:4, :])
    ext_ref[0:8, :] = ext_ref[ts:ts + 8, :]

    xcb = xc.astype(BF16)
    r = jax.nn.sigmoid(jnp.dot(xcb, wa_ref[...], preferred_element_type=F32) + ba_ref[...])
    i = jax.nn.sigmoid(jnp.dot(xcb, wi_ref[...], preferred_element_type=F32) + bi_ref[...])
    log_a = c_ref[...] * r
    a = jnp.exp(log_a)
    mult = jnp.sqrt(-jnp.tanh(log_a) * (a * a + 1.0))
    b = mult * (i * xc)

    sub = 8
    a3 = a.reshape(ts // sub, sub, w)
    b3 = b.reshape(ts // sub, sub, w)
    row = lax.broadcasted_iota(jnp.int32, a3.shape, 1)
    d = 1
    while d < sub:
        keep = row >= d
        a_sh = jnp.where(keep, pltpu.roll(a3, d, 1), 1.0)
        b_sh = jnp.where(keep, pltpu.roll(b3, d, 1), 0.0)
        b3 = a3 * b_sh + b3
        a3 = a3 * a_sh
        d *= 2
    carry = h_ref[...]
    groups = []
    for g in range(ts // sub):
        hg = a3[g] * carry + b3[g]
        groups.append(hg)
        carry = hg[sub - 1:sub, :]
    h_ref[...] = carry
    h = jnp.concatenate(groups, axis=0)
    return h * jax.nn.gelu(ga)


def _sgu_branch(u, v, g_ref, b_ref, ws_ref, bs_ref, groups):
    tr, w = v.shape
    gd = w // groups
    mu = jnp.mean(v, axis=-1, keepdims=True)
    vc = v - mu
    var = jnp.mean(vc * vc, axis=-1, keepdims=True)
    vn = (vc * lax.rsqrt(var + EPS) * g_ref[...] + b_ref[...]).astype(BF16)
    lane_grp = lax.broadcasted_iota(jnp.int32, (SG_LEN, w), 1) // gd
    ws = ws_ref[...]
    zero = jnp.zeros((SG_LEN, w), BF16)
    out = []
    for blk in range(tr // SG_LEN):
        vb = vn[blk * SG_LEN:(blk + 1) * SG_LEN]
        stacked = jnp.concatenate([jnp.where(lane_grp == g, vb, zero) for g in range(groups)], axis=0)
        out.append(jnp.dot(ws, stacked, preferred_element_type=F32) + bs_ref[...])
    return u * jnp.concatenate(out, axis=0)


def _group_rms(y, g_ref, bd_ref):
    ss = jnp.dot((y * y).astype(BF16), bd_ref[...], preferred_element_type=F32)
    return y * lax.rsqrt(ss * (1.0 / QK_DIM) + EPS) * g_ref[...]


def _front_body(x_ref, g1_ref, w_ref, cw_ref, cb_ref, wa_ref, ba_ref, wi_ref, bi_ref, c_ref,
                lng_ref, lnb_ref, ws_ref, bs_ref, gq_ref, gk_ref, bd_ref,
                ya_ref, yb_ref, qt_ref, kn_ref, vt_ref, gates_ref, ext_ref, h_ref, *,
                heads, groups, lru_w, wq):
    @pl.when(pl.program_id(1) == 0)
    def _():
        ext_ref[0:8, :] = jnp.zeros((8, lru_w), F32)
        h_ref[...] = jnp.zeros_like(h_ref)

    h = _rms_norm_bf16(x_ref[...], g1_ref[...])

    def proj(c0, c1):
        return jnp.dot(h, w_ref[:, c0:c1], preferred_element_type=F32)

    c0 = 4 * lru_w
    pa = proj(0, 2 * lru_w)
    ya_ref[...] = _lru_branch(pa[:, :lru_w], pa[:, lru_w:], cw_ref, cb_ref, wa_ref, ba_ref, wi_ref,
                              bi_ref, c_ref, ext_ref, h_ref).astype(ya_ref.dtype)
    pb = proj(2 * lru_w, 4 * lru_w)
    yb_ref[...] = _sgu_branch(pb[:, :lru_w], pb[:, lru_w:], lng_ref, lnb_ref, ws_ref, bs_ref,
                              groups).astype(yb_ref.dtype)
    qn = _group_rms(proj(c0, c0 + wq), gq_ref, bd_ref)
    kn_ref[...] = _group_rms(proj(c0 + wq, c0 + 2 * wq), gk_ref, bd_ref).astype(BF16)
    v = proj(c0 + 2 * wq, c0 + 3 * wq)
    for hd in range(heads):
        cols = slice(hd * V_DIM, (hd + 1) * V_DIM)
        qt_ref[hd] = qn[:, cols].T.astype(BF16)
        vt_ref[hd, :V_DIM, :] = v[:, cols].T.astype(BF16)
        vt_ref[hd, V_DIM:, :] = jnp.ones((V_AUG - V_DIM, v.shape[0]), BF16)
    gates_ref[...] = proj(c0 + 3 * wq, w_ref.shape[-1]).astype(gates_ref.dtype)


def _front(x2, bsz, g1, w, l, lru_params, sgu_params, qk_params, heads, groups, tm=512):
    n, d = x2.shape
    s = n // bsz
    nt = s // tm
    c = w.shape[-1]
    lru_w = lru_params[0].shape[-1]
    wq = heads * 2 * QK_DIM
    n_gate = c - 4 * lru_w - 3 * wq
    small = list(lru_params) + list(sgu_params) + list(qk_params)
    full = lambda a: pl.BlockSpec(a.shape, lambda b, t: (0,) * a.ndim)
    rows = lambda width: pl.BlockSpec((tm, width), lambda b, t: (b * nt + t, 0))
    tspec = lambda r: pl.BlockSpec((None, heads, r, tm), lambda b, t: (b, 0, 0, t))
    return pl.pallas_call(
        functools.partial(_front_body, heads=heads, groups=groups, lru_w=lru_w, wq=wq),
        grid=(bsz, nt),
        in_specs=[rows(d), pl.BlockSpec((1, d), lambda b, t: (0, 0)),
                  _resident((None, d, c), lambda b, t: (l, 0, 0))] + [full(a) for a in small],
        out_specs=[rows(lru_w), rows(lru_w), tspec(V_DIM),
                   pl.BlockSpec((None, tm, wq), lambda b, t: (b, t, 0)), tspec(V_AUG), rows(n_gate)],
        out_shape=[
            jax.ShapeDtypeStruct((n, lru_w), BF16),
            jax.ShapeDtypeStruct((n, lru_w), BF16),
            jax.ShapeDtypeStruct((bsz, heads, V_DIM, s), BF16),
            jax.ShapeDtypeStruct((bsz, s, wq), BF16),
            jax.ShapeDtypeStruct((bsz, heads, V_AUG, s), BF16),
            jax.ShapeDtypeStruct((n, n_gate), BF16),
        ],
        scratch_shapes=[pltpu.VMEM((tm + 8, lru_w), F32), pltpu.VMEM((1, lru_w), F32)],
        compiler_params=_cparams("parallel", "arbitrary"),
        name="front",
    )(x2, g1, w, *small)


def _attn_body(qt_ref, k_ref, vt_ref, bias_ref, lam_ref, subg_ref, o_ref, acc_ref, s_ref, *, t, nd,
               lam_init):
    qi = pl.program_id(2)
    qt = qt_ref[...]
    row = lax.broadcasted_iota(jnp.int32, qt.shape, 0)
    zero = jnp.zeros_like(qt)
    q1 = jnp.where(row < QK_DIM, qt, zero)
    q2 = jnp.where(row >= QK_DIM, qt, zero)
    acc_ref[...] = jnp.zeros_like(acc_ref)

    def scores(kj):
        k = k_ref[pl.ds(pl.multiple_of(kj * t, t), t), :]
        return (jnp.dot(k, q1, preferred_element_type=F32),
                jnp.dot(k, q2, preferred_element_type=F32))

    def colmax(s):
        return jnp.max(s, axis=0, keepdims=True)

    def consume(idx, kj, s, mc, m):
        vt = vt_ref[:, pl.ds(pl.multiple_of(kj * t, t), t)]
        mn = jnp.maximum(m, mc)
        p = jnp.exp2(s - mn).astype(BF16)
        acc_ref[idx] = acc_ref[idx] * jnp.exp2(m - mn) + jnp.dot(vt, p, preferred_element_type=F32)
        return mn

    n_far = jnp.maximum(qi - (nd - 1), 0)
    last_far = jnp.maximum(n_far - 1, 0)

    def produce(slot, kj, bias_idx=None):
        s1, s2 = scores(kj)
        if bias_idx is not None:
            b = bias_ref[bias_idx]
            s1 = s1 + b
            s2 = s2 + b
        s_ref[slot, 0, :, :t] = s1
        s_ref[slot, 1, :, :t] = s2
        return colmax(s1), colmax(s2)

    def consume_slot(slot, kj, c, st):
        return (consume(0, kj, s_ref[slot, 0, :, :t], c[0], st[0]),
                consume(1, kj, s_ref[slot, 1, :, :t], c[1], st[1]))

    neg = jnp.full((1, t), NEG_BIG, F32)
    st = (neg, neg)

    pending = None
    for d in range(nd - 1, -1, -1):
        slot = (d + 1) % 2
        kj = jnp.maximum(qi - d, 0)
        c = produce(slot, kj, jnp.where(qi >= d, d, nd))
        if pending is not None:
            st = consume_slot(*pending, st)
        pending = (slot, kj, c)

    def produce_far(slot, kj):
        return produce(slot, jnp.minimum(kj, last_far))

    def far_pair(i, carry):
        st, c0 = carry
        c1 = produce_far(1, 2 * i + 1)
        st = consume_slot(0, 2 * i, c0, st)
        c0 = produce_far(0, 2 * i + 2)
        st = consume_slot(1, 2 * i + 1, c1, st)
        return st, c0

    def far_quad(i, carry):
        return far_pair(2 * i + 1, far_pair(2 * i, carry))

    c0 = produce_far(0, 0)
    st = consume_slot(*pending, st)
    def far_oct(i, carry):
        return far_quad(2 * i + 1, far_quad(2 * i, carry))

    carry = lax.fori_loop(0, n_far // 8, far_oct, (st, c0))
    carry = lax.cond(n_far % 8 >= 4, lambda: far_quad(n_far // 8 * 2, carry), lambda: carry)
    st, c0 = lax.cond(n_far % 4 >= 2, lambda: far_pair(n_far // 4 * 2, carry), lambda: carry)

    @pl.when(n_far % 2 == 1)
    def _():
        consume_slot(0, last_far, c0, st)

    lam4 = lam_ref[...]
    lam = (jnp.exp(jnp.sum(lam4[0:1] * lam4[1:2], axis=-1, keepdims=True))
           - jnp.exp(jnp.sum(lam4[2:3] * lam4[3:4], axis=-1, keepdims=True)) + lam_init)
    o = (acc_ref[0, :V_DIM, :] / acc_ref[0, V_DIM:V_DIM + 1, :]
         - lam * (acc_ref[1, :V_DIM, :] / acc_ref[1, V_DIM:V_DIM + 1, :]))
    ms = jnp.mean(o * o, axis=0, keepdims=True)
    o = o * lax.rsqrt(ms + EPS) * subg_ref[...] * (1.0 - lam_init)
    o_ref[...] = o.T.astype(o_ref.dtype)


def _attn(qt, kn, vt, bias, lam4, subg, lam_init, t=ATT_T):
    bsz, heads, _, s = qt.shape
    nd = bias.shape[1] - 1
    return pl.pallas_call(
        functools.partial(_attn_body, t=t, nd=nd, lam_init=lam_init),
        grid=(bsz, heads, s // t),
        in_specs=[
            pl.BlockSpec((None, None, V_DIM, t), lambda b, h, i: (b, h, 0, i)),
            pl.BlockSpec((None, s, V_DIM), lambda b, h, i: (b, 0, h)),
            pl.BlockSpec((None, None, V_AUG, s), lambda b, h, i: (b, h, 0, 0)),
            pl.BlockSpec((None, nd + 1, t, t), lambda b, h, i: (h, 0, 0, 0)),
            pl.BlockSpec(lam4.shape, lambda b, h, i: (0, 0)),
            pl.BlockSpec(subg.shape, lambda b, h, i: (0, 0)),
        ],
        out_specs=pl.BlockSpec((None, t, V_DIM), lambda b, h, i: (b, i, h)),
        out_shape=jax.ShapeDtypeStruct((bsz, s, heads * V_DIM), BF16),
        scratch_shapes=[pltpu.VMEM((2, V_AUG, t), F32), pltpu.VMEM((2, 2, t, t + 128), F32)],
        compiler_params=_cparams("parallel", "parallel", "arbitrary"),
        name="diff_attn",
    )(qt, kn, vt, bias, lam4, subg)


def _t5_bucket(rel, n_buckets):
    half = n_buckets // 2
    max_exact = half // 2
    ret = jnp.where(rel > 0, half, 0)
    n = jnp.abs(rel)
    nf = jnp.maximum(n, 1).astype(F32)
    large = max_exact + (jnp.log(nf / max_exact) / math.log(MAX_DISTANCE / max_exact)
                         * (half - max_exact)).astype(jnp.int32)
    large = jnp.minimum(large, half - 1)
    return ret + jnp.where(n < max_exact, n, large)


def _bias_body(f_ref, o_ref, *, t, nd):
    x = jnp.broadcast_to(f_ref[...], (t, 2 * t))
    y = pltpu.roll(x, t + 1, 1, stride=1, stride_axis=0)[:, :t]
    j = lax.broadcasted_iota(jnp.int32, (t, t), 0)
    i = lax.broadcasted_iota(jnp.int32, (t, t), 1)
    delta = pl.program_id(1)
    allowed = (((j // CHUNK) <= (i // CHUNK)) | (delta > 0)) & (delta < nd)
    o_ref[...] = jnp.where(allowed, y, NEG_BIG)


def _bias_tiles(rel_bias, t, nd):
    n_buckets, heads = rel_bias.shape
    c = jnp.arange(2 * t, dtype=jnp.int32)[None, :]
    delta = jnp.arange(nd + 1, dtype=jnp.int32)[:, None]
    rel = (t - 1 - c) - delta * t
    f = (rel_bias[_t5_bucket(rel, n_buckets)] - rel_bias[n_buckets // 2 - 1]) * LOG2E
    f = jnp.transpose(f, (2, 0, 1))[:, :, None, :]
    return pl.pallas_call(
        functools.partial(_bias_body, t=t, nd=nd),
        grid=(heads, nd + 1),
        in_specs=[pl.BlockSpec((None, None, 1, 2 * t), lambda h, d: (h, d, 0, 0))],
        out_specs=pl.BlockSpec((None, None, t, t), lambda h, d: (h, d, 0, 0)),
        out_shape=jax.ShapeDtypeStruct((heads, nd + 1, t, t), F32),
        compiler_params=_cparams("parallel", "parallel"),
        name="bias_tiles",
    )(f)


def _merge_body(x_ref, ya_ref, yb_ref, yc_ref, gates_ref, bg_ref, wpa_ref, wpb_ref, wpc_ref, wo_ref,
                o_ref):
    d = x_ref.shape[-1]

    def branch(idx, y_ref, w_ref):
        p = jnp.dot(y_ref[...], w_ref[...], preferred_element_type=F32)
        g = gates_ref[:, idx * d:(idx + 1) * d].astype(F32)
        return jax.nn.sigmoid(g + bg_ref[idx]) * p

    merged = branch(0, ya_ref, wpa_ref) + branch(1, yb_ref, wpb_ref) + branch(2, yc_ref, wpc_ref)
    o_ref[...] = x_ref[...] + jnp.dot(merged.astype(BF16), wo_ref[...], preferred_element_type=F32)


def _merge(x2, ya, yb, yc, gates, bg, wpa, wpb, wpc, wo, l, tm=1024):
    n, d = x2.shape
    rows = lambda a: pl.BlockSpec((tm, a.shape[-1]), lambda i: (i, 0))
    wspec = lambda a: pl.BlockSpec((None,) + a.shape[1:], lambda i: (l, 0, 0))
    return pl.pallas_call(
        _merge_body,
        grid=(n // tm,),
        in_specs=[rows(x2), rows(ya), rows(yb), rows(yc), rows(gates),
                  pl.BlockSpec(bg.shape, lambda i: (0, 0, 0)),
                  wspec(wpa), wspec(wpb), wspec(wpc), wspec(wo)],
        out_specs=pl.BlockSpec((tm, d), lambda i: (i, 0)),
        out_shape=jax.ShapeDtypeStruct((n, d), F32),
        compiler_params=_cparams("parallel"),
        name="merge",
    )(x2, ya, yb, yc, gates, bg, wpa, wpb, wpc, wo)


def _ffn_body(x_ref, g_ref, wg_ref, wu_ref, wd_ref, o_ref):
    x = x_ref[...]
    h = _rms_norm_bf16(x, g_ref[...])
    gate = jnp.dot(h, wg_ref[...], preferred_element_type=F32)
    up = jnp.dot(h, wu_ref[...], preferred_element_type=F32)
    act = (jax.nn.silu(gate) * up).astype(BF16)
    o_ref[...] = x + jnp.dot(act, wd_ref[...], preferred_element_type=F32)


def _ffn(x2, g, wg, wu, wd, l, tm=512):
    n, d = x2.shape
    hid = wg.shape[-1]
    return pl.pallas_call(
        _ffn_body,
        grid=(n // tm,),
        in_specs=[
            pl.BlockSpec((tm, d), lambda i: (i, 0)),
            pl.BlockSpec((1, d), lambda i: (0, 0)),
            _resident((None, d, hid), lambda i: (l, 0, 0)),
            _resident((None, d, hid), lambda i: (l, 0, 0)),
            _resident((None, hid, d), lambda i: (l, 0, 0)),
        ],
        out_specs=pl.BlockSpec((tm, d), lambda i: (i, 0)),
        out_shape=jax.ShapeDtypeStruct((n, d), F32),
        compiler_params=_cparams("parallel"),
        name="ffn",
    )(x2, g, wg, wu, wd)


def _block_diag(w):
    heads, a, b = w.shape
    eye = jnp.eye(heads, dtype=w.dtype)
    return (eye[:, None, :, None] * w[:, :, None, :]).reshape(heads * a, heads * b)


def kernel(x, ln1_g, w_in, b_gate, conv_w, conv_b, lru_wa, lru_ba, lru_wi, lru_bi, lru_lambda, sg_ln_g, sg_ln_b, sg_w, sg_b, q_norm_g, k_norm_g, lambda_q1, lambda_k1, lambda_q2, lambda_k2, subln_g, rel_bias, w_pa, w_pb, w_pc, w_o, ln2_g, w_ff_gate, w_ff_up, w_ff_down):
    bsz, s, d = x.shape
    depth = w_in.shape[0]
    lru_w = conv_w.shape[-1]
    sg_wd = sg_ln_g.shape[-1]
    groups = sg_w.shape[1]
    heads = rel_bias.shape[1]
    wq = heads * 2 * QK_DIM
    assert w_in.shape[-1] == 2 * lru_w + 2 * sg_wd + 3 * wq + 3 * d and lru_w == sg_wd
    assert s % ATT_T == 0 and ATT_T % CHUNK == 0

    nd = -(-(MAX_DISTANCE // 2 + ATT_T) // ATT_T)
    bias = _bias_tiles(rel_bias, ATT_T, nd)
    bd = _block_diag(jnp.ones((wq // QK_DIM, QK_DIM, QK_DIM), BF16))
    w_in_b, w_pa_b, w_pb_b, w_pc_b, w_o_b = (w.astype(BF16) for w in (w_in, w_pa, w_pb, w_pc, w_o))
    wg_b, wu_b, wd_b = (w.astype(BF16) for w in (w_ff_gate, w_ff_up, w_ff_down))
    causal = jnp.tril(jnp.ones((SG_LEN, SG_LEN), bool))

    x2 = x.reshape(bsz * s, d)
    for l in range(depth):
        lam_init = 0.8 - 0.6 * math.exp(-0.3 * l)
        cvec = (-LRU_C * jax.nn.softplus(-lru_lambda[l]))[None]
        lru_params = (conv_w[l], conv_b[l][None], _block_diag(lru_wa[l]).astype(BF16), lru_ba[l][None],
                      _block_diag(lru_wi[l]).astype(BF16), lru_bi[l][None], cvec)
        ws = jnp.where(causal, sg_w[l], 0.0).astype(BF16)
        ws_cat = jnp.transpose(ws, (1, 0, 2)).reshape(SG_LEN, groups * SG_LEN)
        bs_full = jnp.repeat(sg_b[l].T, sg_wd // groups, axis=1)
        sgu_params = (sg_ln_g[l][None], sg_ln_b[l][None], ws_cat, bs_full)
        gq = jnp.tile(q_norm_g[l] * (QK_DIM ** -0.5 * LOG2E), wq // QK_DIM)[None]
        gk = jnp.tile(k_norm_g[l], wq // QK_DIM)[None]
        ya, yb, qt, kn, vt, gates = _front(x2, bsz, ln1_g[l][None], w_in_b, l, lru_params, sgu_params,
                                           (gq, gk, bd), heads, groups)
        lam4 = jnp.stack([lambda_q1[l], lambda_k1[l], lambda_q2[l], lambda_k2[l]])
        subg = jnp.broadcast_to(subln_g[l][:, None], (V_DIM, ATT_T))
        yc = _attn(qt, kn, vt, bias, lam4, subg, lam_init)

        x2 = _merge(x2, ya, yb, yc.reshape(bsz * s, -1), gates, b_gate[l].reshape(3, 1, d),
                    w_pa_b, w_pb_b, w_pc_b, w_o_b, l)
        x2 = _ffn(x2, ln2_g[l][None], wg_b, wu_b, wd_b, l)
    return x2.reshape(bsz, s, d)
```

```python
import functools
import math

import jax
import jax.numpy as jnp
from jax import lax
from jax.experimental import pallas as pl
from jax.experimental.pallas import tpu as pltpu

F32 = jnp.float32
BF16 = jnp.bfloat16

EPS = 1e-6
CHUNK = 64
LRU_C = 8.0
SG_LEN = 128
QK_DIM = 64
V_DIM = 128
V_AUG = V_DIM + 16
MAX_DISTANCE = 2048
NEG_BIG = -1e30
LOG2E = math.log2(math.e)

ATT_T = 512
VMEM_LIMIT = 56 * 1024 * 1024


def _cparams(*sem):
    return pltpu.CompilerParams(dimension_semantics=sem, vmem_limit_bytes=VMEM_LIMIT)


def _rms_norm_bf16(x, g):
    ms = jnp.mean(x * x, axis=-1, keepdims=True)
    return (x * lax.rsqrt(ms + EPS) * g).astype(BF16)


def _resident(shape, index_map):
    return pl.BlockSpec(shape, index_map, pipeline_mode=pl.Buffered(1))


def _lru_branch(xa, ga, cw_ref, cb_ref, wa_ref, ba_ref, wi_ref, bi_ref, c_ref, ext_ref, h_ref):
    ts, w = xa.shape
    ext_ref[8:8 + ts, :] = xa
    xc = (cb_ref[...] + ext_ref[5:5 + ts, :] * cw_ref[0:1, :] + ext_ref[6:6 + ts, :] * cw_ref[1:2, :]
          + ext_ref[7:7 + ts, :] * cw_ref[2:3, :] + xa * cw_ref[3:4, :])
    ext_ref[0:8, :] = ext_ref[ts:ts + 8, :]

    xcb = xc.astype(BF16)
    r = jax.nn.sigmoid(jnp.dot(xcb, wa_ref[...], preferred_element_type=F32) + ba_ref[...])
    i = jax.nn.sigmoid(jnp.dot(xcb, wi_ref[...], preferred_element_type=F32) + bi_ref[...])
    log_a = c_ref[...] * r
    a = jnp.exp(log_a)
    mult = jnp.sqrt(-jnp.tanh(log_a) * (a * a + 1.0))
    b = mult * (i * xc)

    sub = 8
    a3 = a.reshape(ts // sub, sub, w)
    b3 = b.reshape(ts // sub, sub, w)
    row = lax.broadcasted_iota(jnp.int32, a3.shape, 1)
    d = 1
    while d < sub:
        keep = row >= d
        a_sh = jnp.where(keep, pltpu.roll(a3, d, 1), 1.0)
        b_sh = jnp.where(keep, pltpu.roll(b3, d, 1), 0.0)
        b3 = a3 * b_sh + b3
        a3 = a3 * a_sh
        d *= 2
    carry = h_ref[...]
    groups = []
    for g in range(ts // sub):
        hg = a3[g] * carry + b3[g]
        groups.append(hg)
        carry = hg[sub - 1:sub, :]
    h_ref[...] = carry
    h = jnp.concatenate(groups, axis=0)
    return h * jax.nn.gelu(ga)


def _sgu_branch(u, v, g_ref, b_ref, ws_ref, bs_ref, groups):
    tr, w = v.shape
    gd = w // groups
    mu = jnp.mean(v, axis=-1, keepdims=True)
    vc = v - mu
    var = jnp.mean(vc * vc, axis=-1, keepdims=True)
    vn = (vc * lax.rsqrt(var + EPS) * g_ref[...] + b_ref[...]).astype(BF16)
    lane_grp = lax.broadcasted_iota(jnp.int32, (SG_LEN, w), 1) // gd
    ws = ws_ref[...]
    zero = jnp.zeros((SG_LEN, w), BF16)
    out = []
    for blk in range(tr // SG_LEN):
        vb = vn[blk * SG_LEN:(blk + 1) * SG_LEN]
        stacked = jnp.concatenate([jnp.where(lane_grp == g, vb, zero) for g in range(groups)], axis=0)
        out.append(jnp.dot(ws, stacked, preferred_element_type=F32) + bs_ref[...])
    return u * jnp.concatenate(out, axis=0)


def _group_rms(y, g_ref, bd_ref):
    ss = jnp.dot((y * y).astype(BF16), bd_ref[...], preferred_element_type=F32)
    return y * lax.rsqrt(ss * (1.0 / QK_DIM) + EPS) * g_ref[...]


def _front_body(x_ref, g1_ref, w_ref, cw_ref, cb_ref, wa_ref, ba_ref, wi_ref, bi_ref, c_ref,
                lng_ref, lnb_ref, ws_ref, bs_ref, gq_ref, gk_ref, bd_ref,
                ya_ref, yb_ref, qt_ref, kn_ref, vt_ref, gates_ref, ext_ref, h_ref, *,
                heads, groups, lru_w, wq):
    @pl.when(pl.program_id(1) == 0)
    def _():
        ext_ref[0:8, :] = jnp.zeros((8, lru_w), F32)
        h_ref[...] = jnp.zeros_like(h_ref)

    h = _rms_norm_bf16(x_ref[...], g1_ref[...])

    def proj(c0, c1):
        return jnp.dot(h, w_ref[:, c0:c1], preferred_element_type=F32)

    c0 = 4 * lru_w
    pa = proj(0, 2 * lru_w)
    ya_ref[...] = _lru_branch(pa[:, :lru_w], pa[:, lru_w:], cw_ref, cb_ref, wa_ref, ba_ref, wi_ref,
                              bi_ref, c_ref, ext_ref, h_ref).astype(ya_ref.dtype)
    pb = proj(2 * lru_w, 4 * lru_w)
    yb_ref[...] = _sgu_branch(pb[:, :lru_w], pb[:, lru_w:], lng_ref, lnb_ref, ws_ref, bs_ref,
                              groups).astype(yb_ref.dtype)
    qn = _group_rms(proj(c0, c0 + wq), gq_ref, bd_ref)
    kn_ref[...] = _group_rms(proj(c0 + wq, c0 + 2 * wq), gk_ref, bd_ref).astype(BF16)
    v = proj(c0 + 2 * wq, c0 + 3 * wq)
    for hd in range(heads):
        cols = slice(hd * V_DIM, (hd + 1) * V_DIM)
        qt_ref[hd] = qn[:, cols].T.astype(BF16)
        vt_ref[hd, :V_DIM, :] = v[:, cols].T.astype(BF16)
        vt_ref[hd, V_DIM:, :] = jnp.ones((V_AUG - V_DIM, v.shape[0]), BF16)
    gates_ref[...] = proj(c0 + 3 * wq, w_ref.shape[-1]).astype(gates_ref.dtype)


def _front(x2, bsz, g1, w, l, lru_params, sgu_params, qk_params, heads, groups, tm=512):
    n, d = x2.shape
    s = n // bsz
    nt = s // tm
    c = w.shape[-1]
    lru_w = lru_params[0].shape[-1]
    wq = heads * 2 * QK_DIM
    n_gate = c - 4 * lru_w - 3 * wq
    small = list(lru_params) + list(sgu_params) + list(qk_params)
    full = lambda a: pl.BlockSpec(a.shape, lambda b, t: (0,) * a.ndim)
    rows = lambda width: pl.BlockSpec((tm, width), lambda b, t: (b * nt + t, 0))
    tspec = lambda r: pl.BlockSpec((None, heads, r, tm), lambda b, t: (b, 0, 0, t))
    return pl.pallas_call(
        functools.partial(_front_body, heads=heads, groups=groups, lru_w=lru_w, wq=wq),
        grid=(bsz, nt),
        in_specs=[rows(d), pl.BlockSpec((1, d), lambda b, t: (0, 0)),
                  _resident((None, d, c), lambda b, t: (l, 0, 0))] + [full(a) for a in small],
        out_specs=[rows(lru_w), rows(lru_w), tspec(V_DIM),
                   pl.BlockSpec((None, tm, wq), lambda b, t: (b, t, 0)), tspec(V_AUG), rows(n_gate)],
        out_shape=[
            jax.ShapeDtypeStruct((n, lru_w), BF16),
            jax.ShapeDtypeStruct((n, lru_w), BF16),
            jax.ShapeDtypeStruct((bsz, heads, V_DIM, s), BF16),
            jax.ShapeDtypeStruct((bsz, s, wq), BF16),
            jax.ShapeDtypeStruct((bsz, heads, V_AUG, s), BF16),
            jax.ShapeDtypeStruct((n, n_gate), BF16),
        ],
        scratch_shapes=[pltpu.VMEM((tm + 8, lru_w), F32), pltpu.VMEM((1, lru_w), F32)],
        compiler_params=_cparams("parallel", "arbitrary"),
        name="front",
    )(x2, g1, w, *small)


def _attn_body(qt_ref, k_ref, vt_ref, bias_ref, lam_ref, subg_ref, o_ref, acc_ref, s_ref, *, t, nd,
               lam_init):
    qi = pl.program_id(2)
    qt = qt_ref[...]
    row = lax.broadcasted_iota(jnp.int32, qt.shape, 0)
    zero = jnp.zeros_like(qt)
    q1 = jnp.where(row < QK_DIM, qt, zero)
    q2 = jnp.where(row >= QK_DIM, qt, zero)
    acc_ref[...] = jnp.zeros_like(acc_ref)

    def scores(kj):
        k = k_ref[pl.ds(pl.multiple_of(kj * t, t), t), :]
        return (jnp.dot(k, q1, preferred_element_type=F32),
                jnp.dot(k, q2, preferred_element_type=F32))

    def colmax(s):
        return jnp.max(s, axis=0, keepdims=True)

    def consume(idx, kj, s, mc, m):
        vt = vt_ref[:, pl.ds(pl.multiple_of(kj * t, t), t)]
        mn = jnp.maximum(m, mc)
        p = jnp.exp2(s - mn).astype(BF16)
        acc_ref[idx] = acc_ref[idx] * jnp.exp2(m - mn) + jnp.dot(vt, p, preferred_element_type=F32)
        return mn

    n_far = jnp.maximum(qi - (nd - 1), 0)
    last_far = jnp.maximum(n_far - 1, 0)

    def produce(slot, kj, bias_idx=None):
        s1, s2 = scores(kj)
        if bias_idx is not None:
            b = bias_ref[bias_idx]
            s1 = s1 + b
            s2 = s2 + b
        s_ref[slot, 0, :, :t] = s1
        s_ref[slot, 1, :, :t] = s2
        return colmax(s1), colmax(s2)

    def consume_slot(slot, kj, c, st):
        return (consume(0, kj, s_ref[slot, 0, :, :t], c[0], st[0]),
                consume(1, kj, s_ref[slot, 1, :, :t], c[1], st[1]))

    neg = jnp.full((1, t), NEG_BIG, F32)
    st = (neg, neg)

    pending = None
    for d in range(nd - 1, -1, -1):
        slot = (d + 1) % 2
        kj = jnp.maximum(qi - d, 0)
        c = produce(slot, kj, jnp.where(qi >= d, d, nd))
        if pending is not None:
            st = consume_slot(*pending, st)
        pending = (slot, kj, c)

    def produce_far(slot, kj):
        return produce(slot, jnp.minimum(kj, last_far))

    def far_pair(i, carry):
        st, c0 = carry
        c1 = produce_far(1, 2 * i + 1)
        st = consume_slot(0, 2 * i, c0, st)
        c0 = produce_far(0, 2 * i + 2)
        st = consume_slot(1, 2 * i + 1, c1, st)
        return st, c0

    def far_quad(i, carry):
        return far_pair(2 * i + 1, far_pair(2 * i, carry))

    c0 = produce_far(0, 0)
    st = consume_slot(*pending, st)
    carry = lax.fori_loop(0, n_far // 4, far_quad, (st, c0))
    st, c0 = lax.cond(n_far % 4 >= 2, lambda: far_pair(n_far // 4 * 2, carry), lambda: carry)

    @pl.when(n_far % 2 == 1)
    def _():
        consume_slot(0, last_far, c0, st)

    lam4 = lam_ref[...]
    lam = (jnp.exp(jnp.sum(lam4[0:1] * lam4[1:2], axis=-1, keepdims=True))
           - jnp.exp(jnp.sum(lam4[2:3] * lam4[3:4], axis=-1, keepdims=True)) + lam_init)
    o = (acc_ref[0, :V_DIM, :] / acc_ref[0, V_DIM:V_DIM + 1, :]
         - lam * (acc_ref[1, :V_DIM, :] / acc_ref[1, V_DIM:V_DIM + 1, :]))
    ms = jnp.mean(o * o, axis=0, keepdims=True)
    o = o * lax.rsqrt(ms + EPS) * subg_ref[...] * (1.0 - lam_init)
    o_ref[...] = o.T.astype(o_ref.dtype)


def _attn(qt, kn, vt, bias, lam4, subg, lam_init, t=ATT_T):
    bsz, heads, _, s = qt.shape
    nd = bias.shape[1] - 1
    return pl.pallas_call(
        functools.partial(_attn_body, t=t, nd=nd, lam_init=lam_init),
        grid=(bsz, heads, s // t),
        in_specs=[
            pl.BlockSpec((None, None, V_DIM, t), lambda b, h, i: (b, h, 0, i)),
            pl.BlockSpec((None, s, V_DIM), lambda b, h, i: (b, 0, h)),
            pl.BlockSpec((None, None, V_AUG, s), lambda b, h, i: (b, h, 0, 0)),
            pl.BlockSpec((None, nd + 1, t, t), lambda b, h, i: (h, 0, 0, 0)),
            pl.BlockSpec(lam4.shape, lambda b, h, i: (0, 0)),
            pl.BlockSpec(subg.shape, lambda b, h, i: (0, 0)),
        ],
        out_specs=pl.BlockSpec((None, t, V_DIM), lambda b, h, i: (b, i, h)),
        out_shape=jax.ShapeDtypeStruct((bsz, s, heads * V_DIM), BF16),
        scratch_shapes=[pltpu.VMEM((2, V_AUG, t), F32), pltpu.VMEM((2, 2, t, t + 128), F32)],
        compiler_params=_cparams("parallel", "parallel", "arbitrary"),
        name="diff_attn",
    )(qt, kn, vt, bias, lam4, subg)


def _t5_bucket(rel, n_buckets):
    half = n_buckets // 2
    max_exact = half // 2
    ret = jnp.where(rel > 0, half, 0)
    n = jnp.abs(rel)
    nf = jnp.maximum(n, 1).astype(F32)
    large = max_exact + (jnp.log(nf / max_exact) / math.log(MAX_DISTANCE / max_exact)
                         * (half - max_exact)).astype(jnp.int32)
    large = jnp.minimum(large, half - 1)
    return ret + jnp.where(n < max_exact, n, large)


def _bias_body(f_ref, o_ref, *, t, nd):
    x = jnp.broadcast_to(f_ref[...], (t, 2 * t))
    y = pltpu.roll(x, t + 1, 1, stride=1, stride_axis=0)[:, :t]
    j = lax.broadcasted_iota(jnp.int32, (t, t), 0)
    i = lax.broadcasted_iota(jnp.int32, (t, t), 1)
    delta = pl.program_id(1)
    allowed = (((j // CHUNK) <= (i // CHUNK)) | (delta > 0)) & (delta < nd)
    o_ref[...] = jnp.where(allowed, y, NEG_BIG)


def _bias_tiles(rel_bias, t, nd):
    n_buckets, heads = rel_bias.shape
    c = jnp.arange(2 * t, dtype=jnp.int32)[None, :]
    delta = jnp.arange(nd + 1, dtype=jnp.int32)[:, None]
    rel = (t - 1 - c) - delta * t
    f = (rel_bias[_t5_bucket(rel, n_buckets)] - rel_bias[n_buckets // 2 - 1]) * LOG2E
    f = jnp.transpose(f, (2, 0, 1))[:, :, None, :]
    return pl.pallas_call(
        functools.partial(_bias_body, t=t, nd=nd),
        grid=(heads, nd + 1),
        in_specs=[pl.BlockSpec((None, None, 1, 2 * t), lambda h, d: (h, d, 0, 0))],
        out_specs=pl.BlockSpec((None, None, t, t), lambda h, d: (h, d, 0, 0)),
        out_shape=jax.ShapeDtypeStruct((heads, nd + 1, t, t), F32),
        compiler_params=_cparams("parallel", "parallel"),
        name="bias_tiles",
    )(f)


def _merge_body(x_ref, ya_ref, yb_ref, yc_ref, gates_ref, bg_ref, wpa_ref, wpb_ref, wpc_ref, wo_ref,
                o_ref):
    d = x_ref.shape[-1]

    def branch(idx, y_ref, w_ref):
        p = jnp.dot(y_ref[...], w_ref[...], preferred_element_type=F32)
        g = gates_ref[:, idx * d:(idx + 1) * d].astype(F32)
        return jax.nn.sigmoid(g + bg_ref[idx]) * p

    merged = branch(0, ya_ref, wpa_ref) + branch(1, yb_ref, wpb_ref) + branch(2, yc_ref, wpc_ref)
    o_ref[...] = x_ref[...] + jnp.dot(merged.astype(BF16), wo_ref[...], preferred_element_type=F32)


def _merge(x2, ya, yb, yc, gates, bg, wpa, wpb, wpc, wo, l, tm=1024):
    n, d = x2.shape
    rows = lambda a: pl.BlockSpec((tm, a.shape[-1]), lambda i: (i, 0))
    wspec = lambda a: pl.BlockSpec((None,) + a.shape[1:], lambda i: (l, 0, 0))
    return pl.pallas_call(
        _merge_body,
        grid=(n // tm,),
        in_specs=[rows(x2), rows(ya), rows(yb), rows(yc), rows(gates),
                  pl.BlockSpec(bg.shape, lambda i: (0, 0, 0)),
                  wspec(wpa), wspec(wpb), wspec(wpc), wspec(wo)],
        out_specs=pl.BlockSpec((tm, d), lambda i: (i, 0)),
        out_shape=jax.ShapeDtypeStruct((n, d), F32),
        compiler_params=_cparams("parallel"),
        name="merge",
    )(x2, ya, yb, yc, gates, bg, wpa, wpb, wpc, wo)


def _ffn_body(x_ref, g_ref, wg_ref, wu_ref, wd_ref, o_ref):
    x = x_ref[...]
    h = _rms_norm_bf16(x, g_ref[...])
    hid = wg_ref.shape[-1]
    cut = -(-(hid // 2) // 256) * 256
    out = x
    for lo, hi in ((0, cut), (cut, hid)):
        gate = jnp.dot(h, wg_ref[:, lo:hi], preferred_element_type=F32)
        up = jnp.dot(h, wu_ref[:, lo:hi], preferred_element_type=F32)
        act = (jax.nn.silu(gate) * up).astype(BF16)
        out = out + jnp.dot(act, wd_ref[lo:hi, :], preferred_element_type=F32)
    o_ref[...] = out


def _ffn(x2, g, wg, wu, wd, l, tm=1024):
    n, d = x2.shape
    hid = wg.shape[-1]
    return pl.pallas_call(
        _ffn_body,
        grid=(n // tm,),
        in_specs=[
            pl.BlockSpec((tm, d), lambda i: (i, 0)),
            pl.BlockSpec((1, d), lambda i: (0, 0)),
            _resident((None, d, hid), lambda i: (l, 0, 0)),
            _resident((None, d, hid), lambda i: (l, 0, 0)),
            _resident((None, hid, d), lambda i: (l, 0, 0)),
        ],
        out_specs=pl.BlockSpec((tm, d), lambda i: (i, 0)),
        out_shape=jax.ShapeDtypeStruct((n, d), F32),
        compiler_params=_cparams("parallel"),
        name="ffn",
    )(x2, g, wg, wu, wd)


def _block_diag(w):
    heads, a, b = w.shape
    eye = jnp.eye(heads, dtype=w.dtype)
    return (eye[:, None, :, None] * w[:, :, None, :]).reshape(heads * a, heads * b)


def kernel(x, ln1_g, w_in, b_gate, conv_w, conv_b, lru_wa, lru_ba, lru_wi, lru_bi, lru_lambda, sg_ln_g, sg_ln_b, sg_w, sg_b, q_norm_g, k_norm_g, lambda_q1, lambda_k1, lambda_q2, lambda_k2, subln_g, rel_bias, w_pa, w_pb, w_pc, w_o, ln2_g, w_ff_gate, w_ff_up, w_ff_down):
    bsz, s, d = x.shape
    depth = w_in.shape[0]
    lru_w = conv_w.shape[-1]
    sg_wd = sg_ln_g.shape[-1]
    groups = sg_w.shape[1]
    heads = rel_bias.shape[1]
    wq = heads * 2 * QK_DIM
    assert w_in.shape[-1] == 2 * lru_w + 2 * sg_wd + 3 * wq + 3 * d and lru_w == sg_wd
    assert s % ATT_T == 0 and ATT_T % CHUNK == 0

    nd = -(-(MAX_DISTANCE // 2 + ATT_T) // ATT_T)
    bias = _bias_tiles(rel_bias, ATT_T, nd)
    bd = _block_diag(jnp.ones((wq // QK_DIM, QK_DIM, QK_DIM), BF16))
    w_in_b, w_pa_b, w_pb_b, w_pc_b, w_o_b = (w.astype(BF16) for w in (w_in, w_pa, w_pb, w_pc, w_o))
    wg_b, wu_b, wd_b = (w.astype(BF16) for w in (w_ff_gate, w_ff_up, w_ff_down))
    causal = jnp.tril(jnp.ones((SG_LEN, SG_LEN), bool))

    x2 = x.reshape(bsz * s, d)
    for l in range(depth):
        lam_init = 0.8 - 0.6 * math.exp(-0.3 * l)
        cvec = (-LRU_C * jax.nn.softplus(-lru_lambda[l]))[None]
        lru_params = (conv_w[l], conv_b[l][None], _block_diag(lru_wa[l]).astype(BF16), lru_ba[l][None],
                      _block_diag(lru_wi[l]).astype(BF16), lru_bi[l][None], cvec)
        ws = jnp.where(causal, sg_w[l], 0.0).astype(BF16)
        ws_cat = jnp.transpose(ws, (1, 0, 2)).reshape(SG_LEN, groups * SG_LEN)
        bs_full = jnp.repeat(sg_b[l].T, sg_wd // groups, axis=1)
        sgu_params = (sg_ln_g[l][None], sg_ln_b[l][None], ws_cat, bs_full)
        gq = jnp.tile(q_norm_g[l] * (QK_DIM ** -0.5 * LOG2E), wq // QK_DIM)[None]
        gk = jnp.tile(k_norm_g[l], wq // QK_DIM)[None]
        ya, yb, qt, kn, vt, gates = _front(x2, bsz, ln1_g[l][None], w_in_b, l, lru_params, sgu_params,
                                           (gq, gk, bd), heads, groups)
        lam4 = jnp.stack([lambda_q1[l], lambda_k1[l], lambda_q2[l], lambda_k2[l]])
        subg = jnp.broadcast_to(subln_g[l][:, None], (V_DIM, ATT_T))
        yc = _attn(qt, kn, vt, bias, lam4, subg, lam_init)

        x2 = _merge(x2, ya, yb, yc.reshape(bsz * s, -1), gates, b_gate[l].reshape(3, 1, d),
                    w_pa_b, w_pb_b, w_pc_b, w_o_b, l)
        x2 = _ffn(x2, ln2_g[l][None], wg_b, wu_b, wd_b, l)
    return x2.reshape(bsz, s, d)
```

```python
import functools
import math

import jax
import jax.numpy as jnp
from jax import lax
from jax.experimental import pallas as pl
from jax.experimental.pallas import tpu as pltpu

F32 = jnp.float32
BF16 = jnp.bfloat16

EPS = 1e-6
CHUNK = 64
LRU_C = 8.0
SG_LEN = 128
QK_DIM = 64
V_DIM = 128
V_AUG = V_DIM + 16
MAX_DISTANCE = 2048
NEG_BIG = -1e30
LOG2E = math.log2(math.e)

ATT_T = 512
VMEM_LIMIT = 56 * 1024 * 1024


def _cparams(*sem):
    return pltpu.CompilerParams(dimension_semantics=sem, vmem_limit_bytes=VMEM_LIMIT)


def _rms_norm_bf16(x, g):
    ms = jnp.mean(x * x, axis=-1, keepdims=True)
    return (x * lax.rsqrt(ms + EPS) * g).astype(BF16)


def _resident(shape, index_map):
    return pl.BlockSpec(shape, index_map, pipeline_mode=pl.Buffered(1))


def _lru_branch(xa, ga, cw_ref, cb_ref, wa_ref, ba_ref, wi_ref, bi_ref, c_ref, ext_ref, h_ref):
    ts, w = xa.shape
    ext_ref[8:8 + ts, :] = xa
    xc = (cb_ref[...] + ext_ref[5:5 + ts, :] * cw_ref[0:1, :] + ext_ref[6:6 + ts, :] * cw_ref[1:2, :]
          + ext_ref[7:7 + ts, :] * cw_ref[2:3, :] + xa * cw_ref[3:4, :])
    ext_ref[0:8, :] = ext_ref[ts:ts + 8, :]

    xcb = xc.astype(BF16)
    r = jax.nn.sigmoid(jnp.dot(xcb, wa_ref[...], preferred_element_type=F32) + ba_ref[...])
    i = jax.nn.sigmoid(jnp.dot(xcb, wi_ref[...], preferred_element_type=F32) + bi_ref[...])
    log_a = c_ref[...] * r
    a = jnp.exp(log_a)
    mult = jnp.sqrt(-jnp.tanh(log_a) * (a * a + 1.0))
    b = mult * (i * xc)

    sub = 8
    a3 = a.reshape(ts // sub, sub, w)
    b3 = b.reshape(ts // sub, sub, w)
    row = lax.broadcasted_iota(jnp.int32, a3.shape, 1)
    d = 1
    while d < sub:
        keep = row >= d
        a_sh = jnp.where(keep, pltpu.roll(a3, d, 1), 1.0)
        b_sh = jnp.where(keep, pltpu.roll(b3, d, 1), 0.0)
        b3 = a3 * b_sh + b3
        a3 = a3 * a_sh
        d *= 2
    carry = h_ref[...]
    groups = []
    for g in range(ts // sub):
        hg = a3[g] * carry + b3[g]
        groups.append(hg)
        carry = hg[sub - 1:sub, :]
    h_ref[...] = carry
    h = jnp.concatenate(groups, axis=0)
    return h * jax.nn.gelu(ga)


def _sgu_branch(u, v, g_ref, b_ref, ws_ref, bs_ref, groups):
    tr, w = v.shape
    gd = w // groups
    mu = jnp.mean(v, axis=-1, keepdims=True)
    vc = v - mu
    var = jnp.mean(vc * vc, axis=-1, keepdims=True)
    vn = (vc * lax.rsqrt(var + EPS) * g_ref[...] + b_ref[...]).astype(BF16)
    lane_grp = lax.broadcasted_iota(jnp.int32, (SG_LEN, w), 1) // gd
    ws = ws_ref[...]
    zero = jnp.zeros((SG_LEN, w), BF16)
    out = []
    for blk in range(tr // SG_LEN):
        vb = vn[blk * SG_LEN:(blk + 1) * SG_LEN]
        stacked = jnp.concatenate([jnp.where(lane_grp == g, vb, zero) for g in range(groups)], axis=0)
        out.append(jnp.dot(ws, stacked, preferred_element_type=F32) + bs_ref[...])
    return u * jnp.concatenate(out, axis=0)


def _group_rms(y, g_ref, bd_ref):
    ss = jnp.dot((y * y).astype(BF16), bd_ref[...], preferred_element_type=F32)
    return y * lax.rsqrt(ss * (1.0 / QK_DIM) + EPS) * g_ref[...]


def _front_body(x_ref, g1_ref, w_ref, cw_ref, cb_ref, wa_ref, ba_ref, wi_ref, bi_ref, c_ref,
                lng_ref, lnb_ref, ws_ref, bs_ref, gq_ref, gk_ref, bd_ref,
                ya_ref, yb_ref, qt_ref, kn_ref, vt_ref, gates_ref, ext_ref, h_ref, *,
                heads, groups, lru_w, wq):
    @pl.when(pl.program_id(1) == 0)
    def _():
        ext_ref[0:8, :] = jnp.zeros((8, lru_w), F32)
        h_ref[...] = jnp.zeros_like(h_ref)

    h = _rms_norm_bf16(x_ref[...], g1_ref[...])

    def proj(c0, c1):
        return jnp.dot(h, w_ref[:, c0:c1], preferred_element_type=F32)

    c0 = 4 * lru_w
    pa = proj(0, 2 * lru_w)
    ya_ref[...] = _lru_branch(pa[:, :lru_w], pa[:, lru_w:], cw_ref, cb_ref, wa_ref, ba_ref, wi_ref,
                              bi_ref, c_ref, ext_ref, h_ref).astype(ya_ref.dtype)
    pb = proj(2 * lru_w, 4 * lru_w)
    yb_ref[...] = _sgu_branch(pb[:, :lru_w], pb[:, lru_w:], lng_ref, lnb_ref, ws_ref, bs_ref,
                              groups).astype(yb_ref.dtype)
    qn = _group_rms(proj(c0, c0 + wq), gq_ref, bd_ref)
    kn_ref[...] = _group_rms(proj(c0 + wq, c0 + 2 * wq), gk_ref, bd_ref).astype(BF16)
    v = proj(c0 + 2 * wq, c0 + 3 * wq)
    for hd in range(heads):
        cols = slice(hd * V_DIM, (hd + 1) * V_DIM)
        qt_ref[hd] = qn[:, cols].T.astype(BF16)
        vt_ref[hd, :V_DIM, :] = v[:, cols].T.astype(BF16)
        vt_ref[hd, V_DIM:, :] = jnp.ones((V_AUG - V_DIM, v.shape[0]), BF16)
    gates_ref[...] = proj(c0 + 3 * wq, w_ref.shape[-1]).astype(gates_ref.dtype)


def _front(x2, bsz, g1, w, l, lru_params, sgu_params, qk_params, heads, groups, tm=512):
    n, d = x2.shape
    s = n // bsz
    nt = s // tm
    c = w.shape[-1]
    lru_w = lru_params[0].shape[-1]
    wq = heads * 2 * QK_DIM
    n_gate = c - 4 * lru_w - 3 * wq
    small = list(lru_params) + list(sgu_params) + list(qk_params)
    full = lambda a: pl.BlockSpec(a.shape, lambda b, t: (0,) * a.ndim)
    rows = lambda width: pl.BlockSpec((tm, width), lambda b, t: (b * nt + t, 0))
    tspec = lambda r: pl.BlockSpec((None, heads, r, tm), lambda b, t: (b, 0, 0, t))
    return pl.pallas_call(
        functools.partial(_front_body, heads=heads, groups=groups, lru_w=lru_w, wq=wq),
        grid=(bsz, nt),
        in_specs=[rows(d), pl.BlockSpec((1, d), lambda b, t: (0, 0)),
                  _resident((None, d, c), lambda b, t: (l, 0, 0))] + [full(a) for a in small],
        out_specs=[rows(lru_w), rows(lru_w), tspec(V_DIM),
                   pl.BlockSpec((None, tm, wq), lambda b, t: (b, t, 0)), tspec(V_AUG), rows(n_gate)],
        out_shape=[
            jax.ShapeDtypeStruct((n, lru_w), BF16),
            jax.ShapeDtypeStruct((n, lru_w), BF16),
            jax.ShapeDtypeStruct((bsz, heads, V_DIM, s), BF16),
            jax.ShapeDtypeStruct((bsz, s, wq), BF16),
            jax.ShapeDtypeStruct((bsz, heads, V_AUG, s), BF16),
            jax.ShapeDtypeStruct((n, n_gate), BF16),
        ],
        scratch_shapes=[pltpu.VMEM((tm + 8, lru_w), F32), pltpu.VMEM((1, lru_w), F32)],
        compiler_params=_cparams("parallel", "arbitrary"),
        name="front",
    )(x2, g1, w, *small)


def _attn_body(qt_ref, k_ref, vt_ref, bias_ref, lam_ref, subg_ref, o_ref, acc_ref, s_ref, *, t, nd,
               lam_init):
    qi = pl.program_id(2)
    qt = qt_ref[...]
    row = lax.broadcasted_iota(jnp.int32, qt.shape, 0)
    zero = jnp.zeros_like(qt)
    q1 = jnp.where(row < QK_DIM, qt, zero)
    q2 = jnp.where(row >= QK_DIM, qt, zero)
    acc_ref[...] = jnp.zeros_like(acc_ref)

    def scores(kj):
        k = k_ref[pl.ds(pl.multiple_of(kj * t, t), t), :]
        return (jnp.dot(k, q1, preferred_element_type=F32),
                jnp.dot(k, q2, preferred_element_type=F32))

    def colmax(s):
        return jnp.max(s, axis=0, keepdims=True)

    def consume(idx, kj, s, mc, m):
        vt = vt_ref[:, pl.ds(pl.multiple_of(kj * t, t), t)]
        mn = jnp.maximum(m, mc)
        p = jnp.exp2(s - mn).astype(BF16)
        acc_ref[idx] = acc_ref[idx] * jnp.exp2(m - mn) + jnp.dot(vt, p, preferred_element_type=F32)
        return mn

    n_far = jnp.maximum(qi - (nd - 1), 0)
    last_far = jnp.maximum(n_far - 1, 0)

    def produce(slot, kj, bias_idx=None):
        s1, s2 = scores(kj)
        if bias_idx is not None:
            b = bias_ref[bias_idx]
            s1 = s1 + b
            s2 = s2 + b
        s_ref[slot, 0, :, :t] = s1
        s_ref[slot, 1, :, :t] = s2
        return colmax(s1), colmax(s2)

    def consume_slot(slot, kj, c, st):
        return (consume(0, kj, s_ref[slot, 0, :, :t], c[0], st[0]),
                consume(1, kj, s_ref[slot, 1, :, :t], c[1], st[1]))

    neg = jnp.full((1, t), NEG_BIG, F32)
    st = (neg, neg)

    pending = None
    for d in range(nd - 1, -1, -1):
        slot = (d + 1) % 2
        kj = jnp.maximum(qi - d, 0)
        c = produce(slot, kj, jnp.where(qi >= d, d, nd))
        if pending is not None:
            st = consume_slot(*pending, st)
        pending = (slot, kj, c)

    def produce_far(slot, kj):
        return produce(slot, jnp.minimum(kj, last_far))

    def far_pair(i, carry):
        st, c0 = carry
        c1 = produce_far(1, 2 * i + 1)
        st = consume_slot(0, 2 * i, c0, st)
        c0 = produce_far(0, 2 * i + 2)
        st = consume_slot(1, 2 * i + 1, c1, st)
        return st, c0

    def far_quad(i, carry):
        return far_pair(2 * i + 1, far_pair(2 * i, carry))

    c0 = produce_far(0, 0)
    st = consume_slot(*pending, st)
    def far_oct(i, carry):
        return far_quad(2 * i + 1, far_quad(2 * i, carry))

    carry = lax.fori_loop(0, n_far // 8, far_oct, (st, c0))
    carry = lax.cond(n_far % 8 >= 4, lambda: far_quad(n_far // 8 * 2, carry), lambda: carry)
    st, c0 = lax.cond(n_far % 4 >= 2, lambda: far_pair(n_far // 4 * 2, carry), lambda: carry)

    @pl.when(n_far % 2 == 1)
    def _():
        consume_slot(0, last_far, c0, st)

    lam4 = lam_ref[...]
    lam = (jnp.exp(jnp.sum(lam4[0:1] * lam4[1:2], axis=-1, keepdims=True))
           - jnp.exp(jnp.sum(lam4[2:3] * lam4[3:4], axis=-1, keepdims=True)) + lam_init)
    o = (acc_ref[0, :V_DIM, :] / acc_ref[0, V_DIM:V_DIM + 1, :]
         - lam * (acc_ref[1, :V_DIM, :] / acc_ref[1, V_DIM:V_DIM + 1, :]))
    ms = jnp.mean(o * o, axis=0, keepdims=True)
    o = o * lax.rsqrt(ms + EPS) * subg_ref[...] * (1.0 - lam_init)
    o_ref[...] = o.T.astype(o_ref.dtype)


def _attn(qt, kn, vt, bias, lam4, subg, lam_init, t=ATT_T):
    bsz, heads, _, s = qt.shape
    nd = bias.shape[1] - 1
    return pl.pallas_call(
        functools.partial(_attn_body, t=t, nd=nd, lam_init=lam_init),
        grid=(bsz, heads, s // t),
        in_specs=[
            pl.BlockSpec((None, None, V_DIM, t), lambda b, h, i: (b, h, 0, i)),
            pl.BlockSpec((None, s, V_DIM), lambda b, h, i: (b, 0, h)),
            pl.BlockSpec((None, None, V_AUG, s), lambda b, h, i: (b, h, 0, 0)),
            pl.BlockSpec((None, nd + 1, t, t), lambda b, h, i: (h, 0, 0, 0)),
            pl.BlockSpec(lam4.shape, lambda b, h, i: (0, 0)),
            pl.BlockSpec(subg.shape, lambda b, h, i: (0, 0)),
        ],
        out_specs=pl.BlockSpec((None, t, V_DIM), lambda b, h, i: (b, i, h)),
        out_shape=jax.ShapeDtypeStruct((bsz, s, heads * V_DIM), BF16),
        scratch_shapes=[pltpu.VMEM((2, V_AUG, t), F32), pltpu.VMEM((2, 2, t, t + 128), F32)],
        compiler_params=_cparams("parallel", "parallel", "arbitrary"),
        name="diff_attn",
    )(qt, kn, vt, bias, lam4, subg)


def _t5_bucket(rel, n_buckets):
    half = n_buckets // 2
    max_exact = half // 2
    ret = jnp.where(rel > 0, half, 0)
    n = jnp.abs(rel)
    nf = jnp.maximum(n, 1).astype(F32)
    large = max_exact + (jnp.log(nf / max_exact) / math.log(MAX_DISTANCE / max_exact)
                         * (half - max_exact)).astype(jnp.int32)
    large = jnp.minimum(large, half - 1)
    return ret + jnp.where(n < max_exact, n, large)


def _bias_body(f_ref, o_ref, *, t, nd):
    x = jnp.broadcast_to(f_ref[...], (t, 2 * t))
    y = pltpu.roll(x, t + 1, 1, stride=1, stride_axis=0)[:, :t]
    j = lax.broadcasted_iota(jnp.int32, (t, t), 0)
    i = lax.broadcasted_iota(jnp.int32, (t, t), 1)
    delta = pl.program_id(1)
    allowed = (((j // CHUNK) <= (i // CHUNK)) | (delta > 0)) & (delta < nd)
    o_ref[...] = jnp.where(allowed, y, NEG_BIG)


def _bias_tiles(rel_bias, t, nd):
    n_buckets, heads = rel_bias.shape
    c = jnp.arange(2 * t, dtype=jnp.int32)[None, :]
    delta = jnp.arange(nd + 1, dtype=jnp.int32)[:, None]
    rel = (t - 1 - c) - delta * t
    f = (rel_bias[_t5_bucket(rel, n_buckets)] - rel_bias[n_buckets // 2 - 1]) * LOG2E
    f = jnp.transpose(f, (2, 0, 1))[:, :, None, :]
    return pl.pallas_call(
        functools.partial(_bias_body, t=t, nd=nd),
        grid=(heads, nd + 1),
        in_specs=[pl.BlockSpec((None, None, 1, 2 * t), lambda h, d: (h, d, 0, 0))],
        out_specs=pl.BlockSpec((None, None, t, t), lambda h, d: (h, d, 0, 0)),
        out_shape=jax.ShapeDtypeStruct((heads, nd + 1, t, t), F32),
        compiler_params=_cparams("parallel", "parallel"),
        name="bias_tiles",
    )(f)


def _merge_body(x_ref, ya_ref, yb_ref, yc_ref, gates_ref, bg_ref, wpa_ref, wpb_ref, wpc_ref, wo_ref,
                o_ref):
    d = x_ref.shape[-1]

    def branch(idx, y_ref, w_ref):
        p = jnp.dot(y_ref[...], w_ref[...], preferred_element_type=F32)
        g = gates_ref[:, idx * d:(idx + 1) * d].astype(F32)
        return jax.nn.sigmoid(g + bg_ref[idx]) * p

    merged = branch(0, ya_ref, wpa_ref) + branch(1, yb_ref, wpb_ref) + branch(2, yc_ref, wpc_ref)
    o_ref[...] = x_ref[...] + jnp.dot(merged.astype(BF16), wo_ref[...], preferred_element_type=F32)


def _merge(x2, ya, yb, yc, gates, bg, wpa, wpb, wpc, wo, l, tm=1024):
    n, d = x2.shape
    rows = lambda a: pl.BlockSpec((tm, a.shape[-1]), lambda i: (i, 0))
    wspec = lambda a: pl.BlockSpec((None,) + a.shape[1:], lambda i: (l, 0, 0))
    return pl.pallas_call(
        _merge_body,
        grid=(n // tm,),
        in_specs=[rows(x2), rows(ya), rows(yb), rows(yc), rows(gates),
                  pl.BlockSpec(bg.shape, lambda i: (0, 0, 0)),
                  wspec(wpa), wspec(wpb), wspec(wpc), wspec(wo)],
        out_specs=pl.BlockSpec((tm, d), lambda i: (i, 0)),
        out_shape=jax.ShapeDtypeStruct((n, d), F32),
        compiler_params=_cparams("parallel"),
        name="merge",
    )(x2, ya, yb, yc, gates, bg, wpa, wpb, wpc, wo)


def _ffn_body(x_ref, g_ref, wg_ref, wu_ref, wd_ref, o_ref):
    x = x_ref[...]
    h = _rms_norm_bf16(x, g_ref[...])
    hid = wg_ref.shape[-1]
    cut = -(-(hid // 2) // 256) * 256
    out = x
    for lo, hi in ((0, cut), (cut, hid)):
        gate = jnp.dot(h, wg_ref[:, lo:hi], preferred_element_type=F32)
        up = jnp.dot(h, wu_ref[:, lo:hi], preferred_element_type=F32)
        act = (jax.nn.silu(gate) * up).astype(BF16)
        out = out + jnp.dot(act, wd_ref[lo:hi, :], preferred_element_type=F32)
    o_ref[...] = out


def _ffn(x2, g, wg, wu, wd, l, tm=1024):
    n, d = x2.shape
    hid = wg.shape[-1]
    return pl.pallas_call(
        _ffn_body,
        grid=(n // tm,),
        in_specs=[
            pl.BlockSpec((tm, d), lambda i: (i, 0)),
            pl.BlockSpec((1, d), lambda i: (0, 0)),
            _resident((None, d, hid), lambda i: (l, 0, 0)),
            _resident((None, d, hid), lambda i: (l, 0, 0)),
            _resident((None, hid, d), lambda i: (l, 0, 0)),
        ],
        out_specs=pl.BlockSpec((tm, d), lambda i: (i, 0)),
        out_shape=jax.ShapeDtypeStruct((n, d), F32),
        compiler_params=_cparams("parallel"),
        name="ffn",
    )(x2, g, wg, wu, wd)


def _block_diag(w):
    heads, a, b = w.shape
    eye = jnp.eye(heads, dtype=w.dtype)
    return (eye[:, None, :, None] * w[:, :, None, :]).reshape(heads * a, heads * b)


def kernel(x, ln1_g, w_in, b_gate, conv_w, conv_b, lru_wa, lru_ba, lru_wi, lru_bi, lru_lambda, sg_ln_g, sg_ln_b, sg_w, sg_b, q_norm_g, k_norm_g, lambda_q1, lambda_k1, lambda_q2, lambda_k2, subln_g, rel_bias, w_pa, w_pb, w_pc, w_o, ln2_g, w_ff_gate, w_ff_up, w_ff_down):
    bsz, s, d = x.shape
    depth = w_in.shape[0]
    lru_w = conv_w.shape[-1]
    sg_wd = sg_ln_g.shape[-1]
    groups = sg_w.shape[1]
    heads = rel_bias.shape[1]
    wq = heads * 2 * QK_DIM
    assert w_in.shape[-1] == 2 * lru_w + 2 * sg_wd + 3 * wq + 3 * d and lru_w == sg_wd
    assert s % ATT_T == 0 and ATT_T % CHUNK == 0

    nd = -(-(MAX_DISTANCE // 2 + ATT_T) // ATT_T)
    bias = _bias_tiles(rel_bias, ATT_T, nd)
    bd = _block_diag(jnp.ones((wq // QK_DIM, QK_DIM, QK_DIM), BF16))
    w_in_b, w_pa_b, w_pb_b, w_pc_b, w_o_b = (w.astype(BF16) for w in (w_in, w_pa, w_pb, w_pc, w_o))
    wg_b, wu_b, wd_b = (w.astype(BF16) for w in (w_ff_gate, w_ff_up, w_ff_down))
    causal = jnp.tril(jnp.ones((SG_LEN, SG_LEN), bool))

    x2 = x.reshape(bsz * s, d)
    for l in range(depth):
        lam_init = 0.8 - 0.6 * math.exp(-0.3 * l)
        cvec = (-LRU_C * jax.nn.softplus(-lru_lambda[l]))[None]
        lru_params = (conv_w[l], conv_b[l][None], _block_diag(lru_wa[l]).astype(BF16), lru_ba[l][None],
                      _block_diag(lru_wi[l]).astype(BF16), lru_bi[l][None], cvec)
        ws = jnp.where(causal, sg_w[l], 0.0).astype(BF16)
        ws_cat = jnp.transpose(ws, (1, 0, 2)).reshape(SG_LEN, groups * SG_LEN)
        bs_full = jnp.repeat(sg_b[l].T, sg_wd // groups, axis=1)
        sgu_params = (sg_ln_g[l][None], sg_ln_b[l][None], ws_cat, bs_full)
        gq = jnp.tile(q_norm_g[l] * (QK_DIM ** -0.5 * LOG2E), wq // QK_DIM)[None]
        gk = jnp.tile(k_norm_g[l], wq // QK_DIM)[None]
        ya, yb, qt, kn, vt, gates = _front(x2, bsz, ln1_g[l][None], w_in_b, l, lru_params, sgu_params,
                                           (gq, gk, bd), heads, groups)
        lam4 = jnp.stack([lambda_q1[l], lambda_k1[l], lambda_q2[l], lambda_k2[l]])
        subg = jnp.broadcast_to(subln_g[l][:, None], (V_DIM, ATT_T))
        yc = _attn(qt, kn, vt, bias, lam4, subg, lam_init)

        x2 = _merge(x2, ya, yb, yc.reshape(bsz * s, -1), gates, b_gate[l].reshape(3, 1, d),
                    w_pa_b, w_pb_b, w_pc_b, w_o_b, l)
        x2 = _ffn(x2, ln2_g[l][None], wg_b, wu_b, wd_b, l)
    return x2.reshape(bsz, s, d)
```

```python
import functools
import math

import jax
import jax.numpy as jnp
from jax import lax
from jax.experimental import pallas as pl
from jax.experimental.pallas import tpu as pltpu

F32 = jnp.float32
BF16 = jnp.bfloat16

EPS = 1e-6
CHUNK = 64
LRU_C = 8.0
SG_LEN = 128
QK_DIM = 64
V_DIM = 128
V_AUG = V_DIM + 16
MAX_DISTANCE = 2048
NEG_BIG = -1e30
LOG2E = math.log2(math.e)

ATT_T = 512
VMEM_LIMIT = 56 * 1024 * 1024


def _cparams(*sem):
    return pltpu.CompilerParams(dimension_semantics=sem, vmem_limit_bytes=VMEM_LIMIT)


def _rms_norm_bf16(x, g):
    ms = jnp.mean(x * x, axis=-1, keepdims=True)
    return (x * lax.rsqrt(ms + EPS) * g).astype(BF16)


def _resident(shape, index_map):
    return pl.BlockSpec(shape, index_map, pipeline_mode=pl.Buffered(1))


def _lru_branch(xa, ga, cw_ref, cb_ref, wa_ref, ba_ref, wi_ref, bi_ref, c_ref, ext_ref, h_ref):
    ts, w = xa.shape
    ext_ref[8:8 + ts, :] = xa
    xc = (cb_ref[...] + ext_ref[5:5 + ts, :] * cw_ref[0:1, :] + ext_ref[6:6 + ts, :] * cw_ref[1:2, :]
          + ext_ref[7:7 + ts, :] * cw_ref[2:3, :] + xa * cw_ref[3:4, :])
    ext_ref[0:8, :] = ext_ref[ts:ts + 8, :]

    xcb = xc.astype(BF16)
    r = jax.nn.sigmoid(jnp.dot(xcb, wa_ref[...], preferred_element_type=F32) + ba_ref[...])
    i = jax.nn.sigmoid(jnp.dot(xcb, wi_ref[...], preferred_element_type=F32) + bi_ref[...])
    log_a = c_ref[...] * r
    a = jnp.exp(log_a)
    mult = jnp.sqrt(-jnp.tanh(log_a) * (a * a + 1.0))
    b = mult * (i * xc)

    sub = 8
    a3 = a.reshape(ts // sub, sub, w)
    b3 = b.reshape(ts // sub, sub, w)
    row = lax.broadcasted_iota(jnp.int32, a3.shape, 1)
    d = 1
    while d < sub:
        keep = row >= d
        a_sh = jnp.where(keep, pltpu.roll(a3, d, 1), 1.0)
        b_sh = jnp.where(keep, pltpu.roll(b3, d, 1), 0.0)
        b3 = a3 * b_sh + b3
        a3 = a3 * a_sh
        d *= 2
    carry = h_ref[...]
    groups = []
    for g in range(ts // sub):
        hg = a3[g] * carry + b3[g]
        groups.append(hg)
        carry = hg[sub - 1:sub, :]
    h_ref[...] = carry
    h = jnp.concatenate(groups, axis=0)
    return h * jax.nn.gelu(ga)


def _sgu_branch(u, v, g_ref, b_ref, ws_ref, bs_ref, groups):
    tr, w = v.shape
    gd = w // groups
    mu = jnp.mean(v, axis=-1, keepdims=True)
    vc = v - mu
    var = jnp.mean(vc * vc, axis=-1, keepdims=True)
    vn = (vc * lax.rsqrt(var + EPS) * g_ref[...] + b_ref[...]).astype(BF16)
    lane_grp = lax.broadcasted_iota(jnp.int32, (SG_LEN, w), 1) // gd
    ws = ws_ref[...]
    zero = jnp.zeros((SG_LEN, w), BF16)
    out = []
    for blk in range(tr // SG_LEN):
        vb = vn[blk * SG_LEN:(blk + 1) * SG_LEN]
        stacked = jnp.concatenate([jnp.where(lane_grp == g, vb, zero) for g in range(groups)], axis=0)
        out.append(jnp.dot(ws, stacked, preferred_element_type=F32) + bs_ref[...])
    return u * jnp.concatenate(out, axis=0)


def _group_rms(y, g_ref, bd_ref):
    ss = jnp.dot((y * y).astype(BF16), bd_ref[...], preferred_element_type=F32)
    return y * lax.rsqrt(ss * (1.0 / QK_DIM) + EPS) * g_ref[...]


def _front_body(x_ref, g1_ref, w_ref, cw_ref, cb_ref, wa_ref, ba_ref, wi_ref, bi_ref, c_ref,
                lng_ref, lnb_ref, ws_ref, bs_ref, gq_ref, gk_ref, bd_ref,
                ya_ref, yb_ref, qt_ref, kn_ref, vt_ref, gates_ref, ext_ref, h_ref, *,
                heads, groups, lru_w, wq):
    @pl.when(pl.program_id(1) == 0)
    def _():
        ext_ref[0:8, :] = jnp.zeros((8, lru_w), F32)
        h_ref[...] = jnp.zeros_like(h_ref)

    h = _rms_norm_bf16(x_ref[...], g1_ref[...])

    def proj(c0, c1):
        return jnp.dot(h, w_ref[:, c0:c1], preferred_element_type=F32)

    c0 = 4 * lru_w
    pa = proj(0, 2 * lru_w)
    ya_ref[...] = _lru_branch(pa[:, :lru_w], pa[:, lru_w:], cw_ref, cb_ref, wa_ref, ba_ref, wi_ref,
                              bi_ref, c_ref, ext_ref, h_ref).astype(ya_ref.dtype)
    pb = proj(2 * lru_w, 4 * lru_w)
    yb_ref[...] = _sgu_branch(pb[:, :lru_w], pb[:, lru_w:], lng_ref, lnb_ref, ws_ref, bs_ref,
                              groups).astype(yb_ref.dtype)
    qn = _group_rms(proj(c0, c0 + wq), gq_ref, bd_ref)
    kn_ref[...] = _group_rms(proj(c0 + wq, c0 + 2 * wq), gk_ref, bd_ref).astype(BF16)
    v = proj(c0 + 2 * wq, c0 + 3 * wq)
    for hd in range(heads):
        cols = slice(hd * V_DIM, (hd + 1) * V_DIM)
        qt_ref[hd] = qn[:, cols].T.astype(BF16)
        vt_ref[hd, :V_DIM, :] = v[:, cols].T.astype(BF16)
        vt_ref[hd, V_DIM:, :] = jnp.ones((V_AUG - V_DIM, v.shape[0]), BF16)
    gates_ref[...] = proj(c0 + 3 * wq, w_ref.shape[-1]).astype(gates_ref.dtype)


def _front(x2, bsz, g1, w, l, lru_params, sgu_params, qk_params, heads, groups, tm=512):
    n, d = x2.shape
    s = n // bsz
    nt = s // tm
    c = w.shape[-1]
    lru_w = lru_params[0].shape[-1]
    wq = heads * 2 * QK_DIM
    n_gate = c - 4 * lru_w - 3 * wq
    small = list(lru_params) + list(sgu_params) + list(qk_params)
    full = lambda a: pl.BlockSpec(a.shape, lambda b, t: (0,) * a.ndim)
    rows = lambda width: pl.BlockSpec((tm, width), lambda b, t: (b * nt + t, 0))
    tspec = lambda r: pl.BlockSpec((None, heads, r, tm), lambda b, t: (b, 0, 0, t))
    return pl.pallas_call(
        functools.partial(_front_body, heads=heads, groups=groups, lru_w=lru_w, wq=wq),
        grid=(bsz, nt),
        in_specs=[rows(d), pl.BlockSpec((1, d), lambda b, t: (0, 0)),
                  _resident((None, d, c), lambda b, t: (l, 0, 0))] + [full(a) for a in small],
        out_specs=[rows(lru_w), rows(lru_w), tspec(V_DIM),
                   pl.BlockSpec((None, tm, wq), lambda b, t: (b, t, 0)), tspec(V_AUG), rows(n_gate)],
        out_shape=[
            jax.ShapeDtypeStruct((n, lru_w), BF16),
            jax.ShapeDtypeStruct((n, lru_w), BF16),
            jax.ShapeDtypeStruct((bsz, heads, V_DIM, s), BF16),
            jax.ShapeDtypeStruct((bsz, s, wq), BF16),
            jax.ShapeDtypeStruct((bsz, heads, V_AUG, s), BF16),
            jax.ShapeDtypeStruct((n, n_gate), BF16),
        ],
        scratch_shapes=[pltpu.VMEM((tm + 8, lru_w), F32), pltpu.VMEM((1, lru_w), F32)],
        compiler_params=_cparams("parallel", "arbitrary"),
        name="front",
    )(x2, g1, w, *small)


def _attn_body(qt_ref, k_ref, vt_ref, bias_ref, lam_ref, subg_ref, o_ref, acc_ref, s_ref, *, t, nd,
               lam_init):
    qi = pl.program_id(2)
    qt = qt_ref[...]
    row = lax.broadcasted_iota(jnp.int32, qt.shape, 0)
    zero = jnp.zeros_like(qt)
    q1 = jnp.where(row < QK_DIM, qt, zero)
    q2 = jnp.where(row >= QK_DIM, qt, zero)
    acc_ref[...] = jnp.zeros_like(acc_ref)

    def scores(kj):
        k = k_ref[pl.ds(pl.multiple_of(kj * t, t), t), :]
        return (jnp.dot(k, q1, preferred_element_type=F32),
                jnp.dot(k, q2, preferred_element_type=F32))

    def colmax(s):
        return jnp.max(s, axis=0, keepdims=True)

    def consume(idx, kj, s, mc, m):
        vt = vt_ref[:, pl.ds(pl.multiple_of(kj * t, t), t)]
        mn = jnp.maximum(m, mc)
        p = jnp.exp2(s - mn).astype(BF16)
        acc_ref[idx] = acc_ref[idx] * jnp.exp2(m - mn) + jnp.dot(vt, p, preferred_element_type=F32)
        return mn

    n_far = jnp.maximum(qi - (nd - 1), 0)
    last_far = jnp.maximum(n_far - 1, 0)

    def produce(slot, kj, bias_idx=None):
        s1, s2 = scores(kj)
        if bias_idx is not None:
            b = bias_ref[bias_idx]
            s1 = s1 + b
            s2 = s2 + b
        s_ref[slot, 0, :, :t] = s1
        s_ref[slot, 1, :, :t] = s2
        return colmax(s1), colmax(s2)

    def consume_slot(slot, kj, c, st):
        return (consume(0, kj, s_ref[slot, 0, :, :t], c[0], st[0]),
                consume(1, kj, s_ref[slot, 1, :, :t], c[1], st[1]))

    neg = jnp.full((1, t), NEG_BIG, F32)
    st = (neg, neg)

    pending = None
    for d in range(nd - 1, -1, -1):
        slot = (d + 1) % 2
        kj = jnp.maximum(qi - d, 0)
        c = produce(slot, kj, jnp.where(qi >= d, d, nd))
        if pending is not None:
            st = consume_slot(*pending, st)
        pending = (slot, kj, c)

    def produce_far(slot, kj):
        return produce(slot, jnp.minimum(kj, last_far))

    def far_pair(i, carry):
        st, c0 = carry
        c1 = produce_far(1, 2 * i + 1)
        st = consume_slot(0, 2 * i, c0, st)
        c0 = produce_far(0, 2 * i + 2)
        st = consume_slot(1, 2 * i + 1, c1, st)
        return st, c0

    def far_quad(i, carry):
        return far_pair(2 * i + 1, far_pair(2 * i, carry))

    c0 = produce_far(0, 0)
    st = consume_slot(*pending, st)
    def far_oct(i, carry):
        return far_quad(2 * i + 1, far_quad(2 * i, carry))

    def far_hex(i, carry):
        return far_oct(2 * i + 1, far_oct(2 * i, carry))

    carry = lax.fori_loop(0, n_far // 16, far_hex, (st, c0))
    carry = lax.cond(n_far % 16 >= 8, lambda: far_oct(n_far // 16 * 2, carry), lambda: carry)
    carry = lax.cond(n_far % 8 >= 4, lambda: far_quad(n_far // 8 * 2, carry), lambda: carry)
    st, c0 = lax.cond(n_far % 4 >= 2, lambda: far_pair(n_far // 4 * 2, carry), lambda: carry)

    @pl.when(n_far % 2 == 1)
    def _():
        consume_slot(0, last_far, c0, st)

    lam4 = lam_ref[...]
    lam = (jnp.exp(jnp.sum(lam4[0:1] * lam4[1:2], axis=-1, keepdims=True))
           - jnp.exp(jnp.sum(lam4[2:3] * lam4[3:4], axis=-1, keepdims=True)) + lam_init)
    o = (acc_ref[0, :V_DIM, :] / acc_ref[0, V_DIM:V_DIM + 1, :]
         - lam * (acc_ref[1, :V_DIM, :] / acc_ref[1, V_DIM:V_DIM + 1, :]))
    ms = jnp.mean(o * o, axis=0, keepdims=True)
    o = o * lax.rsqrt(ms + EPS) * subg_ref[...] * (1.0 - lam_init)
    o_ref[...] = o.T.astype(o_ref.dtype)


def _attn(qt, kn, vt, bias, lam4, subg, lam_init, t=ATT_T):
    bsz, heads, _, s = qt.shape
    nd = bias.shape[1] - 1
    return pl.pallas_call(
        functools.partial(_attn_body, t=t, nd=nd, lam_init=lam_init),
        grid=(bsz, heads, s // t),
        in_specs=[
            pl.BlockSpec((None, None, V_DIM, t), lambda b, h, i: (b, h, 0, i)),
            pl.BlockSpec((None, s, V_DIM), lambda b, h, i: (b, 0, h)),
            pl.BlockSpec((None, None, V_AUG, s), lambda b, h, i: (b, h, 0, 0)),
            pl.BlockSpec((None, nd + 1, t, t), lambda b, h, i: (h, 0, 0, 0)),
            pl.BlockSpec(lam4.shape, lambda b, h, i: (0, 0)),
            pl.BlockSpec(subg.shape, lambda b, h, i: (0, 0)),
        ],
        out_specs=pl.BlockSpec((None, t, V_DIM), lambda b, h, i: (b, i, h)),
        out_shape=jax.ShapeDtypeStruct((bsz, s, heads * V_DIM), BF16),
        scratch_shapes=[pltpu.VMEM((2, V_AUG, t), F32), pltpu.VMEM((2, 2, t, t + 128), F32)],
        compiler_params=_cparams("parallel", "parallel", "arbitrary"),
        name="diff_attn",
    )(qt, kn, vt, bias, lam4, subg)


def _t5_bucket(rel, n_buckets):
    half = n_buckets // 2
    max_exact = half // 2
    ret = jnp.where(rel > 0, half, 0)
    n = jnp.abs(rel)
    nf = jnp.maximum(n, 1).astype(F32)
    large = max_exact + (jnp.log(nf / max_exact) / math.log(MAX_DISTANCE / max_exact)
                         * (half - max_exact)).astype(jnp.int32)
    large = jnp.minimum(large, half - 1)
    return ret + jnp.where(n < max_exact, n, large)


def _bias_body(f_ref, o_ref, *, t, nd):
    x = jnp.broadcast_to(f_ref[...], (t, 2 * t))
    y = pltpu.roll(x, t + 1, 1, stride=1, stride_axis=0)[:, :t]
    j = lax.broadcasted_iota(jnp.int32, (t, t), 0)
    i = lax.broadcasted_iota(jnp.int32, (t, t), 1)
    delta = pl.program_id(1)
    allowed = (((j // CHUNK) <= (i // CHUNK)) | (delta > 0)) & (delta < nd)
    o_ref[...] = jnp.where(allowed, y, NEG_BIG)


def _bias_tiles(rel_bias, t, nd):
    n_buckets, heads = rel_bias.shape
    c = jnp.arange(2 * t, dtype=jnp.int32)[None, :]
    delta = jnp.arange(nd + 1, dtype=jnp.int32)[:, None]
    rel = (t - 1 - c) - delta * t
    f = (rel_bias[_t5_bucket(rel, n_buckets)] - rel_bias[n_buckets // 2 - 1]) * LOG2E
    f = jnp.transpose(f, (2, 0, 1))[:, :, None, :]
    return pl.pallas_call(
        functools.partial(_bias_body, t=t, nd=nd),
        grid=(heads, nd + 1),
        in_specs=[pl.BlockSpec((None, None, 1, 2 * t), lambda h, d: (h, d, 0, 0))],
        out_specs=pl.BlockSpec((None, None, t, t), lambda h, d: (h, d, 0, 0)),
        out_shape=jax.ShapeDtypeStruct((heads, nd + 1, t, t), F32),
        compiler_params=_cparams("parallel", "parallel"),
        name="bias_tiles",
    )(f)


def _merge_body(x_ref, ya_ref, yb_ref, yc_ref, gates_ref, bg_ref, wpa_ref, wpb_ref, wpc_ref, wo_ref,
                o_ref):
    d = x_ref.shape[-1]

    def branch(idx, y_ref, w_ref):
        p = jnp.dot(y_ref[...], w_ref[...], preferred_element_type=F32)
        g = gates_ref[:, idx * d:(idx + 1) * d].astype(F32)
        return jax.nn.sigmoid(g + bg_ref[idx]) * p

    merged = branch(0, ya_ref, wpa_ref) + branch(1, yb_ref, wpb_ref) + branch(2, yc_ref, wpc_ref)
    o_ref[...] = x_ref[...] + jnp.dot(merged.astype(BF16), wo_ref[...], preferred_element_type=F32)


def _merge(x2, ya, yb, yc, gates, bg, wpa, wpb, wpc, wo, l, tm=1024):
    n, d = x2.shape
    rows = lambda a: pl.BlockSpec((tm, a.shape[-1]), lambda i: (i, 0))
    wspec = lambda a: pl.BlockSpec((None,) + a.shape[1:], lambda i: (l, 0, 0))
    return pl.pallas_call(
        _merge_body,
        grid=(n // tm,),
        in_specs=[rows(x2), rows(ya), rows(yb), rows(yc), rows(gates),
                  pl.BlockSpec(bg.shape, lambda i: (0, 0, 0)),
                  wspec(wpa), wspec(wpb), wspec(wpc), wspec(wo)],
        out_specs=pl.BlockSpec((tm, d), lambda i: (i, 0)),
        out_shape=jax.ShapeDtypeStruct((n, d), F32),
        compiler_params=_cparams("parallel"),
        name="merge",
    )(x2, ya, yb, yc, gates, bg, wpa, wpb, wpc, wo)


def _ffn_body(x_ref, g_ref, wg_ref, wu_ref, wd_ref, o_ref):
    x = x_ref[...]
    h = _rms_norm_bf16(x, g_ref[...])
    hid = wg_ref.shape[-1]
    cut = -(-(hid // 2) // 256) * 256
    out = x
    for lo, hi in ((0, cut), (cut, hid)):
        gate = jnp.dot(h, wg_ref[:, lo:hi], preferred_element_type=F32)
        up = jnp.dot(h, wu_ref[:, lo:hi], preferred_element_type=F32)
        act = (jax.nn.silu(gate) * up).astype(BF16)
        out = out + jnp.dot(act, wd_ref[lo:hi, :], preferred_element_type=F32)
    o_ref[...] = out


def _ffn(x2, g, wg, wu, wd, l, tm=1024):
    n, d = x2.shape
    hid = wg.shape[-1]
    return pl.pallas_call(
        _ffn_body,
        grid=(n // tm,),
        in_specs=[
            pl.BlockSpec((tm, d), lambda i: (i, 0)),
            pl.BlockSpec((1, d), lambda i: (0, 0)),
            _resident((None, d, hid), lambda i: (l, 0, 0)),
            _resident((None, d, hid), lambda i: (l, 0, 0)),
            _resident((None, hid, d), lambda i: (l, 0, 0)),
        ],
        out_specs=pl.BlockSpec((tm, d), lambda i: (i, 0)),
        out_shape=jax.ShapeDtypeStruct((n, d), F32),
        compiler_params=_cparams("parallel"),
        name="ffn",
    )(x2, g, wg, wu, wd)


def _block_diag(w):
    heads, a, b = w.shape
    eye = jnp.eye(heads, dtype=w.dtype)
    return (eye[:, None, :, None] * w[:, :, None, :]).reshape(heads * a, heads * b)


def kernel(x, ln1_g, w_in, b_gate, conv_w, conv_b, lru_wa, lru_ba, lru_wi, lru_bi, lru_lambda, sg_ln_g, sg_ln_b, sg_w, sg_b, q_norm_g, k_norm_g, lambda_q1, lambda_k1, lambda_q2, lambda_k2, subln_g, rel_bias, w_pa, w_pb, w_pc, w_o, ln2_g, w_ff_gate, w_ff_up, w_ff_down):
    bsz, s, d = x.shape
    depth = w_in.shape[0]
    lru_w = conv_w.shape[-1]
    sg_wd = sg_ln_g.shape[-1]
    groups = sg_w.shape[1]
    heads = rel_bias.shape[1]
    wq = heads * 2 * QK_DIM
    assert w_in.shape[-1] == 2 * lru_w + 2 * sg_wd + 3 * wq + 3 * d and lru_w == sg_wd
    assert s % ATT_T == 0 and ATT_T % CHUNK == 0

    nd = -(-(MAX_DISTANCE // 2 + ATT_T) // ATT_T)
    bias = _bias_tiles(rel_bias, ATT_T, nd)
    bd = _block_diag(jnp.ones((wq // QK_DIM, QK_DIM, QK_DIM), BF16))
    w_in_b, w_pa_b, w_pb_b, w_pc_b, w_o_b = (w.astype(BF16) for w in (w_in, w_pa, w_pb, w_pc, w_o))
    wg_b, wu_b, wd_b = (w.astype(BF16) for w in (w_ff_gate, w_ff_up, w_ff_down))
    causal = jnp.tril(jnp.ones((SG_LEN, SG_LEN), bool))

    x2 = x.reshape(bsz * s, d)
    for l in range(depth):
        lam_init = 0.8 - 0.6 * math.exp(-0.3 * l)
        cvec = (-LRU_C * jax.nn.softplus(-lru_lambda[l]))[None]
        lru_params = (conv_w[l], conv_b[l][None], _block_diag(lru_wa[l]).astype(BF16), lru_ba[l][None],
                      _block_diag(lru_wi[l]).astype(BF16), lru_bi[l][None], cvec)
        ws = jnp.where(causal, sg_w[l], 0.0).astype(BF16)
        ws_cat = jnp.transpose(ws, (1, 0, 2)).reshape(SG_LEN, groups * SG_LEN)
        bs_full = jnp.repeat(sg_b[l].T, sg_wd // groups, axis=1)
        sgu_params = (sg_ln_g[l][None], sg_ln_b[l][None], ws_cat, bs_full)
        gq = jnp.tile(q_norm_g[l] * (QK_DIM ** -0.5 * LOG2E), wq // QK_DIM)[None]
        gk = jnp.tile(k_norm_g[l], wq // QK_DIM)[None]
        ya, yb, qt, kn, vt, gates = _front(x2, bsz, ln1_g[l][None], w_in_b, l, lru_params, sgu_params,
                                           (gq, gk, bd), heads, groups)
        lam4 = jnp.stack([lambda_q1[l], lambda_k1[l], lambda_q2[l], lambda_k2[l]])
        subg = jnp.broadcast_to(subln_g[l][:, None], (V_DIM, ATT_T))
        yc = _attn(qt, kn, vt, bias, lam4, subg, lam_init)

        x2 = _merge(x2, ya, yb, yc.reshape(bsz * s, -1), gates, b_gate[l].reshape(3, 1, d),
                    w_pa_b, w_pb_b, w_pc_b, w_o_b, l)
        x2 = _ffn(x2, ln2_g[l][None], wg_b, wu_b, wd_b, l)
    return x2.reshape(bsz, s, d)
```

```python
import functools
import math

import jax
import jax.numpy as jnp
from jax import lax
from jax.experimental import pallas as pl
from jax.experimental.pallas import tpu as pltpu

F32 = jnp.float32
BF16 = jnp.bfloat16

EPS = 1e-6
CHUNK = 64
LRU_C = 8.0
SG_LEN = 128
QK_DIM = 64
V_DIM = 128
V_AUG = V_DIM + 16
MAX_DISTANCE = 2048
NEG_BIG = -1e30
LOG2E = math.log2(math.e)

ATT_T = 512
VMEM_LIMIT = 56 * 1024 * 1024


def _cparams(*sem):
    return pltpu.CompilerParams(dimension_semantics=sem, vmem_limit_bytes=VMEM_LIMIT)


def _rms_norm_bf16(x, g):
    ms = jnp.mean(x * x, axis=-1, keepdims=True)
    return (x * lax.rsqrt(ms + EPS) * g).astype(BF16)


def _resident(shape, index_map):
    return pl.BlockSpec(shape, index_map, pipeline_mode=pl.Buffered(1))


def _lru_branch(xa, ga, cols, cw_ref, cb_ref, wa_ref, ba_ref, wi_ref, bi_ref, c_ref, ext_ref, h_ref):
    ts, w = xa.shape
    ext_ref[8:8 + ts, cols] = xa
    xc = (cb_ref[:, cols] + ext_ref[5:5 + ts, cols] * cw_ref[0:1, cols]
          + ext_ref[6:6 + ts, cols] * cw_ref[1:2, cols]
          + ext_ref[7:7 + ts, cols] * cw_ref[2:3, cols] + xa * cw_ref[3:4, cols])
    ext_ref[0:8, cols] = ext_ref[ts:ts + 8, cols]

    xcb = xc.astype(BF16)
    r = jax.nn.sigmoid(jnp.dot(xcb, wa_ref[cols, cols], preferred_element_type=F32) + ba_ref[:, cols])
    i = jax.nn.sigmoid(jnp.dot(xcb, wi_ref[cols, cols], preferred_element_type=F32) + bi_ref[:, cols])
    log_a = c_ref[:, cols] * r
    a = jnp.exp(log_a)
    mult = jnp.sqrt(-jnp.tanh(log_a) * (a * a + 1.0))
    b = mult * (i * xc)

    sub = 8
    a3 = a.reshape(ts // sub, sub, w)
    b3 = b.reshape(ts // sub, sub, w)
    row = lax.broadcasted_iota(jnp.int32, a3.shape, 1)
    d = 1
    while d < sub:
        keep = row >= d
        a_sh = jnp.where(keep, pltpu.roll(a3, d, 1), 1.0)
        b_sh = jnp.where(keep, pltpu.roll(b3, d, 1), 0.0)
        b3 = a3 * b_sh + b3
        a3 = a3 * a_sh
        d *= 2
    carry = h_ref[:, cols]
    groups = []
    for g in range(ts // sub):
        hg = a3[g] * carry + b3[g]
        groups.append(hg)
        carry = hg[sub - 1:sub, :]
    h_ref[:, cols] = carry
    h = jnp.concatenate(groups, axis=0)
    return h * jax.nn.gelu(ga)


def _sgu_branch(u, v, g_ref, b_ref, ws_ref, bs_ref, groups):
    tr, w = v.shape
    gd = w // groups
    mu = jnp.mean(v, axis=-1, keepdims=True)
    vc = v - mu
    var = jnp.mean(vc * vc, axis=-1, keepdims=True)
    vn = (vc * lax.rsqrt(var + EPS) * g_ref[...] + b_ref[...]).astype(BF16)
    lane_grp = lax.broadcasted_iota(jnp.int32, (SG_LEN, w), 1) // gd
    ws = ws_ref[...]
    zero = jnp.zeros((SG_LEN, w), BF16)
    out = []
    for blk in range(tr // SG_LEN):
        vb = vn[blk * SG_LEN:(blk + 1) * SG_LEN]
        stacked = jnp.concatenate([jnp.where(lane_grp == g, vb, zero) for g in range(groups)], axis=0)
        out.append(jnp.dot(ws, stacked, preferred_element_type=F32) + bs_ref[...])
    return u * jnp.concatenate(out, axis=0)


def _group_rms(y, g_ref, bd_ref):
    ss = jnp.dot((y * y).astype(BF16), bd_ref[...], preferred_element_type=F32)
    return y * lax.rsqrt(ss * (1.0 / QK_DIM) + EPS) * g_ref[...]


def _front_body(x_ref, g1_ref, w_ref, cw_ref, cb_ref, wa_ref, ba_ref, wi_ref, bi_ref, c_ref,
                lng_ref, lnb_ref, ws_ref, bs_ref, gq_ref, gk_ref, bd_ref,
                ya_ref, yb_ref, qt_ref, kn_ref, vt_ref, gates_ref, ext_ref, h_ref, *,
                heads, groups, lru_w, wq):
    @pl.when(pl.program_id(1) == 0)
    def _():
        ext_ref[0:8, :] = jnp.zeros((8, lru_w), F32)
        h_ref[...] = jnp.zeros_like(h_ref)

    h = _rms_norm_bf16(x_ref[...], g1_ref[...])

    def proj(c0, c1):
        return jnp.dot(h, w_ref[:, c0:c1], preferred_element_type=F32)

    c0 = 4 * lru_w
    pa = proj(0, 2 * lru_w)
    half = lru_w // 2
    for cols in (slice(0, half), slice(half, lru_w)):
        xa, ga = pa[:, cols], pa[:, lru_w + cols.start:lru_w + cols.stop]
        ya_ref[:, cols] = _lru_branch(xa, ga, cols, cw_ref, cb_ref, wa_ref, ba_ref, wi_ref, bi_ref,
                                      c_ref, ext_ref, h_ref).astype(ya_ref.dtype)
    pb = proj(2 * lru_w, 4 * lru_w)
    yb_ref[...] = _sgu_branch(pb[:, :lru_w], pb[:, lru_w:], lng_ref, lnb_ref, ws_ref, bs_ref,
                              groups).astype(yb_ref.dtype)
    qn = _group_rms(proj(c0, c0 + wq), gq_ref, bd_ref)
    kn_ref[...] = _group_rms(proj(c0 + wq, c0 + 2 * wq), gk_ref, bd_ref).astype(BF16)
    v = proj(c0 + 2 * wq, c0 + 3 * wq)
    for hd in range(heads):
        cols = slice(hd * V_DIM, (hd + 1) * V_DIM)
        qt_ref[hd] = qn[:, cols].T.astype(BF16)
        vt_ref[hd, :V_DIM, :] = v[:, cols].T.astype(BF16)
        vt_ref[hd, V_DIM:, :] = jnp.ones((V_AUG - V_DIM, v.shape[0]), BF16)
    gates_ref[...] = proj(c0 + 3 * wq, w_ref.shape[-1]).astype(gates_ref.dtype)


def _front(x2, bsz, g1, w, l, lru_params, sgu_params, qk_params, heads, groups, tm=512):
    n, d = x2.shape
    s = n // bsz
    nt = s // tm
    c = w.shape[-1]
    lru_w = lru_params[0].shape[-1]
    wq = heads * 2 * QK_DIM
    n_gate = c - 4 * lru_w - 3 * wq
    small = list(lru_params) + list(sgu_params) + list(qk_params)
    full = lambda a: pl.BlockSpec(a.shape, lambda b, t: (0,) * a.ndim)
    rows = lambda width: pl.BlockSpec((tm, width), lambda b, t: (b * nt + t, 0))
    tspec = lambda r: pl.BlockSpec((None, heads, r, tm), lambda b, t: (b, 0, 0, t))
    return pl.pallas_call(
        functools.partial(_front_body, heads=heads, groups=groups, lru_w=lru_w, wq=wq),
        grid=(bsz, nt),
        in_specs=[rows(d), pl.BlockSpec((1, d), lambda b, t: (0, 0)),
                  _resident((None, d, c), lambda b, t: (l, 0, 0))] + [full(a) for a in small],
        out_specs=[rows(lru_w), rows(lru_w), tspec(V_DIM),
                   pl.BlockSpec((None, tm, wq), lambda b, t: (b, t, 0)), tspec(V_AUG), rows(n_gate)],
        out_shape=[
            jax.ShapeDtypeStruct((n, lru_w), BF16),
            jax.ShapeDtypeStruct((n, lru_w), BF16),
            jax.ShapeDtypeStruct((bsz, heads, V_DIM, s), BF16),
            jax.ShapeDtypeStruct((bsz, s, wq), BF16),
            jax.ShapeDtypeStruct((bsz, heads, V_AUG, s), BF16),
            jax.ShapeDtypeStruct((n, n_gate), BF16),
        ],
        scratch_shapes=[pltpu.VMEM((tm + 8, lru_w), F32), pltpu.VMEM((1, lru_w), F32)],
        compiler_params=_cparams("parallel", "arbitrary"),
        name="front",
    )(x2, g1, w, *small)


def _attn_body(qt_ref, k_ref, vt_ref, bias_ref, lam_ref, subg_ref, o_ref, acc_ref, s_ref, *, t, nd,
               lam_init):
    qi = pl.program_id(2)
    qt = qt_ref[...]
    row = lax.broadcasted_iota(jnp.int32, qt.shape, 0)
    zero = jnp.zeros_like(qt)
    q1 = jnp.where(row < QK_DIM, qt, zero)
    q2 = jnp.where(row >= QK_DIM, qt, zero)
    acc_ref[...] = jnp.zeros_like(acc_ref)

    def scores(kj):
        k = k_ref[pl.ds(pl.multiple_of(kj * t, t), t), :]
        return (jnp.dot(k, q1, preferred_element_type=F32),
                jnp.dot(k, q2, preferred_element_type=F32))

    def colmax(s):
        return jnp.max(s, axis=0, keepdims=True)

    def consume(idx, kj, s, mc, m):
        vt = vt_ref[:, pl.ds(pl.multiple_of(kj * t, t), t)]
        mn = jnp.maximum(m, mc)
        p = jnp.exp2(s - mn).astype(BF16)
        acc_ref[idx] = acc_ref[idx] * jnp.exp2(m - mn) + jnp.dot(vt, p, preferred_element_type=F32)
        return mn

    n_far = jnp.maximum(qi - (nd - 1), 0)
    last_far = jnp.maximum(n_far - 1, 0)

    def produce(slot, kj, bias_idx=None):
        s1, s2 = scores(kj)
        if bias_idx is not None:
            b = bias_ref[bias_idx]
            s1 = s1 + b
            s2 = s2 + b
        s_ref[slot, 0, :, :t] = s1
        s_ref[slot, 1, :, :t] = s2
        return colmax(s1), colmax(s2)

    def consume_slot(slot, kj, c, st):
        return (consume(0, kj, s_ref[slot, 0, :, :t], c[0], st[0]),
                consume(1, kj, s_ref[slot, 1, :, :t], c[1], st[1]))

    neg = jnp.full((1, t), NEG_BIG, F32)
    st = (neg, neg)

    pending = None
    for d in range(nd - 1, -1, -1):
        slot = (d + 1) % 2
        kj = jnp.maximum(qi - d, 0)
        c = produce(slot, kj, jnp.where(qi >= d, d, nd))
        if pending is not None:
            st = consume_slot(*pending, st)
        pending = (slot, kj, c)

    def produce_far(slot, kj):
        return produce(slot, jnp.minimum(kj, last_far))

    def far_pair(i, carry):
        st, c0 = carry
        c1 = produce_far(1, 2 * i + 1)
        st = consume_slot(0, 2 * i, c0, st)
        c0 = produce_far(0, 2 * i + 2)
        st = consume_slot(1, 2 * i + 1, c1, st)
        return st, c0

    def far_quad(i, carry):
        return far_pair(2 * i + 1, far_pair(2 * i, carry))

    c0 = produce_far(0, 0)
    st = consume_slot(*pending, st)
    def far_oct(i, carry):
        return far_quad(2 * i + 1, far_quad(2 * i, carry))

    carry = lax.fori_loop(0, n_far // 8, far_oct, (st, c0))
    carry = lax.cond(n_far % 8 >= 4, lambda: far_quad(n_far // 8 * 2, carry), lambda: carry)
    st, c0 = lax.cond(n_far % 4 >= 2, lambda: far_pair(n_far // 4 * 2, carry), lambda: carry)

    @pl.when(n_far % 2 == 1)
    def _():
        consume_slot(0, last_far, c0, st)

    lam4 = lam_ref[...]
    lam = (jnp.exp(jnp.sum(lam4[0:1] * lam4[1:2], axis=-1, keepdims=True))
           - jnp.exp(jnp.sum(lam4[2:3] * lam4[3:4], axis=-1, keepdims=True)) + lam_init)
    o = (acc_ref[0, :V_DIM, :] / acc_ref[0, V_DIM:V_DIM + 1, :]
         - lam * (acc_ref[1, :V_DIM, :] / acc_ref[1, V_DIM:V_DIM + 1, :]))
    ms = jnp.mean(o * o, axis=0, keepdims=True)
    o = o * lax.rsqrt(ms + EPS) * subg_ref[...] * (1.0 - lam_init)
    o_ref[...] = o.T.astype(o_ref.dtype)


def _attn(qt, kn, vt, bias, lam4, subg, lam_init, t=ATT_T):
    bsz, heads, _, s = qt.shape
    nd = bias.shape[1] - 1
    return pl.pallas_call(
        functools.partial(_attn_body, t=t, nd=nd, lam_init=lam_init),
        grid=(bsz, heads, s // t),
        in_specs=[
            pl.BlockSpec((None, None, V_DIM, t), lambda b, h, i: (b, h, 0, i)),
            pl.BlockSpec((None, s, V_DIM), lambda b, h, i: (b, 0, h)),
            pl.BlockSpec((None, None, V_AUG, s), lambda b, h, i: (b, h, 0, 0)),
            pl.BlockSpec((None, nd + 1, t, t), lambda b, h, i: (h, 0, 0, 0)),
            pl.BlockSpec(lam4.shape, lambda b, h, i: (0, 0)),
            pl.BlockSpec(subg.shape, lambda b, h, i: (0, 0)),
        ],
        out_specs=pl.BlockSpec((None, t, V_DIM), lambda b, h, i: (b, i, h)),
        out_shape=jax.ShapeDtypeStruct((bsz, s, heads * V_DIM), BF16),
        scratch_shapes=[pltpu.VMEM((2, V_AUG, t), F32), pltpu.VMEM((2, 2, t, t + 128), F32)],
        compiler_params=_cparams("parallel", "parallel", "arbitrary"),
        name="diff_attn",
    )(qt, kn, vt, bias, lam4, subg)


def _t5_bucket(rel, n_buckets):
    half = n_buckets // 2
    max_exact = half // 2
    ret = jnp.where(rel > 0, half, 0)
    n = jnp.abs(rel)
    nf = jnp.maximum(n, 1).astype(F32)
    large = max_exact + (jnp.log(nf / max_exact) / math.log(MAX_DISTANCE / max_exact)
                         * (half - max_exact)).astype(jnp.int32)
    large = jnp.minimum(large, half - 1)
    return ret + jnp.where(n < max_exact, n, large)


def _bias_body(f_ref, o_ref, *, t, nd):
    x = jnp.broadcast_to(f_ref[...], (t, 2 * t))
    y = pltpu.roll(x, t + 1, 1, stride=1, stride_axis=0)[:, :t]
    j = lax.broadcasted_iota(jnp.int32, (t, t), 0)
    i = lax.broadcasted_iota(jnp.int32, (t, t), 1)
    delta = pl.program_id(1)
    allowed = (((j // CHUNK) <= (i // CHUNK)) | (delta > 0)) & (delta < nd)
    o_ref[...] = jnp.where(allowed, y, NEG_BIG)


def _bias_tiles(rel_bias, t, nd):
    n_buckets, heads = rel_bias.shape
    c = jnp.arange(2 * t, dtype=jnp.int32)[None, :]
    delta = jnp.arange(nd + 1, dtype=jnp.int32)[:, None]
    rel = (t - 1 - c) - delta * t
    f = (rel_bias[_t5_bucket(rel, n_buckets)] - rel_bias[n_buckets // 2 - 1]) * LOG2E
    f = jnp.transpose(f, (2, 0, 1))[:, :, None, :]
    return pl.pallas_call(
        functools.partial(_bias_body, t=t, nd=nd),
        grid=(heads, nd + 1),
        in_specs=[pl.BlockSpec((None, None, 1, 2 * t), lambda h, d: (h, d, 0, 0))],
        out_specs=pl.BlockSpec((None, None, t, t), lambda h, d: (h, d, 0, 0)),
        out_shape=jax.ShapeDtypeStruct((heads, nd + 1, t, t), F32),
        compiler_params=_cparams("parallel", "parallel"),
        name="bias_tiles",
    )(f)


def _merge_body(x_ref, ya_ref, yb_ref, yc_ref, gates_ref, bg_ref, wpa_ref, wpb_ref, wpc_ref, wo_ref,
                o_ref):
    d = x_ref.shape[-1]

    def branch(idx, y_ref, w_ref):
        p = jnp.dot(y_ref[...], w_ref[...], preferred_element_type=F32)
        g = gates_ref[:, idx * d:(idx + 1) * d].astype(F32)
        return jax.nn.sigmoid(g + bg_ref[idx]) * p

    merged = branch(0, ya_ref, wpa_ref) + branch(1, yb_ref, wpb_ref) + branch(2, yc_ref, wpc_ref)
    o_ref[...] = x_ref[...] + jnp.dot(merged.astype(BF16), wo_ref[...], preferred_element_type=F32)


def _merge(x2, ya, yb, yc, gates, bg, wpa, wpb, wpc, wo, l, tm=1024):
    n, d = x2.shape
    rows = lambda a: pl.BlockSpec((tm, a.shape[-1]), lambda i: (i, 0))
    wspec = lambda a: pl.BlockSpec((None,) + a.shape[1:], lambda i: (l, 0, 0))
    return pl.pallas_call(
        _merge_body,
        grid=(n // tm,),
        in_specs=[rows(x2), rows(ya), rows(yb), rows(yc), rows(gates),
                  pl.BlockSpec(bg.shape, lambda i: (0, 0, 0)),
                  wspec(wpa), wspec(wpb), wspec(wpc), wspec(wo)],
        out_specs=pl.BlockSpec((tm, d), lambda i: (i, 0)),
        out_shape=jax.ShapeDtypeStruct((n, d), F32),
        compiler_params=_cparams("parallel"),
        name="merge",
    )(x2, ya, yb, yc, gates, bg, wpa, wpb, wpc, wo)


def _ffn_body(x_ref, g_ref, wg_ref, wu_ref, wd_ref, o_ref):
    x = x_ref[...]
    h = _rms_norm_bf16(x, g_ref[...])
    hid = wg_ref.shape[-1]
    cut = -(-(hid // 2) // 256) * 256
    out = x
    for lo, hi in ((0, cut), (cut, hid)):
        gate = jnp.dot(h, wg_ref[:, lo:hi], preferred_element_type=F32)
        up = jnp.dot(h, wu_ref[:, lo:hi], preferred_element_type=F32)
        act = (jax.nn.silu(gate) * up).astype(BF16)
        out = out + jnp.dot(act, wd_ref[lo:hi, :], preferred_element_type=F32)
    o_ref[...] = out


def _ffn(x2, g, wg, wu, wd, l, tm=1024):
    n, d = x2.shape
    hid = wg.shape[-1]
    return pl.pallas_call(
        _ffn_body,
        grid=(n // tm,),
        in_specs=[
            pl.BlockSpec((tm, d), lambda i: (i, 0)),
            pl.BlockSpec((1, d), lambda i: (0, 0)),
            _resident((None, d, hid), lambda i: (l, 0, 0)),
            _resident((None, d, hid), lambda i: (l, 0, 0)),
            _resident((None, hid, d), lambda i: (l, 0, 0)),
        ],
        out_specs=pl.BlockSpec((tm, d), lambda i: (i, 0)),
        out_shape=jax.ShapeDtypeStruct((n, d), F32),
        compiler_params=_cparams("parallel"),
        name="ffn",
    )(x2, g, wg, wu, wd)


def _block_diag(w):
    heads, a, b = w.shape
    eye = jnp.eye(heads, dtype=w.dtype)
    return (eye[:, None, :, None] * w[:, :, None, :]).reshape(heads * a, heads * b)


def kernel(x, ln1_g, w_in, b_gate, conv_w, conv_b, lru_wa, lru_ba, lru_wi, lru_bi, lru_lambda, sg_ln_g, sg_ln_b, sg_w, sg_b, q_norm_g, k_norm_g, lambda_q1, lambda_k1, lambda_q2, lambda_k2, subln_g, rel_bias, w_pa, w_pb, w_pc, w_o, ln2_g, w_ff_gate, w_ff_up, w_ff_down):
    bsz, s, d = x.shape
    depth = w_in.shape[0]
    lru_w = conv_w.shape[-1]
    sg_wd = sg_ln_g.shape[-1]
    groups = sg_w.shape[1]
    heads = rel_bias.shape[1]
    wq = heads * 2 * QK_DIM
    assert w_in.shape[-1] == 2 * lru_w + 2 * sg_wd + 3 * wq + 3 * d and lru_w == sg_wd
    assert s % ATT_T == 0 and ATT_T % CHUNK == 0

    nd = -(-(MAX_DISTANCE // 2 + ATT_T) // ATT_T)
    bias = _bias_tiles(rel_bias, ATT_T, nd)
    bd = _block_diag(jnp.ones((wq // QK_DIM, QK_DIM, QK_DIM), BF16))
    w_in_b, w_pa_b, w_pb_b, w_pc_b, w_o_b = (w.astype(BF16) for w in (w_in, w_pa, w_pb, w_pc, w_o))
    wg_b, wu_b, wd_b = (w.astype(BF16) for w in (w_ff_gate, w_ff_up, w_ff_down))
    causal = jnp.tril(jnp.ones((SG_LEN, SG_LEN), bool))

    x2 = x.reshape(bsz * s, d)
    for l in range(depth):
        lam_init = 0.8 - 0.6 * math.exp(-0.3 * l)
        cvec = (-LRU_C * jax.nn.softplus(-lru_lambda[l]))[None]
        lru_params = (conv_w[l], conv_b[l][None], _block_diag(lru_wa[l]).astype(BF16), lru_ba[l][None],
                      _block_diag(lru_wi[l]).astype(BF16), lru_bi[l][None], cvec)
        ws = jnp.where(causal, sg_w[l], 0.0).astype(BF16)
        ws_cat = jnp.transpose(ws, (1, 0, 2)).reshape(SG_LEN, groups * SG_LEN)
        bs_full = jnp.repeat(sg_b[l].T, sg_wd // groups, axis=1)
        sgu_params = (sg_ln_g[l][None], sg_ln_b[l][None], ws_cat, bs_full)
        gq = jnp.tile(q_norm_g[l] * (QK_DIM ** -0.5 * LOG2E), wq // QK_DIM)[None]
        gk = jnp.tile(k_norm_g[l], wq // QK_DIM)[None]
        ya, yb, qt, kn, vt, gates = _front(x2, bsz, ln1_g[l][None], w_in_b, l, lru_params, sgu_params,
                                           (gq, gk, bd), heads, groups)
        lam4 = jnp.stack([lambda_q1[l], lambda_k1[l], lambda_q2[l], lambda_k2[l]])
        subg = jnp.broadcast_to(subln_g[l][:, None], (V_DIM, ATT_T))
        yc = _attn(qt, kn, vt, bias, lam4, subg, lam_init)

        x2 = _merge(x2, ya, yb, yc.reshape(bsz * s, -1), gates, b_gate[l].reshape(3, 1, d),
                    w_pa_b, w_pb_b, w_pc_b, w_o_b, l)
        x2 = _ffn(x2, ln2_g[l][None], wg_b, wu_b, wd_b, l)
    return x2.reshape(bsz, s, d)
```
